```python
import math
import jax, jax.numpy as jnp
from jax import lax
import numpy as np

D_MODEL = 1024
BATCH = 2
SEQ = 16384
DEPTH = 2

GRID_W = 64
CTX_LEN = 256
RMS_EPS = 1e-6
SCAN_CHUNK = 128
S5_WIDTH = 512
S5_GROUP = 16
S5_GROUPS = S5_WIDTH // S5_GROUP
S5_STATE = 64
RET_HEADS = 4
RET_HEAD_DIM = 128
RET_WIDTH = RET_HEADS * RET_HEAD_DIM
EVEN_IN = S5_WIDTH + 4 * RET_WIDTH
EVEN_MIX = S5_WIDTH + RET_WIDTH
ROPE_BASE = 10000.0
NA_HEADS = 16
NA_HEAD_DIM = 64
NA_WIDTH = NA_HEADS * NA_HEAD_DIM
NA_WIN_ROWS = 8
NA_WIN_COLS = 16
N_EXPERTS = 32
MOE_TOP_K = 4
MOE_FF = D_MODEL
SWIGLU_LIMIT = 7.0
SWIGLU_ALPHA = 1.702
MOE_BLOCK = 128

kernel_name = "hybrid_s5_retention_natten_moe_block"

F32 = jnp.float32


def rms_norm(x, w):
    xf = x.astype(F32)
    y = xf * lax.rsqrt(jnp.mean(xf * xf, axis=-1, keepdims=True) + RMS_EPS)
    return (y * w.astype(F32)).astype(x.dtype)


def modulate(h, shift, scale):
    return h * (1 + scale) + shift


def rope_1d(x, pos):
    half = x.shape[-1] // 2
    freq = ROPE_BASE ** (-jnp.arange(half, dtype=F32) / half)
    ang = pos[:, None] * freq[None, :]
    cos, sin = jnp.cos(ang), jnp.sin(ang)
    xf = x.astype(F32)
    x1, x2 = xf[..., :half], xf[..., half:]
    return jnp.concatenate([x1 * cos - x2 * sin, x1 * sin + x2 * cos], axis=-1).astype(x.dtype)


def axial_rope_2d(x):
    t = jnp.arange(x.shape[2])
    row = (t // GRID_W).astype(F32)
    col = (t % GRID_W).astype(F32)
    h = x.shape[-1] // 2
    return jnp.concatenate([rope_1d(x[..., :h], row), rope_1d(x[..., h:], col)], axis=-1)


def s5_discretize(lam_re, lam_im, log_step, b_re, b_im):
    lam_re = jnp.minimum(lam_re.astype(F32), -1e-4)
    lam_im = lam_im.astype(F32)
    step = jnp.exp(log_step.astype(F32))[:, None]
    mag = jnp.exp(lam_re * step)
    ang = lam_im * step
    a_re, a_im = mag * jnp.cos(ang), mag * jnp.sin(ang)
    den = lam_re * lam_re + lam_im * lam_im
    n_re, n_im = a_re - 1.0, a_im
    co_re = (n_re * lam_re + n_im * lam_im) / den
    co_im = (n_im * lam_re - n_re * lam_im) / den
    b_re, b_im = b_re.astype(F32), b_im.astype(F32)
    bb_re = co_re[..., None] * b_re - co_im[..., None] * b_im
    bb_im = co_re[..., None] * b_im + co_im[..., None] * b_re
    return a_re, a_im, bb_re, bb_im


def complex_linear_combine(e1, e2):
    a1r, a1i, b1r, b1i = e1
    a2r, a2i, b2r, b2i = e2
    return (a2r * a1r - a2i * a1i, a2r * a1i + a2i * a1r,
            a2r * b1r - a2i * b1i + b2r, a2r * b1i + a2i * b1r + b2i)


def s5_scan(u, a_re, a_im, bb_re, bb_im, c_re, c_im, h_re, h_im):
    bsz, seq_len, g, i = u.shape
    n = seq_len // SCAN_CHUNK
    uc = u.reshape(bsz, n, SCAN_CHUNK, g, i).transpose(1, 0, 2, 3, 4)

    def step(carry, u_blk):
        hr, hi = carry
        bu_re = jnp.einsum('bcgi,gpi->bcgp', u_blk, bb_re)
        bu_im = jnp.einsum('bcgi,gpi->bcgp', u_blk, bb_im)
        bu_re = bu_re.at[:, 0].add(a_re * hr - a_im * hi)
        bu_im = bu_im.at[:, 0].add(a_re * hi + a_im * hr)
        ar = jnp.broadcast_to(a_re, bu_re.shape)
        ai = jnp.broadcast_to(a_im, bu_im.shape)
        _, _, sr, si = lax.associative_scan(complex_linear_combine, (ar, ai, bu_re, bu_im), axis=1)
        y = jnp.einsum('bcgp,gip->bcgi', sr, c_re) - jnp.einsum('bcgp,gip->bcgi', si, c_im)
        return (sr[:, -1], si[:, -1]), y

    (hr, hi), y = lax.scan(step, (h_re, h_im), uc)
    return y.transpose(1, 0, 2, 3, 4).reshape(bsz, seq_len, g, i), hr, hi


def s5_mixer(u_lat, u_ctx, lam_re, lam_im, log_step, b_re, b_im, c_re, c_im, d_skip, w_glu):
    def grouped(u):
        return u.astype(F32).reshape(u.shape[0], u.shape[1], S5_GROUPS, S5_GROUP)
    ul, uc = grouped(u_lat), grouped(u_ctx)
    zeros = jnp.zeros((ul.shape[0], S5_GROUPS, S5_STATE), F32)
    d = d_skip.astype(F32).reshape(S5_GROUPS, S5_GROUP)
    y_lat, y_ctx = ul * d, uc * d
    for dr in range(2):
        a_re, a_im, bb_re, bb_im = s5_discretize(lam_re[dr], lam_im[dr], log_step[dr], b_re[dr], b_im[dr])
        cr, ci = c_re[dr].astype(F32), c_im[dr].astype(F32)
        flip = (lambda t: jnp.flip(t, axis=1)) if dr == 1 else (lambda t: t)
        yc, hr, hi = s5_scan(flip(uc), a_re, a_im, bb_re, bb_im, cr, ci, zeros, zeros)
        yl, _, _ = s5_scan(flip(ul), a_re, a_im, bb_re, bb_im, cr, ci, hr, hi)
        y_ctx = y_ctx + flip(yc)
        y_lat = y_lat + flip(yl)

    def glu(y, like):
        y = jax.nn.gelu(y.reshape(y.shape[0], y.shape[1], S5_WIDTH))
        return (y * jax.nn.sigmoid(y @ w_glu.astype(F32))).astype(like.dtype)
    return glu(y_lat, u_lat), glu(y_ctx, u_ctx)


def retention_scan(q, k, v, log_g, s0):
    bsz, h, seq_len, _ = q.shape
    dv = v.shape[-1]
    c = SCAN_CHUNK
    n = seq_len // c
    idx = jnp.arange(c, dtype=F32)
    diff = idx[:, None] - idx[None, :]
    inner = jnp.where(diff >= 0, jnp.exp(log_g[:, None, None] * jnp.maximum(diff, 0.0)), 0.0)
    q_dec = jnp.exp(log_g[:, None] * (idx + 1.0))[..., None]
    k_dec = jnp.exp(log_g[:, None] * (c - 1.0 - idx))[..., None]
    blk_dec = jnp.exp(log_g * c)[:, None, None]

    def chunks(t):
        return t.reshape(bsz, h, n, c, t.shape[-1]).transpose(2, 0, 1, 3, 4)

    def step(s, blk):
        qb, kb, vb = blk
        att = jnp.einsum('bhid,bhjd->bhij', qb, kb) * inner
        o = jnp.einsum('bhij,bhje->bhie', att, vb) + jnp.einsum('bhid,bhde->bhie', qb, s) * q_dec
        s = blk_dec * s + jnp.einsum('bhjd,bhje->bhde', kb * k_dec, vb)
        return s, o

    s, o = lax.scan(step, s0, (chunks(q), chunks(k), chunks(v)))
    return o.transpose(1, 2, 0, 3, 4).reshape(bsz, h, seq_len, dv), s


def retention_mixer(q_lat, k_lat, v_lat, g_lat, q_ctx, k_ctx, v_ctx, g_ctx, log_decay):
    def heads(t):
        return t.reshape(t.shape[0], t.shape[1], RET_HEADS, RET_HEAD_DIM).transpose(0, 2, 1, 3).astype(F32)
    scale = RET_HEAD_DIM ** -0.5
    ql = axial_rope_2d(heads(q_lat))
    kl = axial_rope_2d(heads(k_lat)) * scale
    qc, kc = heads(q_ctx), heads(k_ctx) * scale
    vl, vc = heads(v_lat), heads(v_ctx)
    s0 = jnp.zeros((ql.shape[0], RET_HEADS, RET_HEAD_DIM, RET_HEAD_DIM), F32)
    o_lat = jnp.zeros_like(vl)
    o_ctx = jnp.zeros_like(vc)
    for dr in range(2):
        lg = log_decay[dr].astype(F32)
        flip = (lambda t: jnp.flip(t, axis=2)) if dr == 1 else (lambda t: t)
        oc, sc = retention_scan(flip(qc), flip(kc), flip(vc), lg, s0)
        ol, _ = retention_scan(flip(ql), flip(kl), flip(vl), lg, sc)
        o_ctx = o_ctx + flip(oc)
        o_lat = o_lat + flip(ol)

    def finish(o, g):
        o = o.transpose(0, 2, 1, 3)
        o = o * lax.rsqrt(jnp.mean(o * o, axis=-1, keepdims=True) + RMS_EPS)
        o = o.reshape(o.shape[0], o.shape[1], RET_WIDTH)
        return (o * jax.nn.silu(g.astype(F32))).astype(g.dtype)
    return finish(o_lat, g_lat), finish(o_ctx, g_ctx)


def even_mixer(h_lat, h_ctx, w_in, w_out, lam_re, lam_im, log_step, b_re, b_im, c_re, c_im, d_skip, w_glu, log_decay):
    def parts(p):
        u = p[..., :S5_WIDTH]
        q, k, v, g = jnp.split(p[..., S5_WIDTH:], 4, axis=-1)
        return u, q, k, v, g
    ul, ql, kl, vl, gl = parts(h_lat @ w_in)
    uc, qc, kc, vc, gc = parts(h_ctx @ w_in)
    a_lat, a_ctx = s5_mixer(ul, uc, lam_re, lam_im, log_step, b_re, b_im, c_re, c_im, d_skip, w_glu)
    r_lat, r_ctx = retention_mixer(ql, kl, vl, gl, qc, kc, vc, gc, log_decay)
    o_lat = jnp.concatenate([a_lat, r_lat], axis=-1) @ w_out
    o_ctx = jnp.concatenate([a_ctx, r_ctx], axis=-1) @ w_out
    return o_lat, o_ctx


def na_mixer(h_lat, h_ctx, w_qkv, w_o, rpb, need_ctx_out):
    bsz, seq_len, _ = h_lat.shape
    rows = seq_len // GRID_W
    kr, kc = min(NA_WIN_ROWS, rows), NA_WIN_COLS
    scale = NA_HEAD_DIM ** -0.5
    qkv = (h_lat @ w_qkv).reshape(bsz, rows, GRID_W, 3, NA_HEADS, NA_HEAD_DIM)
    qg = qkv[:, :, :, 0].transpose(0, 3, 1, 2, 4) * scale
    kg = qkv[:, :, :, 1].transpose(0, 3, 1, 2, 4)
    vg = qkv[:, :, :, 2].transpose(0, 3, 1, 2, 4)
    n_ctx = h_ctx.shape[1]
    cqkv = (h_ctx @ w_qkv).reshape(bsz, n_ctx, 3, NA_HEADS, NA_HEAD_DIM)
    q_ctx = cqkv[:, :, 0].transpose(0, 2, 1, 3) * scale
    k_ctx = cqkv[:, :, 1].transpose(0, 2, 1, 3)
    v_ctx = cqkv[:, :, 2].transpose(0, 2, 1, 3)

    col = jnp.arange(GRID_W)
    col_start = jnp.clip(col - kc // 2, 0, GRID_W - kc)
    col_idx = col_start[:, None] + jnp.arange(kc)[None, :]
    col_off = col_idx - col[:, None] + (NA_WIN_COLS - 1)

    def row_block(r):
        rs = jnp.clip(r - kr // 2, 0, rows - kr)
        q_r = lax.dynamic_index_in_dim(qg, r, axis=2, keepdims=False)
        k_band = lax.dynamic_slice_in_dim(kg, rs, kr, axis=2)
        v_band = lax.dynamic_slice_in_dim(vg, rs, kr, axis=2)
        k_win = k_band[:, :, :, col_idx]
        v_win = v_band[:, :, :, col_idx]
        row_off = rs + jnp.arange(kr) - r + (NA_WIN_ROWS - 1)
        bias = rpb[:, row_off[None, :, None], col_off[:, None, :]]
        s_loc = jnp.einsum('bhqd,bhrqcd->bhqrc', q_r, k_win).astype(F32) + bias.astype(F32)
        s_ctx = jnp.einsum('bhqd,bhkd->bhqk', q_r, k_ctx).astype(F32)
        s = jnp.concatenate([s_loc.reshape(bsz, NA_HEADS, GRID_W, kr * kc), s_ctx], axis=-1)
        p = jax.nn.softmax(s, axis=-1).astype(v_win.dtype)
        p_loc = p[..., :kr * kc].reshape(bsz, NA_HEADS, GRID_W, kr, kc)
        p_ctx = p[..., kr * kc:]
        return (jnp.einsum('bhqrc,bhrqcd->bhqd', p_loc, v_win)
                + jnp.einsum('bhqk,bhkd->bhqd', p_ctx, v_ctx))

    o = lax.map(row_block, jnp.arange(rows))
    o_lat = o.transpose(1, 0, 3, 2, 4).reshape(bsz, seq_len, NA_WIDTH) @ w_o
    if not need_ctx_out:
        return o_lat, None
    s = jnp.einsum('bhqd,bhkd->bhqk', q_ctx, k_ctx).astype(F32)
    p = jax.nn.softmax(s, axis=-1).astype(v_ctx.dtype)
    o_ctx = jnp.einsum('bhqk,bhkd->bhqd', p, v_ctx).transpose(0, 2, 1, 3).reshape(bsz, n_ctx, NA_WIDTH) @ w_o
    return o_lat, o_ctx


def moe_ffn(h, w_router, b_router, w1, b1, w2, b2):
    n_tok, d = h.shape
    logits = (h @ w_router + b_router).astype(F32)
    top_val, top_idx = lax.top_k(logits, MOE_TOP_K)
    gates = jax.nn.softmax(top_val, axis=-1)
    n_assign = n_tok * MOE_TOP_K
    flat_e = top_idx.reshape(-1)
    order = jnp.argsort(flat_e)
    sorted_e = flat_e[order]
    counts = jnp.bincount(flat_e, length=N_EXPERTS)
    starts = jnp.cumsum(counts) - counts
    padded = (counts + MOE_BLOCK - 1) // MOE_BLOCK * MOE_BLOCK
    padded_ends = jnp.cumsum(padded)
    padded_starts = padded_ends - padded
    dest = padded_starts[sorted_e] + jnp.arange(n_assign) - starts[sorted_e]
    n_blocks = -(-n_assign // MOE_BLOCK) + N_EXPERTS
    n_slots = n_blocks * MOE_BLOCK
    slot_tok = jnp.full((n_slots,), n_tok, jnp.int32).at[dest].set((order // MOE_TOP_K).astype(jnp.int32))
    slot_gate = jnp.zeros((n_slots,), F32).at[dest].set(gates.reshape(-1)[order])
    block_e = jnp.minimum(jnp.searchsorted(padded_ends, jnp.arange(n_blocks) * MOE_BLOCK, side='right'),
                          N_EXPERTS - 1)
    h_pad = jnp.concatenate([h, jnp.zeros((1, d), h.dtype)], axis=0)

    def run_block(args):
        tok, e = args
        xb = h_pad[tok]
        t = xb @ w1[e] + b1[e]
        x_glu = jnp.minimum(t[:, :MOE_FF], SWIGLU_LIMIT)
        x_lin = jnp.clip(t[:, MOE_FF:], -SWIGLU_LIMIT, SWIGLU_LIMIT)
        act = x_glu * jax.nn.sigmoid(SWIGLU_ALPHA * x_glu) * (x_lin + 1)
        return act @ w2[e] + b2[e]

    y = lax.map(run_block, (slot_tok.reshape(n_blocks, MOE_BLOCK), block_e))
    y = y.reshape(n_slots, d) * slot_gate[:, None].astype(y.dtype)
    out = jnp.zeros((n_tok + 1, d), y.dtype).at[slot_tok].add(y)
    return out[:n_tok]


def setup_inputs(seed: int = 0) -> dict:
    key = jax.random.key(seed)
    ks = jax.random.split(key, 32)
    n_even = (DEPTH + 1) // 2
    n_odd = DEPTH // 2
    nrm = jax.random.normal
    d = D_MODEL
    lam_im0 = math.pi * jnp.arange(S5_STATE, dtype=F32)
    decay0 = jnp.asarray(np.log(1.0 - 2.0 ** (-5.0 - np.arange(RET_HEADS))), F32)
    return {
        'x': nrm(ks[0], (BATCH, SEQ, d), F32),
        'c': nrm(ks[1], (BATCH, d), F32),
        'ctx': nrm(ks[2], (BATCH, CTX_LEN, d), F32),
        'c_ctx': nrm(ks[3], (d,), F32),
        'ada_w': nrm(ks[4], (DEPTH, d, 6 * d), F32) * (0.5 * d ** -0.5),
        'ada_b': 0.02 * nrm(ks[5], (DEPTH, 6 * d), F32),
        'norm_w': 1.0 + 0.02 * nrm(ks[6], (DEPTH, 2, d), F32),
        'final_norm_w': 1.0 + 0.02 * nrm(ks[7], (d,), F32),
        'ev_w_in': nrm(ks[8], (n_even, d, EVEN_IN), F32) * d ** -0.5,
        'ev_w_out': nrm(ks[9], (n_even, EVEN_MIX, d), F32) * EVEN_MIX ** -0.5,
        's5_lam_re': -0.5 + 0.01 * nrm(ks[10], (n_even, 2, S5_GROUPS, S5_STATE), F32),
        's5_lam_im': lam_im0 + 0.01 * nrm(ks[11], (n_even, 2, S5_GROUPS, S5_STATE), F32),
        's5_log_step': jax.random.uniform(ks[12], (n_even, 2, S5_GROUPS), F32, math.log(1e-3), math.log(1e-1)),
        's5_b_re': nrm(ks[13], (n_even, 2, S5_GROUPS, S5_STATE, S5_GROUP), F32) * (2 * S5_GROUP) ** -0.5,
        's5_b_im': nrm(ks[14], (n_even, 2, S5_GROUPS, S5_STATE, S5_GROUP), F32) * (2 * S5_GROUP) ** -0.5,
        's5_c_re': nrm(ks[15], (n_even, 2, S5_GROUPS, S5_GROUP, S5_STATE), F32) * (2 * S5_STATE) ** -0.5,
        's5_c_im': nrm(ks[16], (n_even, 2, S5_GROUPS, S5_GROUP, S5_STATE), F32) * (2 * S5_STATE) ** -0.5,
        's5_d': nrm(ks[17], (n_even, S5_WIDTH), F32),
        's5_w_glu': nrm(ks[18], (n_even, S5_WIDTH, S5_WIDTH), F32) * S5_WIDTH ** -0.5,
        'ret_log_decay': decay0 * (1.0 + 0.01 * nrm(ks[19], (n_even, 2, RET_HEADS), F32)),
        'na_w_qkv': nrm(ks[20], (n_odd, d, 3 * NA_WIDTH), F32) * d ** -0.5,
        'na_w_o': nrm(ks[21], (n_odd, NA_WIDTH, d), F32) * NA_WIDTH ** -0.5,
        'na_rpb': 0.02 * nrm(ks[22], (n_odd, NA_HEADS, 2 * NA_WIN_ROWS - 1, 2 * NA_WIN_COLS - 1), F32),
        'moe_w_router': nrm(ks[23], (DEPTH, d, N_EXPERTS), F32) * d ** -0.5,
        'moe_b_router': 0.01 * nrm(ks[24], (DEPTH, N_EXPERTS), F32),
        'moe_w1': nrm(ks[25], (DEPTH, N_EXPERTS, d, 2 * MOE_FF), F32) * d ** -0.5,
        'moe_b1': 0.01 * nrm(ks[26], (DEPTH, N_EXPERTS, 2 * MOE_FF), F32),
        'moe_w2': nrm(ks[27], (DEPTH, N_EXPERTS, MOE_FF, d), F32) * MOE_FF ** -0.5,
        'moe_b2': 0.01 * nrm(ks[28], (DEPTH, N_EXPERTS, d), F32),
    }


def reference(x, c, ctx, c_ctx, ada_w, ada_b, norm_w, final_norm_w, ev_w_in, ev_w_out,
              s5_lam_re, s5_lam_im, s5_log_step, s5_b_re, s5_b_im, s5_c_re, s5_c_im, s5_d, s5_w_glu,
              ret_log_decay, na_w_qkv, na_w_o, na_rpb,
              moe_w_router, moe_b_router, moe_w1, moe_b1, moe_w2, moe_b2):
    bsz, seq_len, d = x.shape
    cond_lat = jax.nn.silu(c)
    cond_ctx = jax.nn.silu(c_ctx)
    for i in range(DEPTH):
        last = i == DEPTH - 1
        sh1, sc1, g1, sh2, sc2, g2 = jnp.split(cond_lat @ ada_w[i] + ada_b[i], 6, axis=-1)
        csh1, csc1, cg1, csh2, csc2, cg2 = jnp.split(cond_ctx @ ada_w[i] + ada_b[i], 6, axis=-1)
        h_lat = modulate(rms_norm(x, norm_w[i, 0]), sh1[:, None], sc1[:, None])
        h_ctx = modulate(rms_norm(ctx, norm_w[i, 0]), csh1, csc1)
        if i % 2 == 0:
            j = i // 2
            m_lat, m_ctx = even_mixer(h_lat, h_ctx, ev_w_in[j], ev_w_out[j],
                                      s5_lam_re[j], s5_lam_im[j], s5_log_step[j], s5_b_re[j], s5_b_im[j],
                                      s5_c_re[j], s5_c_im[j], s5_d[j], s5_w_glu[j], ret_log_decay[j])
        else:
            j = i // 2
            m_lat, m_ctx = na_mixer(h_lat, h_ctx, na_w_qkv[j], na_w_o[j], na_rpb[j], not last)
        x = x + g1[:, None] * m_lat
        h_lat = modulate(rms_norm(x, norm_w[i, 1]), sh2[:, None], sc2[:, None])
        if last:
            y = moe_ffn(h_lat.reshape(-1, d), moe_w_router[i], moe_b_router[i],
                        moe_w1[i], moe_b1[i], moe_w2[i], moe_b2[i])
            x = x + g2[:, None] * y.reshape(bsz, seq_len, d)
        else:
            ctx = ctx + cg1 * m_ctx
            h_ctx = modulate(rms_norm(ctx, norm_w[i, 1]), csh2, csc2)
            tokens = jnp.concatenate([h_lat.reshape(-1, d), h_ctx.reshape(-1, d)], axis=0)
            y = moe_ffn(tokens, moe_w_router[i], moe_b_router[i],
                        moe_w1[i], moe_b1[i], moe_w2[i], moe_b2[i])
            n_lat = bsz * seq_len
            x = x + g2[:, None] * y[:n_lat].reshape(bsz, seq_len, d)
            ctx = ctx + cg2 * y[n_lat:].reshape(ctx.shape)
    return rms_norm(x, final_norm_w)
```

```python
import functools
import math

import numpy as np
import jax
import jax.numpy as jnp
from jax import lax
from jax.experimental import pallas as pl
from jax.experimental.pallas import tpu as pltpu

F32 = jnp.float32
BF16 = jnp.bfloat16

GRID_W = 64
RMS_EPS = 1e-6
S5_WIDTH = 512
S5_GROUP = 16
S5_GROUPS = S5_WIDTH // S5_GROUP
S5_STATE = 64
RET_HEADS = 4
RET_HEAD_DIM = 128
RET_WIDTH = RET_HEADS * RET_HEAD_DIM
ROPE_BASE = 10000.0
NA_HEADS = 16
NA_HEAD_DIM = 64
NA_WIDTH = NA_HEADS * NA_HEAD_DIM
NA_WIN_ROWS = 8
NA_WIN_COLS = 16
N_EXPERTS = 32
MOE_TOP_K = 4
SWIGLU_LIMIT = 7.0
SWIGLU_ALPHA = 1.702

LANES = 128
SUBLANES = 8
MXU_DIM = 256
V7X_VMEM_BYTES = 64 * 1024 * 1024
VMEM_LIMIT = V7X_VMEM_BYTES * 3 // 4

TOKEN_TILE = 256
S5_CHUNK = 128
S5_PITCH = S5_CHUNK + SUBLANES
RET_CHUNK = 256
MOE_TILE = 256
MASK_VALUE = -1e30

S5_SLICES = 8
assert S5_SLICES * MXU_DIM == S5_GROUPS * S5_STATE


def _params(sem):
    return pltpu.CompilerParams(dimension_semantics=sem, vmem_limit_bytes=VMEM_LIMIT)


def _adaln_kernel(c_ref, w_ref, b_ref, o_ref):
    c = c_ref[...]
    s = c * jax.nn.sigmoid(c)
    o_ref[0] = jnp.dot(s, w_ref[0], preferred_element_type=F32,
                       precision=lax.Precision.HIGHEST) + b_ref[0]


def adaln_table(c, c_ctx, ada_w, ada_b):
    depth, d, d6 = ada_w.shape
    bsz = c.shape[0]
    cond = jnp.concatenate([c, c_ctx[None, :]], axis=0)
    cond = jnp.pad(cond, ((0, SUBLANES - (bsz + 1)), (0, 0)))
    tn = d6 // 4
    out = pl.pallas_call(
        _adaln_kernel,
        grid=(depth, d6 // tn),
        in_specs=[pl.BlockSpec((SUBLANES, d), lambda i, j: (0, 0)),
                  pl.BlockSpec((1, d, tn), lambda i, j: (i, 0, j)),
                  pl.BlockSpec((1, 1, tn), lambda i, j: (i, 0, j))],
        out_specs=pl.BlockSpec((1, SUBLANES, tn), lambda i, j: (i, 0, j)),
        out_shape=jax.ShapeDtypeStruct((depth, SUBLANES, d6), F32),
        compiler_params=_params(("arbitrary", "arbitrary")),
        name="adaln",
    )(cond, ada_w, ada_b.reshape(depth, 1, d6))
    tab = out[:, :bsz + 1].reshape(depth, bsz + 1, 6, d)
    return jnp.pad(tab, ((0, 0), (0, 0), (0, 2), (0, 0)))


def _norm_mod(x, nw, shift, scale):
    y = x * lax.rsqrt(jnp.mean(x * x, axis=-1, keepdims=True) + RMS_EPS)
    return (y * nw) * (1 + scale) + shift


def _mod_index(n_lat_tiles, bsz):
    def index(b, j):
        return (jnp.where(j >= n_lat_tiles, bsz, b), 0, 0)
    return index


def _proj_kernel(x_ref, mod_ref, nw_ref, w_ref, o_ref, *, shift_row):
    h = _norm_mod(x_ref[0], nw_ref[...], mod_ref[0, shift_row:shift_row + 1, :],
                  mod_ref[0, shift_row + 1:shift_row + 2, :])
    o_ref[0] = jnp.dot(h.astype(BF16), w_ref[...], preferred_element_type=F32).astype(o_ref.dtype)


def norm_proj(xc, mod, norm_w, w, n_lat, shift_row):
    bsz, nt, d = xc.shape
    n = w.shape[1]
    tm = TOKEN_TILE
    return pl.pallas_call(
        functools.partial(_proj_kernel, shift_row=shift_row),
        grid=(bsz, nt // tm),
        in_specs=[pl.BlockSpec((1, tm, d), lambda b, j: (b, j, 0)),
                  pl.BlockSpec((1, SUBLANES, d), _mod_index(n_lat // tm, bsz)),
                  pl.BlockSpec((1, d), lambda b, j: (0, 0)),
                  pl.BlockSpec((d, n), lambda b, j: (0, 0))],
        out_specs=pl.BlockSpec((1, tm, n), lambda b, j: (b, j, 0)),
        out_shape=jax.ShapeDtypeStruct((bsz, nt, n), BF16),
        compiler_params=_params(("arbitrary", "arbitrary")),
        name="norm_proj",
    )(xc, mod, norm_w.reshape(1, d), w.astype(BF16))


def _s5_discretize(lam_re, lam_im, log_step, b_re, b_im):
    lam_re = jnp.minimum(lam_re.astype(F32), -1e-4)
    lam_im = lam_im.astype(F32)
    step = jnp.exp(log_step.astype(F32))[..., None]
    mag = jnp.exp(lam_re * step)
    ang = lam_im * step
    a_re, a_im = mag * jnp.cos(ang), mag * jnp.sin(ang)
    den = lam_re * lam_re + lam_im * lam_im
    n_re, n_im = a_re - 1.0, a_im
    co_re = (n_re * lam_re + n_im * lam_im) / den
    co_im = (n_im * lam_re - n_re * lam_im) / den
    b_re, b_im = b_re.astype(F32), b_im.astype(F32)
    bb_re = co_re[..., None] * b_re - co_im[..., None] * b_im
    bb_im = co_re[..., None] * b_im + co_im[..., None] * b_re
    return a_re, a_im, bb_re, bb_im


def _s5_layout(lam_re, lam_im, log_step, b_re, b_im, c_re, c_im):
    a_re, a_im, bb_re, bb_im = _s5_discretize(lam_re, lam_im, log_step, b_re, b_im)
    gh = S5_GROUPS // 2
    nq = S5_STATE // S5_GROUP
    eye = jnp.eye(gh, dtype=F32)

    def arrange_a(a):
        a = a.reshape(2, 2, gh, nq, S5_GROUP)
        return a.transpose(0, 1, 3, 2, 4).reshape(2, S5_SLICES, MXU_DIM)

    def arrange_b(bb):
        bb = bb.reshape(2, 2, gh, nq, S5_GROUP, S5_GROUP)
        m = jnp.einsum('dhgqni,Gg->dhqGign', bb, eye)
        return m.reshape(2, S5_SLICES, MXU_DIM, MXU_DIM).astype(BF16)

    def arrange_c(c):
        c = c.astype(F32).reshape(2, 2, gh, S5_GROUP, nq, S5_GROUP)
        m = jnp.einsum('dhgiqn,gG->dhqgnGi', c, eye)
        return m.reshape(2, S5_SLICES, MXU_DIM, MXU_DIM).astype(BF16)

    return (arrange_a(a_re), arrange_a(a_im), arrange_b(bb_re), arrange_b(bb_im),
            arrange_c(c_re), arrange_c(-c_im.astype(F32)))


def _s5_kernel(u_ref, bre_ref, bim_ref, cre_ref, cim_ref, are_ref, aim_ref, y_ref, st_ref, h_ref, *, nb):
    t_chunk = S5_CHUNK
    pitch = S5_PITCH
    half = MXU_DIM // LANES
    d = pl.program_id(0)

    @pl.when(pl.program_id(1) == 0)
    def _():
        h_ref[...] = jnp.zeros_like(h_ref)

    def slab(b, c, lh):
        return (b * 2 + c) * half + lh

    for b in range(nb):
        for h in range(2):
            ub = u_ref[b, :, h * MXU_DIM:(h + 1) * MXU_DIM]
            for q in range(S5_SLICES // 2):
                k = h * (S5_SLICES // 2) + q
                for c, w_ref in ((0, bre_ref), (1, bim_ref)):
                    r = jnp.dot(ub, w_ref[0, k], preferred_element_type=F32)
                    for lh in range(half):
                        st_ref[slab(b, c, lh), k * pitch:k * pitch + t_chunk, :] = r[:, lh * LANES:(lh + 1) * LANES]

    ar = [are_ref[0, :, lh * LANES:(lh + 1) * LANES] for lh in range(half)]
    ai = [aim_ref[0, :, lh * LANES:(lh + 1) * LANES] for lh in range(half)]

    def step(t, carry):
        tt = jnp.where(d == 0, t, t_chunk - 1 - t)
        new = []
        for b in range(nb):
            for lh in range(half):
                hr, hi = carry[2 * (b * half + lh)], carry[2 * (b * half + lh) + 1]
                rows = pl.ds(tt, S5_SLICES, stride=pitch)
                xr = st_ref[slab(b, 0, lh), rows, :]
                xi = st_ref[slab(b, 1, lh), rows, :]
                nr = ar[lh] * hr - ai[lh] * hi + xr
                ni = ar[lh] * hi + ai[lh] * hr + xi
                st_ref[slab(b, 0, lh), rows, :] = nr
                st_ref[slab(b, 1, lh), rows, :] = ni
                new += [nr, ni]
        return tuple(new)

    n_state = nb * half * 2
    carry = lax.fori_loop(0, t_chunk, step, tuple(h_ref[j] for j in range(n_state)), unroll=4)
    for j in range(n_state):
        h_ref[j] = carry[j]

    for b in range(nb):
        for h in range(2):
            acc = None
            for q in range(S5_SLICES // 2):
                k = h * (S5_SLICES // 2) + q
                for c, w_ref in ((0, cre_ref), (1, cim_ref)):
                    s = jnp.concatenate(
                        [st_ref[slab(b, c, lh), k * pitch:k * pitch + t_chunk, :] for lh in range(half)], axis=1)
                    term = jnp.dot(s.astype(BF16), w_ref[0, k], preferred_element_type=F32)
                    acc = term if acc is None else acc + term
            y_ref[0, b, :, h * MXU_DIM:(h + 1) * MXU_DIM] = acc


def _scan_chunk_index(n_chunks, n_ctx_chunks):
    def chunk(d, i):
        fwd = lax.rem(i + (n_chunks - n_ctx_chunks), n_chunks)
        return jnp.where(d == 0, fwd, n_chunks - 1 - i)
    return chunk


def s5_scan(proj, s5p, n_lat):
    bsz, nt, _ = proj.shape
    a_re, a_im, bb_re, bb_im, cc_re, cc_im = s5p
    t = S5_CHUNK
    nc = nt // t
    chunk = _scan_chunk_index(nc, (nt - n_lat) // t)
    wspec = pl.BlockSpec((1, S5_SLICES, MXU_DIM, MXU_DIM), lambda d, i: (d, 0, 0, 0))
    aspec = pl.BlockSpec((1, S5_SLICES, MXU_DIM), lambda d, i: (d, 0, 0))
    n_slab = bsz * 2 * (MXU_DIM // LANES)
    return pl.pallas_call(
        functools.partial(_s5_kernel, nb=bsz),
        grid=(2, nc),
        in_specs=[pl.BlockSpec((bsz, t, S5_WIDTH), lambda d, i: (0, chunk(d, i), 0)),
                  wspec, wspec, wspec, wspec, aspec, aspec],
        out_specs=pl.BlockSpec((1, bsz, t, S5_WIDTH), lambda d, i: (d, 0, chunk(d, i), 0)),
        out_shape=jax.ShapeDtypeStruct((2, bsz, nt, S5_WIDTH), F32),
        scratch_shapes=[pltpu.VMEM((n_slab, S5_SLICES * S5_PITCH, LANES), F32),
                        pltpu.VMEM((n_slab, S5_SLICES, LANES), F32)],
        compiler_params=_params(("arbitrary", "arbitrary")),
        name="s5_scan",
    )(proj, bb_re, bb_im, cc_re, cc_im, a_re, a_im)


def _rope_tables(n_lat, n_ctx):
    half = RET_HEAD_DIM // 4
    freq = ROPE_BASE ** (-jnp.arange(half, dtype=F32) / half)
    t = jnp.arange(n_lat)
    row = (t // GRID_W).astype(F32)
    col = (t % GRID_W).astype(F32)
    ang_r = row[:, None] * freq[None, :]
    ang_c = col[:, None] * freq[None, :]
    cos = jnp.concatenate([jnp.cos(ang_r)] * 2 + [jnp.cos(ang_c)] * 2, axis=-1)
    sin = jnp.concatenate([jnp.sin(ang_r)] * 2 + [jnp.sin(ang_c)] * 2, axis=-1)
    cos = jnp.concatenate([cos, jnp.ones((n_ctx, RET_HEAD_DIM), F32)], axis=0)
    sin = jnp.concatenate([sin, jnp.zeros((n_ctx, RET_HEAD_DIM), F32)], axis=0)
    return cos, sin


def _rope_rotation_matrix():
    blk = RET_HEAD_DIM // 2
    half = blk // 2
    r = np.zeros((RET_HEAD_DIM, RET_HEAD_DIM), np.float32)
    for base in (0, blk):
        for l in range(half):
            r[base + l + half, base + l] = -1.0
            r[base + l, base + l + half] = 1.0
    return jnp.asarray(r, BF16)


def _ret_decay(log_decay, t):
    lg = log_decay.astype(F32)
    scale = RET_HEAD_DIM ** -0.5
    idx = jnp.arange(t, dtype=F32)
    diff = idx[:, None] - idx[None, :]
    diff = jnp.stack([diff, -diff])[:, None]
    inner = jnp.where(diff >= 0, jnp.exp(lg[:, :, None, None] * jnp.maximum(diff, 0.0)), 0.0) * scale
    pos = jnp.stack([idx, t - 1.0 - idx])
    q_dec = jnp.exp(lg[:, :, None] * (pos[:, None, :] + 1.0))
    k_dec = jnp.exp(lg[:, :, None] * (t - 1.0 - pos[:, None, :])) * scale
    blk = jnp.exp(lg * t)
    bcast = lambda v: jnp.broadcast_to(v[..., None], v.shape + (RET_HEAD_DIM,))
    return inner, bcast(q_dec), bcast(k_dec), jnp.broadcast_to(blk[:, :, None, None], (2, RET_HEADS, SUBLANES, RET_HEAD_DIM))


def _ret_kernel(q_ref, k_ref, v_ref, cos_ref, sin_ref, rot_ref, inner_ref, qd_ref, kd_ref, bd_ref, o_ref, s_ref):
    @pl.when(pl.program_id(2) == 0)
    def _():
        s_ref[...] = jnp.zeros_like(s_ref)

    cos = cos_ref[...]
    sin = sin_ref[...]
    rot = rot_ref[...]
    nt_dims = (((1,), (1,)), ((), ()))
    for hd in range(RET_HEADS):
        sl = slice(hd * RET_HEAD_DIM, (hd + 1) * RET_HEAD_DIM)
        q = q_ref[0, :, sl]
        k = k_ref[0, :, sl]
        v = v_ref[0, :, sl]
        qr = q.astype(F32) * cos + jnp.dot(q, rot, preferred_element_type=F32) * sin
        kr = k.astype(F32) * cos + jnp.dot(k, rot, preferred_element_type=F32) * sin
        qb = qr.astype(BF16)
        att = lax.dot_general(qb, kr.astype(BF16), nt_dims, preferred_element_type=F32) * inner_ref[0, hd]
        s = s_ref[hd]
        o = jnp.dot(att.astype(BF16), v, preferred_element_type=F32)
        o = o + jnp.dot(qb, s.astype(BF16), preferred_element_type=F32) * qd_ref[0, hd]
        kd_t = (kr * kd_ref[0, hd]).T.astype(BF16)
        s_ref[hd] = bd_ref[0, hd, 0:1, :] * s + jnp.dot(kd_t, v, preferred_element_type=F32)
        o_ref[0, 0, :, sl] = o


def retention(proj, log_decay, n_lat):
    bsz, nt, _ = proj.shape
    t = RET_CHUNK
    nc = nt // t
    chunk = _scan_chunk_index(nc, (nt - n_lat) // t)
    cos, sin = _rope_tables(n_lat, nt - n_lat)
    inner, q_dec, k_dec, blk = _ret_decay(log_decay, t)
    w = RET_WIDTH

    def col(j):
        return pl.BlockSpec((1, t, w), lambda d, b, i: (b, chunk(d, i), j))

    tab = pl.BlockSpec((t, RET_HEAD_DIM), lambda d, b, i: (chunk(d, i), 0))
    dec = pl.BlockSpec((1, RET_HEADS, t, RET_HEAD_DIM), lambda d, b, i: (d, 0, 0, 0))
    return pl.pallas_call(
        _ret_kernel,
        grid=(2, bsz, nc),
        in_specs=[col(1), col(2), col(3), tab, tab,
                  pl.BlockSpec((RET_HEAD_DIM, RET_HEAD_DIM), lambda d, b, i: (0, 0)),
                  pl.BlockSpec((1, RET_HEADS, t, t), lambda d, b, i: (d, 0, 0, 0)),
                  dec, dec,
                  pl.BlockSpec((1, RET_HEADS, SUBLANES, RET_HEAD_DIM), lambda d, b, i: (d, 0, 0, 0))],
        out_specs=pl.BlockSpec((1, 1, t, w), lambda d, b, i: (d, b, chunk(d, i), 0)),
        out_shape=jax.ShapeDtypeStruct((2, bsz, nt, w), F32),
        scratch_shapes=[pltpu.VMEM((RET_HEADS, RET_HEAD_DIM, RET_HEAD_DIM), F32)],
        compiler_params=_params(("arbitrary", "arbitrary", "arbitrary")),
        name="retention",
    )(proj, proj, proj, cos, sin, _rope_rotation_matrix(), inner, q_dec, k_dec, blk)


def _mix_out_kernel(u_ref, g_ref, ys_ref, or_ref, x_ref, mod_ref, d_ref, wglu_ref, wo_ref, o_ref):
    y = u_ref[0].astype(F32) * d_ref[...] + ys_ref[0, 0] + ys_ref[1, 0]
    y = jax.nn.gelu(y)
    a = y * jax.nn.sigmoid(jnp.dot(y.astype(BF16), wglu_ref[...], preferred_element_type=F32))
    o = or_ref[0, 0] + or_ref[1, 0]
    heads = []
    for hd in range(RET_HEADS):
        oh = o[:, hd * RET_HEAD_DIM:(hd + 1) * RET_HEAD_DIM]
        heads.append(oh * lax.rsqrt(jnp.mean(oh * oh, axis=-1, keepdims=True) + RMS_EPS))
    r = jnp.concatenate(heads, axis=1) * jax.nn.silu(g_ref[0].astype(F32))
    m = jnp.dot(a.astype(BF16), wo_ref[:S5_WIDTH, :], preferred_element_type=F32)
    m = m + jnp.dot(r.astype(BF16), wo_ref[S5_WIDTH:, :], preferred_element_type=F32)
    o_ref[0] = x_ref[0] + mod_ref[0, 2:3, :] * m


def mix_out(xc, proj, y_s5, o_ret, mod, d_skip, w_glu, w_out, n_lat):
    bsz, nt, d = xc.shape
    tm = TOKEN_TILE
    w = S5_WIDTH
    dirs = pl.BlockSpec((2, 1, tm, w), lambda b, j: (0, b, j, 0))
    return pl.pallas_call(
        _mix_out_kernel,
        grid=(bsz, nt // tm),
        in_specs=[pl.BlockSpec((1, tm, w), lambda b, j: (b, j, 0)),
                  pl.BlockSpec((1, tm, w), lambda b, j: (b, j, 4)),
                  dirs, dirs,
                  pl.BlockSpec((1, tm, d), lambda b, j: (b, j, 0)),
                  pl.BlockSpec((1, SUBLANES, d), _mod_index(n_lat // tm, bsz)),
                  pl.BlockSpec((1, w), lambda b, j: (0, 0)),
                  pl.BlockSpec((w, w), lambda b, j: (0, 0)),
                  pl.BlockSpec((w + RET_WIDTH, d), lambda b, j: (0, 0))],
        out_specs=pl.BlockSpec((1, tm, d), lambda b, j: (b, j, 0)),
        out_shape=jax.ShapeDtypeStruct((bsz, nt, d), F32),
        compiler_params=_params(("arbitrary", "arbitrary")),
        name="mix_out",
    )(proj, proj, y_s5, o_ret, xc, mod, d_skip.reshape(1, w), w_glu.astype(BF16), w_out.astype(BF16))


def _na_bias(rpb):
    w = GRID_W
    col = jnp.arange(w)
    col_start = jnp.clip(col - NA_WIN_COLS // 2, 0, w - NA_WIN_COLS)
    kc = jnp.arange(w)
    inside = (kc[None, :] >= col_start[:, None]) & (kc[None, :] < col_start[:, None] + NA_WIN_COLS)
    col_off = jnp.clip(kc[None, :] - col[:, None] + (NA_WIN_COLS - 1), 0, 2 * NA_WIN_COLS - 2)
    v = jnp.arange(NA_WIN_ROWS)
    i = jnp.arange(NA_WIN_ROWS)
    row_off = i[None, :] - v[:, None] + (NA_WIN_ROWS - 1)
    b = rpb.astype(F32)[:, row_off[:, :, None, None], col_off[None, None, :, :]]
    b = jnp.where(inside[None, None, None], b, MASK_VALUE)
    b = b.transpose(1, 0, 3, 2, 4)
    return b.reshape(NA_WIN_ROWS, NA_HEADS, w, NA_WIN_ROWS * w)


def _na_kernel(q_ref, kb_ref, vb_ref, kc_ref, vc_ref, bias_ref, o_ref):
    scale = NA_HEAD_DIM ** -0.5
    nt_dims = (((1,), (1,)), ((), ()))
    lane = lax.broadcasted_iota(jnp.int32, (GRID_W, LANES), 1)
    for j in range(NA_WIDTH // LANES):
        sl = slice(j * LANES, (j + 1) * LANES)
        q2 = q_ref[0, :, sl].astype(F32)
        k2 = kb_ref[0, :, sl]
        v2 = vb_ref[0, :, sl]
        kc2 = kc_ref[0, :, sl]
        vc2 = vc_ref[0, :, sl]
        outs = []
        for hh in range(LANES // NA_HEAD_DIM):
            in_head = (lane >= hh * NA_HEAD_DIM) & (lane < (hh + 1) * NA_HEAD_DIM)
            qm = jnp.where(in_head, q2, 0.0).astype(BF16)
            s_loc = lax.dot_general(qm, k2, nt_dims, preferred_element_type=F32) * scale
            s_loc = s_loc + bias_ref[0, j * (LANES // NA_HEAD_DIM) + hh]
            s_ctx = lax.dot_general(qm, kc2, nt_dims, preferred_element_type=F32) * scale
            m = jnp.maximum(jnp.max(s_loc, axis=-1, keepdims=True), jnp.max(s_ctx, axis=-1, keepdims=True))
            p_loc = jnp.exp(s_loc - m)
            p_ctx = jnp.exp(s_ctx - m)
            denom = jnp.sum(p_loc, axis=-1, keepdims=True) + jnp.sum(p_ctx, axis=-1, keepdims=True)
            o = jnp.dot(p_loc.astype(BF16), v2, preferred_element_type=F32)
            o = o + jnp.dot(p_ctx.astype(BF16), vc2, preferred_element_type=F32)
            outs.append(o / denom)
        o2 = outs[0]
        for hh in range(1, len(outs)):
            o2 = jnp.where(lane >= hh * NA_HEAD_DIM, outs[hh], o2)
        o_ref[0, :, sl] = o2.astype(o_ref.dtype)


def na_attention(qkv, rpb, n_lat):
    bsz, nt, _ = qkv.shape
    w = GRID_W
    rows = n_lat // w
    n_ctx = nt - n_lat
    band = NA_WIN_ROWS * w

    def row_start(r):
        return jnp.clip(r - NA_WIN_ROWS // 2, 0, rows - NA_WIN_ROWS)

    def band_spec(j):
        return pl.BlockSpec((pl.Element(1), pl.Element(band), pl.Element(NA_WIDTH)),
                            lambda b, r: (b, row_start(r) * w, j * NA_WIDTH))

    def ctx_spec(j):
        return pl.BlockSpec((1, n_ctx, NA_WIDTH), lambda b, r: (b, n_lat // n_ctx, j))

    return pl.pallas_call(
        _na_kernel,
        grid=(bsz, rows),
        in_specs=[pl.BlockSpec((1, w, NA_WIDTH), lambda b, r: (b, r, 0)),
                  band_spec(1), band_spec(2), ctx_spec(1), ctx_spec(2),
                  pl.BlockSpec((1, NA_HEADS, w, band), lambda b, r: (r - row_start(r), 0, 0, 0))],
        out_specs=pl.BlockSpec((1, w, NA_WIDTH), lambda b, r: (b, r, 0)),
        out_shape=jax.ShapeDtypeStruct((bsz, n_lat, NA_WIDTH), BF16),
        compiler_params=_params(("arbitrary", "arbitrary")),
        name="na_attention",
    )(qkv, qkv, qkv, qkv, qkv, _na_bias(rpb))


def _proj_res_kernel(a_ref, x_ref, mod_ref, w_ref, o_ref):
    m = jnp.dot(a_ref[0], w_ref[...], preferred_element_type=F32)
    o_ref[0] = x_ref[0] + mod_ref[0, 2:3, :] * m


def proj_residual(a, x, mod, w):
    bsz, n, d = x.shape
    k = a.shape[-1]
    tm = TOKEN_TILE
    return pl.pallas_call(
        _proj_res_kernel,
        grid=(bsz, n // tm),
        in_specs=[pl.BlockSpec((1, tm, k), lambda b, j: (b, j, 0)),
                  pl.BlockSpec((1, tm, d), lambda b, j: (b, j, 0)),
                  pl.BlockSpec((1, SUBLANES, d), lambda b, j: (b, 0, 0)),
                  pl.BlockSpec((k, d), lambda b, j: (0, 0))],
        out_specs=pl.BlockSpec((1, tm, d), lambda b, j: (b, j, 0)),
        out_shape=jax.ShapeDtypeStruct((bsz, n, d), F32),
        compiler_params=_params(("arbitrary", "arbitrary")),
        name="proj_residual",
    )(a, x, mod, w.astype(BF16))


def _router_kernel(x_ref, mod_ref, nw_ref, wr_ref, br_ref, h_ref, idx_ref, gate_ref):
    h = _norm_mod(x_ref[0], nw_ref[...], mod_ref[0, 3:4, :], mod_ref[0, 4:5, :])
    h_ref[0] = h.astype(h_ref.dtype)
    logits = jnp.dot(h.astype(BF16), wr_ref[...], preferred_element_type=F32) + br_ref[...]
    lane = lax.broadcasted_iota(jnp.int32, logits.shape, 1)
    lane_f = lane.astype(F32)
    vals, idxs = [], []
    cur = logits
    for _ in range(MOE_TOP_K):
        m = jnp.max(cur, axis=-1, keepdims=True)
        first = jnp.min(jnp.where(cur == m, lane_f, float(LANES)), axis=-1, keepdims=True)
        vals.append(m)
        idxs.append(first)
        cur = jnp.where(lane_f == first, MASK_VALUE, cur)
    es = [jnp.exp(v - vals[0]) for v in vals]
    tot = es[0]
    for e in es[1:]:
        tot = tot + e
    idx_out = jnp.zeros(logits.shape, F32)
    gate_out = jnp.zeros(logits.shape, F32)
    for k in range(MOE_TOP_K):
        idx_out = jnp.where(lane == k, idxs[k], idx_out)
        gate_out = jnp.where(lane == k, es[k] / tot, gate_out)
    idx_ref[0] = idx_out.astype(jnp.int32)
    gate_ref[0] = gate_out


def router(xc, mod, norm_w, w_router, b_router, n_lat):
    bsz, nt, d = xc.shape
    tm = TOKEN_TILE
    wr = jnp.pad(w_router, ((0, 0), (0, LANES - N_EXPERTS))).astype(BF16)
    br = jnp.pad(b_router.astype(F32), (0, LANES - N_EXPERTS), constant_values=MASK_VALUE).reshape(1, LANES)
    tok = lambda n, dt: jax.ShapeDtypeStruct((bsz, nt, n), dt)
    out = lambda n: pl.BlockSpec((1, tm, n), lambda b, j: (b, j, 0))
    return pl.pallas_call(
        _router_kernel,
        grid=(bsz, nt // tm),
        in_specs=[pl.BlockSpec((1, tm, d), lambda b, j: (b, j, 0)),
                  pl.BlockSpec((1, SUBLANES, d), _mod_index(n_lat // tm, bsz)),
                  pl.BlockSpec((1, d), lambda b, j: (0, 0)),
                  pl.BlockSpec((d, LANES), lambda b, j: (0, 0)),
                  pl.BlockSpec((1, LANES), lambda b, j: (0, 0))],
        out_specs=[out(d), out(LANES), out(LANES)],
        out_shape=[tok(d, BF16), tok(LANES, jnp.int32), tok(LANES, F32)],
        compiler_params=_params(("arbitrary", "arbitrary")),
        name="router",
    )(xc, mod, norm_w.reshape(1, d), wr, br)


def _moe_kernel(be_ref, nu_ref, x_ref, w1_ref, b1_ref, w2_ref, b2_ref, o_ref):
    i = pl.program_id(0)
    ff = w2_ref.shape[1]

    @pl.when(i < nu_ref[0])
    def _():
        t = jnp.dot(x_ref[...], w1_ref[0], preferred_element_type=F32) + b1_ref[0]
        x_glu = jnp.minimum(t[:, :ff], SWIGLU_LIMIT)
        x_lin = jnp.clip(t[:, ff:], -SWIGLU_LIMIT, SWIGLU_LIMIT)
        act = x_glu * jax.nn.sigmoid(SWIGLU_ALPHA * x_glu) * (x_lin + 1)
        y = jnp.dot(act.astype(BF16), w2_ref[0], preferred_element_type=F32) + b2_ref[0]
        o_ref[...] = y.astype(o_ref.dtype)

    @pl.when(i >= nu_ref[0])
    def _():
        o_ref[...] = jnp.zeros_like(o_ref)


def moe_experts(xs, block_e, n_used, w1, b1, w2, b2):
    n_slots, d = xs.shape
    ne, _, ff2 = w1.shape
    ff = w2.shape[1]
    tm = MOE_TILE
    nb = n_slots // tm
    grid_spec = pltpu.PrefetchScalarGridSpec(
        num_scalar_prefetch=2,
        grid=(nb,),
        in_specs=[pl.BlockSpec((tm, d), lambda i, be, nu: (i, 0)),
                  pl.BlockSpec((1, d, ff2), lambda i, be, nu: (be[i], 0, 0)),
                  pl.BlockSpec((1, 1, ff2), lambda i, be, nu: (be[i], 0, 0)),
                  pl.BlockSpec((1, ff, d), lambda i, be, nu: (be[i], 0, 0)),
                  pl.BlockSpec((1, 1, d), lambda i, be, nu: (be[i], 0, 0))],
        out_specs=pl.BlockSpec((tm, d), lambda i, be, nu: (i, 0)),
    )
    return pl.pallas_call(
        _moe_kernel,
        grid_spec=grid_spec,
        out_shape=jax.ShapeDtypeStruct((n_slots, d), F32),
        compiler_params=_params(("arbitrary",)),
        name="moe_experts",
    )(block_e, n_used, xs, w1.astype(BF16), b1.reshape(ne, 1, ff2), w2.astype(BF16), b2.reshape(ne, 1, d))


def _moe_plan(top_idx, n_tok):
    n_assign = n_tok * MOE_TOP_K
    flat_e = top_idx.reshape(-1)
    order = jnp.argsort(flat_e)
    sorted_e = flat_e[order]
    counts = jnp.bincount(flat_e, length=N_EXPERTS)
    starts = jnp.cumsum(counts) - counts
    padded = (counts + MOE_TILE - 1) // MOE_TILE * MOE_TILE
    padded_ends = jnp.cumsum(padded)
    padded_starts = padded_ends - padded
    dest = (padded_starts[sorted_e] + jnp.arange(n_assign) - starts[sorted_e]).astype(jnp.int32)
    n_blocks = -(-n_assign // MOE_TILE) + N_EXPERTS
    n_slots = n_blocks * MOE_TILE
    slot_tok = jnp.full((n_slots,), n_tok, jnp.int32).at[dest].set((order // MOE_TOP_K).astype(jnp.int32))
    slot_of = jnp.zeros((n_assign,), jnp.int32).at[order].set(dest)
    block_e = jnp.minimum(jnp.searchsorted(padded_ends, jnp.arange(n_blocks) * MOE_TILE, side='right'),
                          N_EXPERTS - 1).astype(jnp.int32)
    n_used = (padded_ends[-1:] // MOE_TILE).astype(jnp.int32)
    return slot_tok, slot_of, block_e, n_used


def _combine_kernel(x_ref, y_ref, gate_ref, mod_ref, *rest, final):
    o_ref = rest[-1]
    d = x_ref.shape[-1]
    gate = gate_ref[0]
    y = None
    for k in range(MOE_TOP_K):
        term = gate[:, k:k + 1] * y_ref[0, :, k * d:(k + 1) * d]
        y = term if y is None else y + term
    x = x_ref[0] + mod_ref[0, 5:6, :] * y
    if final:
        fw_ref = rest[0]
        x = x * lax.rsqrt(jnp.mean(x * x, axis=-1, keepdims=True) + RMS_EPS) * fw_ref[...]
    o_ref[0] = x


def moe_combine(xc, y_tok, gates, mod, n_lat, n_out, final_w=None):
    bsz, nt, d = xc.shape
    tm = TOKEN_TILE
    in_specs = [pl.BlockSpec((1, tm, d), lambda b, j: (b, j, 0)),
                pl.BlockSpec((1, tm, MOE_TOP_K * d), lambda b, j: (b, j, 0)),
                pl.BlockSpec((1, tm, LANES), lambda b, j: (b, j, 0)),
                pl.BlockSpec((1, SUBLANES, d), _mod_index(n_lat // tm, bsz))]
    args = [xc, y_tok, gates, mod]
    if final_w is not None:
        in_specs.append(pl.BlockSpec((1, d), lambda b, j: (0, 0)))
        args.append(final_w.reshape(1, d))
    return pl.pallas_call(
        functools.partial(_combine_kernel, final=final_w is not None),
        grid=(bsz, n_out // tm),
        in_specs=in_specs,
        out_specs=pl.BlockSpec((1, tm, d), lambda b, j: (b, j, 0)),
        out_shape=jax.ShapeDtypeStruct((bsz, n_out, d), F32),
        compiler_params=_params(("arbitrary", "arbitrary")),
        name="moe_combine",
    )(*args)


def moe_layer(xc, mod, norm_w, w_router, b_router, w1, b1, w2, b2, n_lat, n_out, final_w=None):
    bsz, nt, d = xc.shape
    n_tok = bsz * nt
    h, idx, gates = router(xc, mod, norm_w, w_router, b_router, n_lat)
    top_idx = idx.reshape(n_tok, LANES)[:, :MOE_TOP_K]
    slot_tok, slot_of, block_e, n_used = _moe_plan(top_idx, n_tok)
    h_pad = jnp.concatenate([h.reshape(n_tok, d), jnp.zeros((1, d), h.dtype)], axis=0)
    xs = h_pad[slot_tok]
    ys = moe_experts(xs, block_e, n_used, w1, b1, w2, b2)
    y_tok = ys[slot_of].reshape(bsz, nt, MOE_TOP_K * d)
    return moe_combine(xc, y_tok, gates, mod, n_lat, n_out, final_w)


def kernel(x, c, ctx, c_ctx, ada_w, ada_b, norm_w, final_norm_w, ev_w_in, ev_w_out, s5_lam_re, s5_lam_im, s5_log_step, s5_b_re, s5_b_im, s5_c_re, s5_c_im, s5_d, s5_w_glu, ret_log_decay, na_w_qkv, na_w_o, na_rpb, moe_w_router, moe_b_router, moe_w1, moe_b1, moe_w2, moe_b2):
    bsz, n_lat, d = x.shape
    n_ctx = ctx.shape[1]
    depth = ada_w.shape[0]
    mod = adaln_table(c, c_ctx, ada_w, ada_b)
    xc = jnp.concatenate([x, ctx], axis=1)
    for i in range(depth):
        last = i == depth - 1
        j = i // 2
        if i % 2 == 0:
            proj = norm_proj(xc, mod[i], norm_w[i, 0], ev_w_in[j], n_lat, 0)
            s5p = _s5_layout(s5_lam_re[j], s5_lam_im[j], s5_log_step[j], s5_b_re[j], s5_b_im[j],
                             s5_c_re[j], s5_c_im[j])
            y_s5 = s5_scan(proj, s5p, n_lat)
            o_ret = retention(proj, ret_log_decay[j], n_lat)
            xc = mix_out(xc, proj, y_s5, o_ret, mod[i], s5_d[j], s5_w_glu[j], ev_w_out[j], n_lat)
        else:
            qkv = norm_proj(xc, mod[i], norm_w[i, 0], na_w_qkv[j], n_lat, 0)
            att = na_attention(qkv, na_rpb[j], n_lat)
            assert last, "an odd layer is only supported as the final layer (no context output needed)"
            xc = proj_residual(att, xc[:, :n_lat], mod[i], na_w_o[j])
        if last:
            return moe_layer(xc[:, :n_lat], mod[i], norm_w[i, 1], moe_w_router[i], moe_b_router[i],
                             moe_w1[i], moe_b1[i], moe_w2[i], moe_b2[i], n_lat, n_lat, final_norm_w)
        xc = moe_layer(xc, mod[i], norm_w[i, 1], moe_w_router[i], moe_b_router[i],
                       moe_w1[i], moe_b1[i], moe_w2[i], moe_b2[i], n_lat, n_lat + n_ctx)
```

```python
import functools
import math

import numpy as np
import jax
import jax.numpy as jnp
from jax import lax
from jax.experimental import pallas as pl
from jax.experimental.pallas import tpu as pltpu

F32 = jnp.float32
BF16 = jnp.bfloat16

GRID_W = 64
RMS_EPS = 1e-6
S5_WIDTH = 512
S5_GROUP = 16
S5_GROUPS = S5_WIDTH // S5_GROUP
S5_STATE = 64
RET_HEADS = 4
RET_HEAD_DIM = 128
RET_WIDTH = RET_HEADS * RET_HEAD_DIM
ROPE_BASE = 10000.0
NA_HEADS = 16
NA_HEAD_DIM = 64
NA_WIDTH = NA_HEADS * NA_HEAD_DIM
NA_WIN_ROWS = 8
NA_WIN_COLS = 16
N_EXPERTS = 32
MOE_TOP_K = 4
SWIGLU_LIMIT = 7.0
SWIGLU_ALPHA = 1.702

LANES = 128
SUBLANES = 8
MXU_DIM = 256
V7X_VMEM_BYTES = 64 * 1024 * 1024
VMEM_LIMIT = V7X_VMEM_BYTES * 3 // 4

TOKEN_TILE = 256
S5_CHUNK = 128
S5_PITCH = S5_CHUNK + SUBLANES
RET_CHUNK = 256
MOE_TILE = 256
NA_Q_ROWS = 2
NA_BAND_ROWS = 10
NA_SOFTMAX_ROWS = 16
NA_SCORE_SLOTS = 4
assert NA_BAND_ROWS >= NA_WIN_ROWS + NA_Q_ROWS - 1 and (NA_BAND_ROWS - NA_Q_ROWS) % NA_Q_ROWS == 0
MASK_VALUE = -1e30

S5_SLICES = 8
assert S5_SLICES * MXU_DIM == S5_GROUPS * S5_STATE


def _params(sem):
    return pltpu.CompilerParams(dimension_semantics=sem, vmem_limit_bytes=VMEM_LIMIT)


def _adaln_kernel(c_ref, w_ref, b_ref, o_ref):
    c = c_ref[...]
    s = c * jax.nn.sigmoid(c)
    o_ref[0] = jnp.dot(s, w_ref[0], preferred_element_type=F32,
                       precision=lax.Precision.HIGHEST) + b_ref[0]


def adaln_table(c, c_ctx, ada_w, ada_b):
    depth, d, d6 = ada_w.shape
    bsz = c.shape[0]
    cond = jnp.concatenate([c, c_ctx[None, :]], axis=0)
    cond = jnp.pad(cond, ((0, SUBLANES - (bsz + 1)), (0, 0)))
    tn = d6 // 4
    out = pl.pallas_call(
        _adaln_kernel,
        grid=(depth, d6 // tn),
        in_specs=[pl.BlockSpec((SUBLANES, d), lambda i, j: (0, 0)),
                  pl.BlockSpec((1, d, tn), lambda i, j: (i, 0, j)),
                  pl.BlockSpec((1, 1, tn), lambda i, j: (i, 0, j))],
        out_specs=pl.BlockSpec((1, SUBLANES, tn), lambda i, j: (i, 0, j)),
        out_shape=jax.ShapeDtypeStruct((depth, SUBLANES, d6), F32),
        compiler_params=_params(("arbitrary", "arbitrary")),
        name="adaln",
    )(cond, ada_w, ada_b.reshape(depth, 1, d6))
    tab = out[:, :bsz + 1].reshape(depth, bsz + 1, 6, d)
    return jnp.pad(tab, ((0, 0), (0, 0), (0, 2), (0, 0)))


def _norm_mod(x, nw, shift, scale):
    y = x * lax.rsqrt(jnp.mean(x * x, axis=-1, keepdims=True) + RMS_EPS)
    return (y * nw) * (1 + scale) + shift


def _mod_index(n_lat_tiles, bsz):
    def index(b, j):
        return (jnp.where(j >= n_lat_tiles, bsz, b), 0, 0)
    return index


def _proj_kernel(x_ref, mod_ref, nw_ref, w_ref, o_ref, *, shift_row):
    h = _norm_mod(x_ref[0], nw_ref[...], mod_ref[0, shift_row:shift_row + 1, :],
                  mod_ref[0, shift_row + 1:shift_row + 2, :])
    o_ref[0] = jnp.dot(h.astype(BF16), w_ref[...], preferred_element_type=F32).astype(o_ref.dtype)


def norm_proj(xc, mod, norm_w, w, n_lat, shift_row):
    bsz, nt, d = xc.shape
    n = w.shape[1]
    tm = TOKEN_TILE
    return pl.pallas_call(
        functools.partial(_proj_kernel, shift_row=shift_row),
        grid=(bsz, nt // tm),
        in_specs=[pl.BlockSpec((1, tm, d), lambda b, j: (b, j, 0)),
                  pl.BlockSpec((1, SUBLANES, d), _mod_index(n_lat // tm, bsz)),
                  pl.BlockSpec((1, d), lambda b, j: (0, 0)),
                  pl.BlockSpec((d, n), lambda b, j: (0, 0))],
        out_specs=pl.BlockSpec((1, tm, n), lambda b, j: (b, j, 0)),
        out_shape=jax.ShapeDtypeStruct((bsz, nt, n), BF16),
        compiler_params=_params(("arbitrary", "arbitrary")),
        name="norm_proj",
    )(xc, mod, norm_w.reshape(1, d), w.astype(BF16))


def _s5_discretize(lam_re, lam_im, log_step, b_re, b_im):
    lam_re = jnp.minimum(lam_re.astype(F32), -1e-4)
    lam_im = lam_im.astype(F32)
    step = jnp.exp(log_step.astype(F32))[..., None]
    mag = jnp.exp(lam_re * step)
    ang = lam_im * step
    a_re, a_im = mag * jnp.cos(ang), mag * jnp.sin(ang)
    den = lam_re * lam_re + lam_im * lam_im
    n_re, n_im = a_re - 1.0, a_im
    co_re = (n_re * lam_re + n_im * lam_im) / den
    co_im = (n_im * lam_re - n_re * lam_im) / den
    b_re, b_im = b_re.astype(F32), b_im.astype(F32)
    bb_re = co_re[..., None] * b_re - co_im[..., None] * b_im
    bb_im = co_re[..., None] * b_im + co_im[..., None] * b_re
    return a_re, a_im, bb_re, bb_im


def _s5_layout(lam_re, lam_im, log_step, b_re, b_im, c_re, c_im):
    a_re, a_im, bb_re, bb_im = _s5_discretize(lam_re, lam_im, log_step, b_re, b_im)
    gh = S5_GROUPS // 2
    nq = S5_STATE // S5_GROUP
    eye = jnp.eye(gh, dtype=F32)

    def arrange_a(a):
        a = a.reshape(2, 2, gh, nq, S5_GROUP)
        return a.transpose(0, 1, 3, 2, 4).reshape(2, S5_SLICES, MXU_DIM)

    def arrange_b(bb):
        bb = bb.reshape(2, 2, gh, nq, S5_GROUP, S5_GROUP)
        m = jnp.einsum('dhgqni,Gg->dhqGign', bb, eye)
        return m.reshape(2, S5_SLICES, MXU_DIM, MXU_DIM).astype(BF16)

    def arrange_c(c):
        c = c.astype(F32).reshape(2, 2, gh, S5_GROUP, nq, S5_GROUP)
        m = jnp.einsum('dhgiqn,gG->dhqgnGi', c, eye)
        return m.reshape(2, S5_SLICES, MXU_DIM, MXU_DIM).astype(BF16)

    return (arrange_a(a_re), arrange_a(a_im), arrange_b(bb_re), arrange_b(bb_im),
            arrange_c(c_re), arrange_c(-c_im.astype(F32)))


def _s5_kernel(u_ref, bre_ref, bim_ref, cre_ref, cim_ref, are_ref, aim_ref, y_ref, st_ref, h_ref, *, nb):
    t_chunk = S5_CHUNK
    pitch = S5_PITCH
    half = MXU_DIM // LANES
    d = pl.program_id(0)

    @pl.when(pl.program_id(1) == 0)
    def _():
        h_ref[...] = jnp.zeros_like(h_ref)

    def slab(b, c, lh):
        return (b * 2 + c) * half + lh

    for b in range(nb):
        for h in range(2):
            ub = u_ref[b, :, h * MXU_DIM:(h + 1) * MXU_DIM]
            for q in range(S5_SLICES // 2):
                k = h * (S5_SLICES // 2) + q
                for c, w_ref in ((0, bre_ref), (1, bim_ref)):
                    r = jnp.dot(ub, w_ref[0, k], preferred_element_type=F32)
                    for lh in range(half):
                        st_ref[slab(b, c, lh), k * pitch:k * pitch + t_chunk, :] = r[:, lh * LANES:(lh + 1) * LANES]

    ar = [are_ref[0, :, lh * LANES:(lh + 1) * LANES] for lh in range(half)]
    ai = [aim_ref[0, :, lh * LANES:(lh + 1) * LANES] for lh in range(half)]

    def step(t, carry):
        tt = jnp.where(d == 0, t, t_chunk - 1 - t)
        new = []
        for b in range(nb):
            for lh in range(half):
                hr, hi = carry[2 * (b * half + lh)], carry[2 * (b * half + lh) + 1]
                rows = pl.ds(tt, S5_SLICES, stride=pitch)
                xr = st_ref[slab(b, 0, lh), rows, :]
                xi = st_ref[slab(b, 1, lh), rows, :]
                nr = ar[lh] * hr - ai[lh] * hi + xr
                ni = ar[lh] * hi + ai[lh] * hr + xi
                st_ref[slab(b, 0, lh), rows, :] = nr
                st_ref[slab(b, 1, lh), rows, :] = ni
                new += [nr, ni]
        return tuple(new)

    n_state = nb * half * 2
    carry = lax.fori_loop(0, t_chunk, step, tuple(h_ref[j] for j in range(n_state)), unroll=4)
    for j in range(n_state):
        h_ref[j] = carry[j]

    for b in range(nb):
        for h in range(2):
            acc = None
            for q in range(S5_SLICES // 2):
                k = h * (S5_SLICES // 2) + q
                for c, w_ref in ((0, cre_ref), (1, cim_ref)):
                    s = jnp.concatenate(
                        [st_ref[slab(b, c, lh), k * pitch:k * pitch + t_chunk, :] for lh in range(half)], axis=1)
                    term = jnp.dot(s.astype(BF16), w_ref[0, k], preferred_element_type=F32)
                    acc = term if acc is None else acc + term
            y_ref[0, b, :, h * MXU_DIM:(h + 1) * MXU_DIM] = acc


def _scan_chunk_index(n_chunks, n_ctx_chunks):
    def chunk(d, i):
        fwd = lax.rem(i + (n_chunks - n_ctx_chunks), n_chunks)
        return jnp.where(d == 0, fwd, n_chunks - 1 - i)
    return chunk


def s5_scan(proj, s5p, n_lat):
    bsz, nt, _ = proj.shape
    a_re, a_im, bb_re, bb_im, cc_re, cc_im = s5p
    t = S5_CHUNK
    nc = nt // t
    chunk = _scan_chunk_index(nc, (nt - n_lat) // t)
    wspec = pl.BlockSpec((1, S5_SLICES, MXU_DIM, MXU_DIM), lambda d, i: (d, 0, 0, 0))
    aspec = pl.BlockSpec((1, S5_SLICES, MXU_DIM), lambda d, i: (d, 0, 0))
    n_slab = bsz * 2 * (MXU_DIM // LANES)
    return pl.pallas_call(
        functools.partial(_s5_kernel, nb=bsz),
        grid=(2, nc),
        in_specs=[pl.BlockSpec((bsz, t, S5_WIDTH), lambda d, i: (0, chunk(d, i), 0)),
                  wspec, wspec, wspec, wspec, aspec, aspec],
        out_specs=pl.BlockSpec((1, bsz, t, S5_WIDTH), lambda d, i: (d, 0, chunk(d, i), 0)),
        out_shape=jax.ShapeDtypeStruct((2, bsz, nt, S5_WIDTH), F32),
        scratch_shapes=[pltpu.VMEM((n_slab, S5_SLICES * S5_PITCH, LANES), F32),
                        pltpu.VMEM((n_slab, S5_SLICES, LANES), F32)],
        compiler_params=_params(("arbitrary", "arbitrary")),
        name="s5_scan",
    )(proj, bb_re, bb_im, cc_re, cc_im, a_re, a_im)


def _rope_tables(n_lat, n_ctx):
    half = RET_HEAD_DIM // 4
    freq = ROPE_BASE ** (-jnp.arange(half, dtype=F32) / half)
    t = jnp.arange(n_lat)
    row = (t // GRID_W).astype(F32)
    col = (t % GRID_W).astype(F32)
    ang_r = row[:, None] * freq[None, :]
    ang_c = col[:, None] * freq[None, :]
    cos = jnp.concatenate([jnp.cos(ang_r)] * 2 + [jnp.cos(ang_c)] * 2, axis=-1)
    sin = jnp.concatenate([jnp.sin(ang_r)] * 2 + [jnp.sin(ang_c)] * 2, axis=-1)
    cos = jnp.concatenate([cos, jnp.ones((n_ctx, RET_HEAD_DIM), F32)], axis=0)
    sin = jnp.concatenate([sin, jnp.zeros((n_ctx, RET_HEAD_DIM), F32)], axis=0)
    return cos, sin


def _rope_rotation_matrix():
    blk = RET_HEAD_DIM // 2
    half = blk // 2
    r = np.zeros((RET_HEAD_DIM, RET_HEAD_DIM), np.float32)
    for base in (0, blk):
        for l in range(half):
            r[base + l + half, base + l] = -1.0
            r[base + l, base + l + half] = 1.0
    return jnp.asarray(r, BF16)


def _ret_decay(log_decay, t):
    lg = log_decay.astype(F32)
    scale = RET_HEAD_DIM ** -0.5
    idx = jnp.arange(t, dtype=F32)
    diff = idx[:, None] - idx[None, :]
    diff = jnp.stack([diff, -diff])[:, None]
    inner = jnp.where(diff >= 0, jnp.exp(lg[:, :, None, None] * jnp.maximum(diff, 0.0)), 0.0) * scale
    pos = jnp.stack([idx, t - 1.0 - idx])
    q_dec = jnp.exp(lg[:, :, None] * (pos[:, None, :] + 1.0))
    k_dec = jnp.exp(lg[:, :, None] * (t - 1.0 - pos[:, None, :])) * scale
    blk = jnp.exp(lg * t)
    bcast = lambda v: jnp.broadcast_to(v[..., None], v.shape + (RET_HEAD_DIM,))
    return inner, bcast(q_dec), bcast(k_dec), jnp.broadcast_to(blk[:, :, None, None], (2, RET_HEADS, SUBLANES, RET_HEAD_DIM))


def _ret_kernel(q_ref, k_ref, v_ref, cos_ref, sin_ref, rot_ref, inner_ref, qd_ref, kd_ref, bd_ref, o_ref, s_ref):
    @pl.when(pl.program_id(2) == 0)
    def _():
        s_ref[...] = jnp.zeros_like(s_ref)

    cos = cos_ref[...]
    sin = sin_ref[...]
    rot = rot_ref[...]
    nt_dims = (((1,), (1,)), ((), ()))
    for hd in range(RET_HEADS):
        sl = slice(hd * RET_HEAD_DIM, (hd + 1) * RET_HEAD_DIM)
        q = q_ref[0, :, sl]
        k = k_ref[0, :, sl]
        v = v_ref[0, :, sl]
        qr = q.astype(F32) * cos + jnp.dot(q, rot, preferred_element_type=F32) * sin
        kr = k.astype(F32) * cos + jnp.dot(k, rot, preferred_element_type=F32) * sin
        qb = qr.astype(BF16)
        att = lax.dot_general(qb, kr.astype(BF16), nt_dims, preferred_element_type=F32) * inner_ref[0, hd]
        s = s_ref[hd]
        o = jnp.dot(att.astype(BF16), v, preferred_element_type=F32)
        o = o + jnp.dot(qb, s.astype(BF16), preferred_element_type=F32) * qd_ref[0, hd]
        kd_t = (kr * kd_ref[0, hd]).T.astype(BF16)
        s_ref[hd] = bd_ref[0, hd, 0:1, :] * s + jnp.dot(kd_t, v, preferred_element_type=F32)
        o_ref[0, 0, :, sl] = o


def retention(proj, log_decay, n_lat):
    bsz, nt, _ = proj.shape
    t = RET_CHUNK
    nc = nt // t
    chunk = _scan_chunk_index(nc, (nt - n_lat) // t)
    cos, sin = _rope_tables(n_lat, nt - n_lat)
    inner, q_dec, k_dec, blk = _ret_decay(log_decay, t)
    w = RET_WIDTH

    def col(j):
        return pl.BlockSpec((1, t, w), lambda d, b, i: (b, chunk(d, i), j))

    tab = pl.BlockSpec((t, RET_HEAD_DIM), lambda d, b, i: (chunk(d, i), 0))
    dec = pl.BlockSpec((1, RET_HEADS, t, RET_HEAD_DIM), lambda d, b, i: (d, 0, 0, 0))
    return pl.pallas_call(
        _ret_kernel,
        grid=(2, bsz, nc),
        in_specs=[col(1), col(2), col(3), tab, tab,
                  pl.BlockSpec((RET_HEAD_DIM, RET_HEAD_DIM), lambda d, b, i: (0, 0)),
                  pl.BlockSpec((1, RET_HEADS, t, t), lambda d, b, i: (d, 0, 0, 0)),
                  dec, dec,
                  pl.BlockSpec((1, RET_HEADS, SUBLANES, RET_HEAD_DIM), lambda d, b, i: (d, 0, 0, 0))],
        out_specs=pl.BlockSpec((1, 1, t, w), lambda d, b, i: (d, b, chunk(d, i), 0)),
        out_shape=jax.ShapeDtypeStruct((2, bsz, nt, w), F32),
        scratch_shapes=[pltpu.VMEM((RET_HEADS, RET_HEAD_DIM, RET_HEAD_DIM), F32)],
        compiler_params=_params(("arbitrary", "arbitrary", "arbitrary")),
        name="retention",
    )(proj, proj, proj, cos, sin, _rope_rotation_matrix(), inner, q_dec, k_dec, blk)


def _mix_out_kernel(u_ref, g_ref, ys_ref, or_ref, x_ref, mod_ref, d_ref, wglu_ref, wo_ref, o_ref):
    y = u_ref[0].astype(F32) * d_ref[...] + ys_ref[0, 0] + ys_ref[1, 0]
    y = jax.nn.gelu(y)
    a = y * jax.nn.sigmoid(jnp.dot(y.astype(BF16), wglu_ref[...], preferred_element_type=F32))
    o = or_ref[0, 0] + or_ref[1, 0]
    heads = []
    for hd in range(RET_HEADS):
        oh = o[:, hd * RET_HEAD_DIM:(hd + 1) * RET_HEAD_DIM]
        heads.append(oh * lax.rsqrt(jnp.mean(oh * oh, axis=-1, keepdims=True) + RMS_EPS))
    r = jnp.concatenate(heads, axis=1) * jax.nn.silu(g_ref[0].astype(F32))
    m = jnp.dot(a.astype(BF16), wo_ref[:S5_WIDTH, :], preferred_element_type=F32)
    m = m + jnp.dot(r.astype(BF16), wo_ref[S5_WIDTH:, :], preferred_element_type=F32)
    o_ref[0] = x_ref[0] + mod_ref[0, 2:3, :] * m


def mix_out(xc, proj, y_s5, o_ret, mod, d_skip, w_glu, w_out, n_lat):
    bsz, nt, d = xc.shape
    tm = TOKEN_TILE
    w = S5_WIDTH
    dirs = pl.BlockSpec((2, 1, tm, w), lambda b, j: (0, b, j, 0))
    return pl.pallas_call(
        _mix_out_kernel,
        grid=(bsz, nt // tm),
        in_specs=[pl.BlockSpec((1, tm, w), lambda b, j: (b, j, 0)),
                  pl.BlockSpec((1, tm, w), lambda b, j: (b, j, 4)),
                  dirs, dirs,
                  pl.BlockSpec((1, tm, d), lambda b, j: (b, j, 0)),
                  pl.BlockSpec((1, SUBLANES, d), _mod_index(n_lat // tm, bsz)),
                  pl.BlockSpec((1, w), lambda b, j: (0, 0)),
                  pl.BlockSpec((w, w), lambda b, j: (0, 0)),
                  pl.BlockSpec((w + RET_WIDTH, d), lambda b, j: (0, 0))],
        out_specs=pl.BlockSpec((1, tm, d), lambda b, j: (b, j, 0)),
        out_shape=jax.ShapeDtypeStruct((bsz, nt, d), F32),
        compiler_params=_params(("arbitrary", "arbitrary")),
        name="mix_out",
    )(proj, proj, y_s5, o_ret, xc, mod, d_skip.reshape(1, w), w_glu.astype(BF16), w_out.astype(BF16))


def _na_band_start(r0, rows):
    return jnp.clip(r0 - NA_WIN_ROWS // 2, 0, rows - NA_BAND_ROWS)


def _na_bias(rpb, rows):
    w = GRID_W
    n_var = (NA_BAND_ROWS - NA_Q_ROWS) // NA_Q_ROWS + 1
    half = NA_WIN_ROWS // 2
    r0 = np.array([v * NA_Q_ROWS if v * NA_Q_ROWS <= half else rows - NA_BAND_ROWS + v * NA_Q_ROWS
                   for v in range(n_var)])
    bs = np.clip(r0 - half, 0, rows - NA_BAND_ROWS)
    assert list(r0 - bs) == [v * NA_Q_ROWS for v in range(n_var)]
    r = r0[:, None] + np.arange(NA_Q_ROWS)[None, :]
    rs = np.clip(r - half, 0, rows - NA_WIN_ROWS)
    a = bs[:, None] + np.arange(NA_BAND_ROWS)[None, :]
    row_ok = (a[:, None, :] >= rs[:, :, None]) & (a[:, None, :] < rs[:, :, None] + NA_WIN_ROWS)
    row_off = np.clip(a[:, None, :] - r[:, :, None] + (NA_WIN_ROWS - 1), 0, 2 * NA_WIN_ROWS - 2)
    col = np.arange(w)
    col_start = np.clip(col - NA_WIN_COLS // 2, 0, w - NA_WIN_COLS)
    col_ok = (col[None, :] >= col_start[:, None]) & (col[None, :] < col_start[:, None] + NA_WIN_COLS)
    col_off = np.clip(col[None, :] - col[:, None] + (NA_WIN_COLS - 1), 0, 2 * NA_WIN_COLS - 2)
    b = rpb.astype(F32)[:, row_off[:, :, :, None, None], col_off[None, None, None]]
    ok = row_ok[:, :, :, None, None] & col_ok[None, None, None]
    b = jnp.where(ok[None], b, MASK_VALUE)
    b = b.transpose(1, 0, 2, 4, 3, 5)
    return b.reshape(n_var, NA_HEADS, NA_Q_ROWS * w, NA_BAND_ROWS * w)


def _na_kernel(q_ref, kb_ref, vb_ref, kc_ref, vc_ref, bias_ref, o_ref, s_ref, p_ref):
    scale = NA_HEAD_DIM ** -0.5
    nt_dims = (((1,), (1,)), ((), ()))
    nq = q_ref.shape[1]
    n_ctx = kc_ref.shape[1]
    n_slot = s_ref.shape[0]
    heads_per_tile = LANES // NA_HEAD_DIM
    lane = lax.broadcasted_iota(jnp.int32, (nq, LANES), 1)
    for j in range(NA_WIDTH // LANES):
        sl = slice(j * LANES, (j + 1) * LANES)
        q2 = q_ref[0, :, sl].astype(F32) * scale
        k2 = kb_ref[0, :, sl]
        v2 = vb_ref[0, :, sl]
        kc2 = kc_ref[0, :, sl]
        vc2 = vc_ref[0, :, sl]
        outs = []
        for hh in range(heads_per_tile):
            h = j * heads_per_tile + hh
            slot = h % n_slot
            in_head = (lane >= hh * NA_HEAD_DIM) & (lane < (hh + 1) * NA_HEAD_DIM)
            qm = jnp.where(in_head, q2, 0.0).astype(BF16)
            s_ref[slot, :, :n_ctx] = lax.dot_general(qm, kc2, nt_dims, preferred_element_type=F32)
            s_ref[slot, :, n_ctx:] = (lax.dot_general(qm, k2, nt_dims, preferred_element_type=F32)
                                      + bias_ref[0, h])
            inv = []
            for c in range(nq // NA_SOFTMAX_ROWS):
                rows = slice(c * NA_SOFTMAX_ROWS, (c + 1) * NA_SOFTMAX_ROWS)
                s = s_ref[slot, rows, :]
                p = jnp.exp(s - jnp.max(s, axis=-1, keepdims=True))
                inv.append(1.0 / jnp.sum(p, axis=-1, keepdims=True))
                p_ref[slot, rows, :] = p.astype(BF16)
            o = jnp.dot(p_ref[slot, :, :n_ctx], vc2, preferred_element_type=F32)
            o = o + jnp.dot(p_ref[slot, :, n_ctx:], v2, preferred_element_type=F32)
            outs.append(o * jnp.concatenate(inv, axis=0))
        o2 = outs[0]
        for hh in range(1, len(outs)):
            o2 = jnp.where(lane >= hh * NA_HEAD_DIM, outs[hh], o2)
        o_ref[0, :, sl] = o2.astype(o_ref.dtype)


def na_attention(qkv, rpb, n_lat):
    bsz, nt, _ = qkv.shape
    w = GRID_W
    rows = n_lat // w
    n_ctx = nt - n_lat
    nq = NA_Q_ROWS * w
    band = NA_BAND_ROWS * w
    n_all = n_ctx + band
    assert n_ctx % LANES == 0 and band % LANES == 0 and rows % NA_Q_ROWS == 0 and rows >= NA_BAND_ROWS + NA_WIN_ROWS // 2

    def band_start(g):
        return _na_band_start(g * NA_Q_ROWS, rows)

    def band_spec(j):
        return pl.BlockSpec((pl.Element(1), pl.Element(band), pl.Element(NA_WIDTH)),
                            lambda b, g: (b, band_start(g) * w, j * NA_WIDTH))

    def ctx_spec(j):
        return pl.BlockSpec((1, n_ctx, NA_WIDTH), lambda b, g: (b, n_lat // n_ctx, j))

    return pl.pallas_call(
        _na_kernel,
        grid=(bsz, rows // NA_Q_ROWS),
        in_specs=[pl.BlockSpec((1, nq, NA_WIDTH), lambda b, g: (b, g, 0)),
                  band_spec(1), band_spec(2), ctx_spec(1), ctx_spec(2),
                  pl.BlockSpec((1, NA_HEADS, nq, band), lambda b, g: (g - band_start(g) // NA_Q_ROWS, 0, 0, 0))],
        out_specs=pl.BlockSpec((1, nq, NA_WIDTH), lambda b, g: (b, g, 0)),
        out_shape=jax.ShapeDtypeStruct((bsz, n_lat, NA_WIDTH), BF16),
        scratch_shapes=[pltpu.VMEM((NA_SCORE_SLOTS, nq, n_all), F32), pltpu.VMEM((NA_SCORE_SLOTS, nq, n_all), BF16)],
        compiler_params=_params(("arbitrary", "arbitrary")),
        name="na_attention",
    )(qkv, qkv, qkv, qkv, qkv, _na_bias(rpb, rows))


def _proj_res_kernel(a_ref, x_ref, mod_ref, w_ref, o_ref):
    m = jnp.dot(a_ref[0], w_ref[...], preferred_element_type=F32)
    o_ref[0] = x_ref[0] + mod_ref[0, 2:3, :] * m


def proj_residual(a, x, mod, w):
    bsz, n, d = x.shape
    k = a.shape[-1]
    tm = TOKEN_TILE
    return pl.pallas_call(
        _proj_res_kernel,
        grid=(bsz, n // tm),
        in_specs=[pl.BlockSpec((1, tm, k), lambda b, j: (b, j, 0)),
                  pl.BlockSpec((1, tm, d), lambda b, j: (b, j, 0)),
                  pl.BlockSpec((1, SUBLANES, d), lambda b, j: (b, 0, 0)),
                  pl.BlockSpec((k, d), lambda b, j: (0, 0))],
        out_specs=pl.BlockSpec((1, tm, d), lambda b, j: (b, j, 0)),
        out_shape=jax.ShapeDtypeStruct((bsz, n, d), F32),
        compiler_params=_params(("arbitrary", "arbitrary")),
        name="proj_residual",
    )(a, x, mod, w.astype(BF16))


def _router_kernel(x_ref, mod_ref, nw_ref, wr_ref, br_ref, tri_ref, h_ref, idx_ref, gate_ref, rank_ref, cnt_ref,
                   run_ref):
    first_step = (pl.program_id(0) == 0) & (pl.program_id(1) == 0)

    @pl.when(first_step)
    def _():
        run_ref[...] = jnp.zeros_like(run_ref)

    h = _norm_mod(x_ref[0], nw_ref[...], mod_ref[0, 3:4, :], mod_ref[0, 4:5, :])
    h_ref[0] = h.astype(h_ref.dtype)
    logits = jnp.dot(h.astype(BF16), wr_ref[...], preferred_element_type=F32) + br_ref[...]
    lane = lax.broadcasted_iota(jnp.int32, logits.shape, 1)
    lane_f = lane.astype(F32)
    vals, idxs = [], []
    cur = logits
    for _ in range(MOE_TOP_K):
        m = jnp.max(cur, axis=-1, keepdims=True)
        first = jnp.min(jnp.where(cur == m, lane_f, float(LANES)), axis=-1, keepdims=True)
        vals.append(m)
        idxs.append(first)
        cur = jnp.where(lane_f == first, MASK_VALUE, cur)
    es = [jnp.exp(v - vals[0]) for v in vals]
    tot = es[0]
    for e in es[1:]:
        tot = tot + e
    onehots = [(lane_f == idxs[k]).astype(F32) for k in range(MOE_TOP_K)]
    multi = onehots[0]
    for oh in onehots[1:]:
        multi = multi + oh
    before = jnp.dot(tri_ref[...], multi.astype(BF16), preferred_element_type=F32) + run_ref[0:1, :]
    idx_out = jnp.zeros(logits.shape, F32)
    gate_out = jnp.zeros(logits.shape, F32)
    rank_out = jnp.zeros(logits.shape, F32)
    for k in range(MOE_TOP_K):
        idx_out = jnp.where(lane == k, idxs[k], idx_out)
        gate_out = jnp.where(lane == k, es[k] / tot, gate_out)
        rank_out = jnp.where(lane == k, jnp.sum(before * onehots[k], axis=-1, keepdims=True), rank_out)
    idx_ref[0] = idx_out.astype(jnp.int32)
    gate_ref[0] = gate_out
    rank_ref[0] = rank_out.astype(jnp.int32)
    run_ref[0:1, :] = run_ref[0:1, :] + jnp.sum(multi, axis=0, keepdims=True)
    cnt_ref[...] = jnp.broadcast_to(run_ref[0:1, :], cnt_ref.shape)


def router(xc, mod, norm_w, w_router, b_router, n_lat):
    bsz, nt, d = xc.shape
    tm = TOKEN_TILE
    wr = jnp.pad(w_router, ((0, 0), (0, LANES - N_EXPERTS))).astype(BF16)
    br = jnp.pad(b_router.astype(F32), (0, LANES - N_EXPERTS), constant_values=MASK_VALUE).reshape(1, LANES)
    tri = jnp.tril(jnp.ones((tm, tm), BF16), k=-1)
    tok = lambda n, dt: jax.ShapeDtypeStruct((bsz, nt, n), dt)
    out = lambda n: pl.BlockSpec((1, tm, n), lambda b, j: (b, j, 0))
    return pl.pallas_call(
        _router_kernel,
        grid=(bsz, nt // tm),
        in_specs=[pl.BlockSpec((1, tm, d), lambda b, j: (b, j, 0)),
                  pl.BlockSpec((1, SUBLANES, d), _mod_index(n_lat // tm, bsz)),
                  pl.BlockSpec((1, d), lambda b, j: (0, 0)),
                  pl.BlockSpec((d, LANES), lambda b, j: (0, 0)),
                  pl.BlockSpec((1, LANES), lambda b, j: (0, 0)),
                  pl.BlockSpec((tm, tm), lambda b, j: (0, 0))],
        out_specs=[out(d), out(LANES), out(LANES), out(LANES),
                   pl.BlockSpec((SUBLANES, LANES), lambda b, j: (0, 0))],
        out_shape=[tok(d, F32), tok(LANES, jnp.int32), tok(LANES, F32), tok(LANES, jnp.int32),
                   jax.ShapeDtypeStruct((SUBLANES, LANES), F32)],
        scratch_shapes=[pltpu.VMEM((SUBLANES, LANES), F32)],
        compiler_params=_params(("arbitrary", "arbitrary")),
        name="router",
    )(xc, mod, norm_w.reshape(1, d), wr, br, tri)


def _moe_kernel(be_ref, nu_ref, x_ref, w1_ref, b1_ref, w2_ref, b2_ref, o_ref, w1b_ref, w2b_ref):
    i = pl.program_id(0)
    ff = w2_ref.shape[1]
    used = i < nu_ref[0]

    @pl.when(used & ((i == 0) | (be_ref[i] != be_ref[jnp.maximum(i - 1, 0)])))
    def _():
        w1b_ref[...] = w1_ref[0].astype(BF16)
        w2b_ref[...] = w2_ref[0].astype(BF16)

    @pl.when(used)
    def _():
        t = jnp.dot(x_ref[...].astype(BF16), w1b_ref[...], preferred_element_type=F32) + b1_ref[0]
        x_glu = jnp.minimum(t[:, :ff], SWIGLU_LIMIT)
        x_lin = jnp.clip(t[:, ff:], -SWIGLU_LIMIT, SWIGLU_LIMIT)
        act = x_glu * jax.nn.sigmoid(SWIGLU_ALPHA * x_glu) * (x_lin + 1)
        y = jnp.dot(act.astype(BF16), w2b_ref[...], preferred_element_type=F32) + b2_ref[0]
        o_ref[...] = y.astype(o_ref.dtype)

    @pl.when(jnp.logical_not(used))
    def _():
        o_ref[...] = jnp.zeros_like(o_ref)


def moe_experts(xs, block_e, n_used, w1, b1, w2, b2):
    n_slots, d = xs.shape
    ne, _, ff2 = w1.shape
    ff = w2.shape[1]
    tm = MOE_TILE
    nb = n_slots // tm
    grid_spec = pltpu.PrefetchScalarGridSpec(
        num_scalar_prefetch=2,
        grid=(nb,),
        in_specs=[pl.BlockSpec((tm, d), lambda i, be, nu: (jnp.where(i < nu[0], i, 0), 0)),
                  pl.BlockSpec((1, d, ff2), lambda i, be, nu: (be[i], 0, 0)),
                  pl.BlockSpec((1, 1, ff2), lambda i, be, nu: (be[i], 0, 0)),
                  pl.BlockSpec((1, ff, d), lambda i, be, nu: (be[i], 0, 0)),
                  pl.BlockSpec((1, 1, d), lambda i, be, nu: (be[i], 0, 0))],
        out_specs=pl.BlockSpec((tm, d), lambda i, be, nu: (i, 0)),
        scratch_shapes=[pltpu.VMEM((d, ff2), BF16), pltpu.VMEM((ff, d), BF16)],
    )
    return pl.pallas_call(
        _moe_kernel,
        grid_spec=grid_spec,
        out_shape=jax.ShapeDtypeStruct((n_slots, d), F32),
        compiler_params=_params(("arbitrary",)),
        name="moe_experts",
    )(block_e, n_used, xs, w1, b1.reshape(ne, 1, ff2), w2, b2.reshape(ne, 1, d))


def _moe_plan(idx, rank, counts, n_tok):
    counts = counts.astype(jnp.int32)
    padded = (counts + MOE_TILE - 1) // MOE_TILE * MOE_TILE
    padded_ends = jnp.cumsum(padded)
    padded_starts = padded_ends - padded
    onehot = idx[..., None] == jnp.arange(N_EXPERTS, dtype=jnp.int32)
    slot_of = rank + jnp.sum(jnp.where(onehot, padded_starts, 0), axis=-1)
    n_blocks = -(-n_tok * MOE_TOP_K // MOE_TILE) + N_EXPERTS
    block_row = jnp.arange(n_blocks, dtype=jnp.int32) * MOE_TILE
    block_e = jnp.minimum(jnp.sum(padded_ends[None, :] <= block_row[:, None], axis=1), N_EXPERTS - 1)
    n_used = padded_ends[-1:] // MOE_TILE
    last_block = jnp.concatenate([jnp.where(counts > 0, padded_ends - MOE_TILE, -1), n_used])
    slot_tiles = slot_of.astype(jnp.int32).reshape(n_tok // TOKEN_TILE, 1, TOKEN_TILE * MOE_TOP_K)
    return slot_tiles, block_e.astype(jnp.int32), n_used.astype(jnp.int32), last_block.astype(jnp.int32), n_blocks


def _dispatch_kernel(slot_ref, last_ref, h_ref, xs_ref, zero_ref, sem):
    tm = h_ref.shape[0]

    @pl.when(pl.program_id(0) == 0)
    def _():
        zero_ref[...] = jnp.zeros_like(zero_ref)

        def zero_copy(e):
            row = pl.multiple_of(jnp.maximum(last_ref[e], 0), MOE_TILE)
            return pltpu.make_async_copy(zero_ref, xs_ref.at[pl.ds(row, MOE_TILE)], sem)

        def block_copy(i):
            return pltpu.make_async_copy(zero_ref, xs_ref.at[pl.ds(pl.multiple_of(i * MOE_TILE, MOE_TILE), MOE_TILE)], sem)

        n_used = last_ref[N_EXPERTS]
        n_blocks = xs_ref.shape[0] // MOE_TILE
        lax.fori_loop(n_used, n_blocks, lambda i, c: (block_copy(i).start(), c)[1], 0)
        for e in range(N_EXPERTS):
            pl.when(last_ref[e] >= 0)(lambda e=e: zero_copy(e).start())
        for e in range(N_EXPERTS):
            pl.when(last_ref[e] >= 0)(lambda e=e: zero_copy(e).wait())
        lax.fori_loop(n_used, n_blocks, lambda i, c: (block_copy(i).wait(), c)[1], 0)

    def issue(t, carry):
        for k in range(MOE_TOP_K):
            s = slot_ref[0, 0, t * MOE_TOP_K + k]
            pltpu.make_async_copy(h_ref.at[pl.ds(t, 1)], xs_ref.at[pl.ds(s, 1)], sem).start()
        return carry

    lax.fori_loop(0, tm, issue, 0)
    for k in range(MOE_TOP_K):
        pltpu.make_async_copy(h_ref, xs_ref.at[pl.ds(0, tm)], sem).wait()


def moe_dispatch(h, slot_tiles, last_block, n_blocks):
    n_tok, d = h.shape
    tm = TOKEN_TILE
    return pl.pallas_call(
        _dispatch_kernel,
        grid=(n_tok // tm,),
        in_specs=[pl.BlockSpec((1, 1, tm * MOE_TOP_K), lambda i: (i, 0, 0), memory_space=pltpu.SMEM),
                  pl.BlockSpec(memory_space=pltpu.SMEM),
                  pl.BlockSpec((tm, d), lambda i: (i, 0))],
        out_specs=pl.BlockSpec(memory_space=pl.ANY),
        out_shape=jax.ShapeDtypeStruct((n_blocks * MOE_TILE, d), F32),
        scratch_shapes=[pltpu.VMEM((MOE_TILE, d), F32), pltpu.SemaphoreType.DMA(())],
        compiler_params=_params(("arbitrary",)),
        name="moe_dispatch",
    )(slot_tiles, last_block, h)


def _combine_kernel(slot_ref, x_ref, gate_ref, mod_ref, *rest, final):
    ys_ref, o_ref, ybuf_ref, sem = rest[-4:]
    tm = x_ref.shape[1]

    def issue(t, carry):
        for k in range(MOE_TOP_K):
            s = slot_ref[0, 0, t * MOE_TOP_K + k]
            pltpu.make_async_copy(ys_ref.at[pl.ds(s, 1)], ybuf_ref.at[k, pl.ds(t, 1)], sem).start()
        return carry

    lax.fori_loop(0, tm, issue, 0)
    for k in range(MOE_TOP_K):
        pltpu.make_async_copy(ys_ref.at[pl.ds(0, tm)], ybuf_ref.at[k], sem).wait()

    gate = gate_ref[0]
    y = None
    for k in range(MOE_TOP_K):
        term = gate[:, k:k + 1] * ybuf_ref[k]
        y = term if y is None else y + term
    x = x_ref[0] + mod_ref[0, 5:6, :] * y
    if final:
        fw_ref = rest[0]
        x = x * lax.rsqrt(jnp.mean(x * x, axis=-1, keepdims=True) + RMS_EPS) * fw_ref[...]
    o_ref[0] = x


def moe_combine(xc, ys, slot_tiles, gates, mod, n_lat, final_w=None):
    bsz, nt, d = xc.shape
    tm = TOKEN_TILE
    tiles = nt // tm
    in_specs = [pl.BlockSpec((1, 1, tm * MOE_TOP_K), lambda b, j: (b * tiles + j, 0, 0), memory_space=pltpu.SMEM),
                pl.BlockSpec((1, tm, d), lambda b, j: (b, j, 0)),
                pl.BlockSpec((1, tm, LANES), lambda b, j: (b, j, 0)),
                pl.BlockSpec((1, SUBLANES, d), _mod_index(n_lat // tm, bsz))]
    args = [slot_tiles, xc, gates, mod]
    if final_w is not None:
        in_specs.append(pl.BlockSpec((1, d), lambda b, j: (0, 0)))
        args.append(final_w.reshape(1, d))
    in_specs.append(pl.BlockSpec(memory_space=pl.ANY))
    args.append(ys)
    return pl.pallas_call(
        functools.partial(_combine_kernel, final=final_w is not None),
        grid=(bsz, tiles),
        in_specs=in_specs,
        out_specs=pl.BlockSpec((1, tm, d), lambda b, j: (b, j, 0)),
        out_shape=jax.ShapeDtypeStruct((bsz, nt, d), F32),
        scratch_shapes=[pltpu.VMEM((MOE_TOP_K, tm, d), F32), pltpu.SemaphoreType.DMA(())],
        compiler_params=_params(("arbitrary", "arbitrary")),
        name="moe_combine",
    )(*args)


def moe_layer(xc, mod, norm_w, w_router, b_router, w1, b1, w2, b2, n_lat, final_w=None):
    bsz, nt, d = xc.shape
    n_tok = bsz * nt
    h, idx, gates, rank, counts = router(xc, mod, norm_w, w_router, b_router, n_lat)
    top = lambda a: a.reshape(n_tok, LANES)[:, :MOE_TOP_K]
    slot_tiles, block_e, n_used, last_block, n_blocks = _moe_plan(top(idx), top(rank), counts[0, :N_EXPERTS], n_tok)
    xs = moe_dispatch(h.reshape(n_tok, d), slot_tiles, last_block, n_blocks)
    ys = moe_experts(xs, block_e, n_used, w1, b1, w2, b2)
    return moe_combine(xc, ys, slot_tiles, gates, mod, n_lat, final_w)


def kernel(x, c, ctx, c_ctx, ada_w, ada_b, norm_w, final_norm_w, ev_w_in, ev_w_out, s5_lam_re, s5_lam_im, s5_log_step, s5_b_re, s5_b_im, s5_c_re, s5_c_im, s5_d, s5_w_glu, ret_log_decay, na_w_qkv, na_w_o, na_rpb, moe_w_router, moe_b_router, moe_w1, moe_b1, moe_w2, moe_b2):
    bsz, n_lat, d = x.shape
    n_ctx = ctx.shape[1]
    depth = ada_w.shape[0]
    mod = adaln_table(c, c_ctx, ada_w, ada_b)
    xc = jnp.concatenate([x, ctx], axis=1)
    for i in range(depth):
        last = i == depth - 1
        j = i // 2
        if i % 2 == 0:
            proj = norm_proj(xc, mod[i], norm_w[i, 0], ev_w_in[j], n_lat, 0)
            s5p = _s5_layout(s5_lam_re[j], s5_lam_im[j], s5_log_step[j], s5_b_re[j], s5_b_im[j],
                             s5_c_re[j], s5_c_im[j])
            y_s5 = s5_scan(proj, s5p, n_lat)
            o_ret = retention(proj, ret_log_decay[j], n_lat)
            xc = mix_out(xc, proj, y_s5, o_ret, mod[i], s5_d[j], s5_w_glu[j], ev_w_out[j], n_lat)
        else:
            qkv = norm_proj(xc, mod[i], norm_w[i, 0], na_w_qkv[j], n_lat, 0)
            att = na_attention(qkv, na_rpb[j], n_lat)
            assert last, "an odd layer is only supported as the final layer (no context output needed)"
            xc = proj_residual(att, xc[:, :n_lat], mod[i], na_w_o[j])
        if last:
            return moe_layer(xc[:, :n_lat], mod[i], norm_w[i, 1], moe_w_router[i], moe_b_router[i],
                             moe_w1[i], moe_b1[i], moe_w2[i], moe_b2[i], n_lat, final_norm_w)
        xc = moe_layer(xc, mod[i], norm_w[i, 1], moe_w_router[i], moe_b_router[i],
                       moe_w1[i], moe_b1[i], moe_w2[i], moe_b2[i], n_lat)
```

```python
import functools
import math

import numpy as np
import jax
import jax.numpy as jnp
from jax import lax
from jax.experimental import pallas as pl
from jax.experimental.pallas import tpu as pltpu

F32 = jnp.float32
BF16 = jnp.bfloat16

GRID_W = 64
RMS_EPS = 1e-6
S5_WIDTH = 512
S5_GROUP = 16
S5_GROUPS = S5_WIDTH // S5_GROUP
S5_STATE = 64
RET_HEADS = 4
RET_HEAD_DIM = 128
RET_WIDTH = RET_HEADS * RET_HEAD_DIM
ROPE_BASE = 10000.0
NA_HEADS = 16
NA_HEAD_DIM = 64
NA_WIDTH = NA_HEADS * NA_HEAD_DIM
NA_WIN_ROWS = 8
NA_WIN_COLS = 16
N_EXPERTS = 32
MOE_TOP_K = 4
SWIGLU_LIMIT = 7.0
SWIGLU_ALPHA = 1.702

LANES = 128
SUBLANES = 8
MXU_DIM = 256
V7X_VMEM_BYTES = 64 * 1024 * 1024
VMEM_LIMIT = V7X_VMEM_BYTES * 7 // 8

TOKEN_TILE = 256
S5_CHUNK = 256
S5_PITCH = S5_CHUNK + SUBLANES
RET_CHUNK = 256
MOE_TILE = 512
NA_Q_ROWS = 2
NA_BAND_ROWS = 10
NA_SOFTMAX_ROWS = 16
NA_SCORE_SLOTS = 4
assert NA_BAND_ROWS >= NA_WIN_ROWS + NA_Q_ROWS - 1 and (NA_BAND_ROWS - NA_Q_ROWS) % NA_Q_ROWS == 0
MASK_VALUE = -1e30

S5_SLICES = 8
assert S5_SLICES * MXU_DIM == S5_GROUPS * S5_STATE


def _params(sem):
    return pltpu.CompilerParams(dimension_semantics=sem, vmem_limit_bytes=VMEM_LIMIT)


def _adaln_kernel(c_ref, w_ref, b_ref, o_ref):
    c = c_ref[...]
    s = c * jax.nn.sigmoid(c)
    o_ref[0] = jnp.dot(s, w_ref[0], preferred_element_type=F32,
                       precision=lax.Precision.HIGHEST) + b_ref[0]


def adaln_table(c, c_ctx, ada_w, ada_b):
    depth, d, d6 = ada_w.shape
    bsz = c.shape[0]
    cond = jnp.concatenate([c, c_ctx[None, :]], axis=0)
    cond = jnp.pad(cond, ((0, SUBLANES - (bsz + 1)), (0, 0)))
    tn = d6 // 4
    out = pl.pallas_call(
        _adaln_kernel,
        grid=(depth, d6 // tn),
        in_specs=[pl.BlockSpec((SUBLANES, d), lambda i, j: (0, 0)),
                  pl.BlockSpec((1, d, tn), lambda i, j: (i, 0, j)),
                  pl.BlockSpec((1, 1, tn), lambda i, j: (i, 0, j))],
        out_specs=pl.BlockSpec((1, SUBLANES, tn), lambda i, j: (i, 0, j)),
        out_shape=jax.ShapeDtypeStruct((depth, SUBLANES, d6), F32),
        compiler_params=_params(("arbitrary", "arbitrary")),
        name="adaln",
    )(cond, ada_w, ada_b.reshape(depth, 1, d6))
    tab = out[:, :bsz + 1].reshape(depth, bsz + 1, 6, d)
    return jnp.pad(tab, ((0, 0), (0, 0), (0, 2), (0, 0)))


def _norm_mod(x, nw, shift, scale):
    y = x * lax.rsqrt(jnp.mean(x * x, axis=-1, keepdims=True) + RMS_EPS)
    return (y * nw) * (1 + scale) + shift


def _mod_index(n_lat_tiles, bsz):
    def index(b, j):
        return (jnp.where(j >= n_lat_tiles, bsz, b), 0, 0)
    return index


def _proj_kernel(x_ref, mod_ref, nw_ref, w_ref, o_ref, *, shift_row):
    h = _norm_mod(x_ref[0], nw_ref[...], mod_ref[0, shift_row:shift_row + 1, :],
                  mod_ref[0, shift_row + 1:shift_row + 2, :])
    o_ref[0] = jnp.dot(h.astype(BF16), w_ref[...], preferred_element_type=F32).astype(o_ref.dtype)


def norm_proj(xc, mod, norm_w, w, n_lat, shift_row):
    bsz, nt, d = xc.shape
    n = w.shape[1]
    tm = TOKEN_TILE
    return pl.pallas_call(
        functools.partial(_proj_kernel, shift_row=shift_row),
        grid=(bsz, nt // tm),
        in_specs=[pl.BlockSpec((1, tm, d), lambda b, j: (b, j, 0)),
                  pl.BlockSpec((1, SUBLANES, d), _mod_index(n_lat // tm, bsz)),
                  pl.BlockSpec((1, d), lambda b, j: (0, 0)),
                  pl.BlockSpec((d, n), lambda b, j: (0, 0))],
        out_specs=pl.BlockSpec((1, tm, n), lambda b, j: (b, j, 0)),
        out_shape=jax.ShapeDtypeStruct((bsz, nt, n), BF16),
        compiler_params=_params(("arbitrary", "arbitrary")),
        name="norm_proj",
    )(xc, mod, norm_w.reshape(1, d), w.astype(BF16))


def _s5_discretize(lam_re, lam_im, log_step, b_re, b_im):
    lam_re = jnp.minimum(lam_re.astype(F32), -1e-4)
    lam_im = lam_im.astype(F32)
    step = jnp.exp(log_step.astype(F32))[..., None]
    mag = jnp.exp(lam_re * step)
    ang = lam_im * step
    a_re, a_im = mag * jnp.cos(ang), mag * jnp.sin(ang)
    den = lam_re * lam_re + lam_im * lam_im
    n_re, n_im = a_re - 1.0, a_im
    co_re = (n_re * lam_re + n_im * lam_im) / den
    co_im = (n_im * lam_re - n_re * lam_im) / den
    b_re, b_im = b_re.astype(F32), b_im.astype(F32)
    bb_re = co_re[..., None] * b_re - co_im[..., None] * b_im
    bb_im = co_re[..., None] * b_im + co_im[..., None] * b_re
    return a_re, a_im, bb_re, bb_im


def _s5_layout(lam_re, lam_im, log_step, b_re, b_im, c_re, c_im):
    a_re, a_im, bb_re, bb_im = _s5_discretize(lam_re, lam_im, log_step, b_re, b_im)
    gh = S5_GROUPS // 2
    nq = S5_STATE // S5_GROUP
    eye = jnp.eye(gh, dtype=F32)

    def arrange_a(a):
        a = a.reshape(2, 2, gh, nq, S5_GROUP)
        return a.transpose(0, 1, 3, 2, 4).reshape(2, S5_SLICES, MXU_DIM)

    def arrange_b(bb):
        bb = bb.reshape(2, 2, gh, nq, S5_GROUP, S5_GROUP)
        m = jnp.einsum('dhgqni,Gg->dhqGign', bb, eye)
        return m.reshape(2, S5_SLICES, MXU_DIM, MXU_DIM).astype(BF16)

    def arrange_c(c):
        c = c.astype(F32).reshape(2, 2, gh, S5_GROUP, nq, S5_GROUP)
        m = jnp.einsum('dhgiqn,gG->dhqgnGi', c, eye)
        return m.reshape(2, S5_SLICES, MXU_DIM, MXU_DIM).astype(BF16)

    return (arrange_a(a_re), arrange_a(a_im), arrange_b(bb_re), arrange_b(bb_im),
            arrange_c(c_re), arrange_c(-c_im.astype(F32)))


def _s5_kernel(u_ref, bre_ref, bim_ref, cre_ref, cim_ref, are_ref, aim_ref, y_ref, st_ref, h_ref, *, nb):
    t_chunk = S5_CHUNK
    pitch = S5_PITCH
    half = MXU_DIM // LANES
    d = pl.program_id(0)

    @pl.when(pl.program_id(1) == 0)
    def _():
        h_ref[...] = jnp.zeros_like(h_ref)

    def slab(b, c, lh):
        return (b * 2 + c) * half + lh

    for b in range(nb):
        for h in range(2):
            ub = u_ref[b, :, h * MXU_DIM:(h + 1) * MXU_DIM]
            for q in range(S5_SLICES // 2):
                k = h * (S5_SLICES // 2) + q
                for c, w_ref in ((0, bre_ref), (1, bim_ref)):
                    r = jnp.dot(ub, w_ref[0, k], preferred_element_type=F32)
                    for lh in range(half):
                        st_ref[slab(b, c, lh), k * pitch:k * pitch + t_chunk, :] = r[:, lh * LANES:(lh + 1) * LANES]

    ar = [are_ref[0, :, lh * LANES:(lh + 1) * LANES] for lh in range(half)]
    ai = [aim_ref[0, :, lh * LANES:(lh + 1) * LANES] for lh in range(half)]

    def step(t, carry):
        tt = jnp.where(d == 0, t, t_chunk - 1 - t)
        new = []
        for b in range(nb):
            for lh in range(half):
                hr, hi = carry[2 * (b * half + lh)], carry[2 * (b * half + lh) + 1]
                rows = pl.ds(tt, S5_SLICES, stride=pitch)
                xr = st_ref[slab(b, 0, lh), rows, :]
                xi = st_ref[slab(b, 1, lh), rows, :]
                nr = ar[lh] * hr - ai[lh] * hi + xr
                ni = ar[lh] * hi + ai[lh] * hr + xi
                st_ref[slab(b, 0, lh), rows, :] = nr
                st_ref[slab(b, 1, lh), rows, :] = ni
                new += [nr, ni]
        return tuple(new)

    n_state = nb * half * 2
    carry = lax.fori_loop(0, t_chunk, step, tuple(h_ref[j] for j in range(n_state)), unroll=4)
    for j in range(n_state):
        h_ref[j] = carry[j]

    for b in range(nb):
        for h in range(2):
            acc = None
            for q in range(S5_SLICES // 2):
                k = h * (S5_SLICES // 2) + q
                for c, w_ref in ((0, cre_ref), (1, cim_ref)):
                    s = jnp.concatenate(
                        [st_ref[slab(b, c, lh), k * pitch:k * pitch + t_chunk, :] for lh in range(half)], axis=1)
                    term = jnp.dot(s.astype(BF16), w_ref[0, k], preferred_element_type=F32)
                    acc = term if acc is None else acc + term
            y_ref[0, b, :, h * MXU_DIM:(h + 1) * MXU_DIM] = acc


def _scan_chunk_index(n_chunks, n_ctx_chunks):
    def chunk(d, i):
        fwd = lax.rem(i + (n_chunks - n_ctx_chunks), n_chunks)
        return jnp.where(d == 0, fwd, n_chunks - 1 - i)
    return chunk


def s5_scan(proj, s5p, n_lat):
    bsz, nt, _ = proj.shape
    a_re, a_im, bb_re, bb_im, cc_re, cc_im = s5p
    t = S5_CHUNK
    nc = nt // t
    chunk = _scan_chunk_index(nc, (nt - n_lat) // t)
    wspec = pl.BlockSpec((1, S5_SLICES, MXU_DIM, MXU_DIM), lambda d, i: (d, 0, 0, 0))
    aspec = pl.BlockSpec((1, S5_SLICES, MXU_DIM), lambda d, i: (d, 0, 0))
    n_slab = bsz * 2 * (MXU_DIM // LANES)
    return pl.pallas_call(
        functools.partial(_s5_kernel, nb=bsz),
        grid=(2, nc),
        in_specs=[pl.BlockSpec((bsz, t, S5_WIDTH), lambda d, i: (0, chunk(d, i), 0)),
                  wspec, wspec, wspec, wspec, aspec, aspec],
        out_specs=pl.BlockSpec((1, bsz, t, S5_WIDTH), lambda d, i: (d, 0, chunk(d, i), 0)),
        out_shape=jax.ShapeDtypeStruct((2, bsz, nt, S5_WIDTH), F32),
        scratch_shapes=[pltpu.VMEM((n_slab, S5_SLICES * S5_PITCH, LANES), F32),
                        pltpu.VMEM((n_slab, S5_SLICES, LANES), F32)],
        compiler_params=_params(("arbitrary", "arbitrary")),
        name="s5_scan",
    )(proj, bb_re, bb_im, cc_re, cc_im, a_re, a_im)


def _rope_tables(n_lat, n_ctx):
    half = RET_HEAD_DIM // 4
    freq = ROPE_BASE ** (-jnp.arange(half, dtype=F32) / half)
    t = jnp.arange(n_lat)
    row = (t // GRID_W).astype(F32)
    col = (t % GRID_W).astype(F32)
    ang_r = row[:, None] * freq[None, :]
    ang_c = col[:, None] * freq[None, :]
    cos = jnp.concatenate([jnp.cos(ang_r)] * 2 + [jnp.cos(ang_c)] * 2, axis=-1)
    sin = jnp.concatenate([jnp.sin(ang_r)] * 2 + [jnp.sin(ang_c)] * 2, axis=-1)
    cos = jnp.concatenate([cos, jnp.ones((n_ctx, RET_HEAD_DIM), F32)], axis=0)
    sin = jnp.concatenate([sin, jnp.zeros((n_ctx, RET_HEAD_DIM), F32)], axis=0)
    return cos, sin


def _rope_rotation_matrix():
    blk = RET_HEAD_DIM // 2
    half = blk // 2
    r = np.zeros((RET_HEAD_DIM, RET_HEAD_DIM), np.float32)
    for base in (0, blk):
        for l in range(half):
            r[base + l + half, base + l] = -1.0
            r[base + l, base + l + half] = 1.0
    return jnp.asarray(r, BF16)


def _ret_decay(log_decay, t):
    lg = log_decay.astype(F32)
    scale = RET_HEAD_DIM ** -0.5
    idx = jnp.arange(t, dtype=F32)
    diff = idx[:, None] - idx[None, :]
    diff = jnp.stack([diff, -diff])[:, None]
    inner = jnp.where(diff >= 0, jnp.exp(lg[:, :, None, None] * jnp.maximum(diff, 0.0)), 0.0) * scale
    pos = jnp.stack([idx, t - 1.0 - idx])
    q_dec = jnp.exp(lg[:, :, None] * (pos[:, None, :] + 1.0))
    k_dec = jnp.exp(lg[:, :, None] * (t - 1.0 - pos[:, None, :])) * scale
    blk = jnp.exp(lg * t)
    bcast = lambda v: jnp.broadcast_to(v[..., None], v.shape + (RET_HEAD_DIM,))
    return inner, bcast(q_dec), bcast(k_dec), jnp.broadcast_to(blk[:, :, None, None], (2, RET_HEADS, SUBLANES, RET_HEAD_DIM))


def _ret_kernel(q_ref, k_ref, v_ref, cos_ref, sin_ref, rot_ref, inner_ref, qd_ref, kd_ref, bd_ref, o_ref, s_ref):
    @pl.when(pl.program_id(2) == 0)
    def _():
        s_ref[...] = jnp.zeros_like(s_ref)

    cos = cos_ref[...]
    sin = sin_ref[...]
    rot = rot_ref[...]
    nt_dims = (((1,), (1,)), ((), ()))
    for hd in range(RET_HEADS):
        sl = slice(hd * RET_HEAD_DIM, (hd + 1) * RET_HEAD_DIM)
        q = q_ref[0, :, sl]
        k = k_ref[0, :, sl]
        v = v_ref[0, :, sl]
        qr = q.astype(F32) * cos + jnp.dot(q, rot, preferred_element_type=F32) * sin
        kr = k.astype(F32) * cos + jnp.dot(k, rot, preferred_element_type=F32) * sin
        qb = qr.astype(BF16)
        att = lax.dot_general(qb, kr.astype(BF16), nt_dims, preferred_element_type=F32) * inner_ref[0, hd]
        s = s_ref[hd]
        o = jnp.dot(att.astype(BF16), v, preferred_element_type=F32)
        o = o + jnp.dot(qb, s.astype(BF16), preferred_element_type=F32) * qd_ref[0, hd]
        kd_t = (kr * kd_ref[0, hd]).T.astype(BF16)
        s_ref[hd] = bd_ref[0, hd, 0:1, :] * s + jnp.dot(kd_t, v, preferred_element_type=F32)
        o_ref[0, 0, :, sl] = o


def retention(proj, log_decay, n_lat):
    bsz, nt, _ = proj.shape
    t = RET_CHUNK
    nc = nt // t
    chunk = _scan_chunk_index(nc, (nt - n_lat) // t)
    cos, sin = _rope_tables(n_lat, nt - n_lat)
    inner, q_dec, k_dec, blk = _ret_decay(log_decay, t)
    w = RET_WIDTH

    def col(j):
        return pl.BlockSpec((1, t, w), lambda d, b, i: (b, chunk(d, i), j))

    tab = pl.BlockSpec((t, RET_HEAD_DIM), lambda d, b, i: (chunk(d, i), 0))
    dec = pl.BlockSpec((1, RET_HEADS, t, RET_HEAD_DIM), lambda d, b, i: (d, 0, 0, 0))
    return pl.pallas_call(
        _ret_kernel,
        grid=(2, bsz, nc),
        in_specs=[col(1), col(2), col(3), tab, tab,
                  pl.BlockSpec((RET_HEAD_DIM, RET_HEAD_DIM), lambda d, b, i: (0, 0)),
                  pl.BlockSpec((1, RET_HEADS, t, t), lambda d, b, i: (d, 0, 0, 0)),
                  dec, dec,
                  pl.BlockSpec((1, RET_HEADS, SUBLANES, RET_HEAD_DIM), lambda d, b, i: (d, 0, 0, 0))],
        out_specs=pl.BlockSpec((1, 1, t, w), lambda d, b, i: (d, b, chunk(d, i), 0)),
        out_shape=jax.ShapeDtypeStruct((2, bsz, nt, w), F32),
        scratch_shapes=[pltpu.VMEM((RET_HEADS, RET_HEAD_DIM, RET_HEAD_DIM), F32)],
        compiler_params=_params(("arbitrary", "arbitrary", "arbitrary")),
        name="retention",
    )(proj, proj, proj, cos, sin, _rope_rotation_matrix(), inner, q_dec, k_dec, blk)


def _mix_out_kernel(u_ref, g_ref, ys_ref, or_ref, x_ref, mod_ref, d_ref, wglu_ref, wo_ref, o_ref):
    y = u_ref[0].astype(F32) * d_ref[...] + ys_ref[0, 0] + ys_ref[1, 0]
    y = jax.nn.gelu(y)
    a = y * jax.nn.sigmoid(jnp.dot(y.astype(BF16), wglu_ref[...], preferred_element_type=F32))
    o = or_ref[0, 0] + or_ref[1, 0]
    heads = []
    for hd in range(RET_HEADS):
        oh = o[:, hd * RET_HEAD_DIM:(hd + 1) * RET_HEAD_DIM]
        heads.append(oh * lax.rsqrt(jnp.mean(oh * oh, axis=-1, keepdims=True) + RMS_EPS))
    r = jnp.concatenate(heads, axis=1) * jax.nn.silu(g_ref[0].astype(F32))
    m = jnp.dot(a.astype(BF16), wo_ref[:S5_WIDTH, :], preferred_element_type=F32)
    m = m + jnp.dot(r.astype(BF16), wo_ref[S5_WIDTH:, :], preferred_element_type=F32)
    o_ref[0] = x_ref[0] + mod_ref[0, 2:3, :] * m


def mix_out(xc, proj, y_s5, o_ret, mod, d_skip, w_glu, w_out, n_lat):
    bsz, nt, d = xc.shape
    tm = TOKEN_TILE
    w = S5_WIDTH
    dirs = pl.BlockSpec((2, 1, tm, w), lambda b, j: (0, b, j, 0))
    return pl.pallas_call(
        _mix_out_kernel,
        grid=(bsz, nt // tm),
        in_specs=[pl.BlockSpec((1, tm, w), lambda b, j: (b, j, 0)),
                  pl.BlockSpec((1, tm, w), lambda b, j: (b, j, 4)),
                  dirs, dirs,
                  pl.BlockSpec((1, tm, d), lambda b, j: (b, j, 0)),
                  pl.BlockSpec((1, SUBLANES, d), _mod_index(n_lat // tm, bsz)),
                  pl.BlockSpec((1, w), lambda b, j: (0, 0)),
                  pl.BlockSpec((w, w), lambda b, j: (0, 0)),
                  pl.BlockSpec((w + RET_WIDTH, d), lambda b, j: (0, 0))],
        out_specs=pl.BlockSpec((1, tm, d), lambda b, j: (b, j, 0)),
        out_shape=jax.ShapeDtypeStruct((bsz, nt, d), F32),
        compiler_params=_params(("arbitrary", "arbitrary")),
        name="mix_out",
    )(proj, proj, y_s5, o_ret, xc, mod, d_skip.reshape(1, w), w_glu.astype(BF16), w_out.astype(BF16))


def _na_band_start(r0, rows):
    return jnp.clip(r0 - NA_WIN_ROWS // 2, 0, rows - NA_BAND_ROWS)


def _na_bias(rpb, rows):
    w = GRID_W
    n_var = (NA_BAND_ROWS - NA_Q_ROWS) // NA_Q_ROWS + 1
    half = NA_WIN_ROWS // 2
    r0 = np.array([v * NA_Q_ROWS if v * NA_Q_ROWS <= half else rows - NA_BAND_ROWS + v * NA_Q_ROWS
                   for v in range(n_var)])
    bs = np.clip(r0 - half, 0, rows - NA_BAND_ROWS)
    assert list(r0 - bs) == [v * NA_Q_ROWS for v in range(n_var)]
    r = r0[:, None] + np.arange(NA_Q_ROWS)[None, :]
    rs = np.clip(r - half, 0, rows - NA_WIN_ROWS)
    a = bs[:, None] + np.arange(NA_BAND_ROWS)[None, :]
    row_ok = (a[:, None, :] >= rs[:, :, None]) & (a[:, None, :] < rs[:, :, None] + NA_WIN_ROWS)
    row_off = np.clip(a[:, None, :] - r[:, :, None] + (NA_WIN_ROWS - 1), 0, 2 * NA_WIN_ROWS - 2)
    col = np.arange(w)
    col_start = np.clip(col - NA_WIN_COLS // 2, 0, w - NA_WIN_COLS)
    col_ok = (col[None, :] >= col_start[:, None]) & (col[None, :] < col_start[:, None] + NA_WIN_COLS)
    col_off = col[None, :] - col[:, None] + (NA_WIN_COLS - 1)
    col_sel = (col_off[:, :, None] == np.arange(2 * NA_WIN_COLS - 1)) & col_ok[:, :, None]
    tiles = jnp.sum(rpb.astype(F32)[:, :, None, None, :] * jnp.asarray(col_sel, F32)[None, None], axis=-1)
    tiles = jnp.where(jnp.asarray(col_ok)[None, None], tiles, MASK_VALUE)
    masked = jnp.full((NA_HEADS, w, w), MASK_VALUE, F32)
    groups = []
    for v in range(n_var):
        per_row = []
        for rr in range(NA_Q_ROWS):
            per_row.append(jnp.concatenate(
                [tiles[:, int(row_off[v, rr, i])] if row_ok[v, rr, i] else masked for i in range(NA_BAND_ROWS)],
                axis=-1))
        groups.append(jnp.concatenate(per_row, axis=1))
    return jnp.stack(groups)


def _na_kernel(q_ref, kb_ref, vb_ref, kc_ref, vc_ref, bias_ref, o_ref, s_ref, p_ref):
    scale = NA_HEAD_DIM ** -0.5
    nt_dims = (((1,), (1,)), ((), ()))
    nq = q_ref.shape[1]
    n_ctx = kc_ref.shape[1]
    n_slot = s_ref.shape[0]
    heads_per_tile = LANES // NA_HEAD_DIM
    lane = lax.broadcasted_iota(jnp.int32, (nq, LANES), 1)
    for j in range(NA_WIDTH // LANES):
        sl = slice(j * LANES, (j + 1) * LANES)
        q2 = q_ref[0, :, sl].astype(F32) * scale
        k2 = kb_ref[0, :, sl]
        v2 = vb_ref[0, :, sl]
        kc2 = kc_ref[0, :, sl]
        vc2 = vc_ref[0, :, sl]
        outs = []
        for hh in range(heads_per_tile):
            h = j * heads_per_tile + hh
            slot = h % n_slot
            in_head = (lane >= hh * NA_HEAD_DIM) & (lane < (hh + 1) * NA_HEAD_DIM)
            qm = jnp.where(in_head, q2, 0.0).astype(BF16)
            s_ref[slot, :, :n_ctx] = lax.dot_general(qm, kc2, nt_dims, preferred_element_type=F32)
            s_ref[slot, :, n_ctx:] = (lax.dot_general(qm, k2, nt_dims, preferred_element_type=F32)
                                      + bias_ref[0, h])
            inv = []
            for c in range(nq // NA_SOFTMAX_ROWS):
                rows = slice(c * NA_SOFTMAX_ROWS, (c + 1) * NA_SOFTMAX_ROWS)
                s = s_ref[slot, rows, :]
                p = jnp.exp(s - jnp.max(s, axis=-1, keepdims=True))
                inv.append(1.0 / jnp.sum(p, axis=-1, keepdims=True))
                p_ref[slot, rows, :] = p.astype(BF16)
            o = jnp.dot(p_ref[slot, :, :n_ctx], vc2, preferred_element_type=F32)
            o = o + jnp.dot(p_ref[slot, :, n_ctx:], v2, preferred_element_type=F32)
            outs.append(o * jnp.concatenate(inv, axis=0))
        o2 = outs[0]
        for hh in range(1, len(outs)):
            o2 = jnp.where(lane >= hh * NA_HEAD_DIM, outs[hh], o2)
        o_ref[0, :, sl] = o2.astype(o_ref.dtype)


def na_attention(qkv, rpb, n_lat):
    bsz, nt, _ = qkv.shape
    w = GRID_W
    rows = n_lat // w
    n_ctx = nt - n_lat
    nq = NA_Q_ROWS * w
    band = NA_BAND_ROWS * w
    n_all = n_ctx + band
    assert n_ctx % LANES == 0 and band % LANES == 0 and rows % NA_Q_ROWS == 0 and rows >= NA_BAND_ROWS + NA_WIN_ROWS // 2

    def band_start(g):
        return _na_band_start(g * NA_Q_ROWS, rows)

    def band_spec(j):
        return pl.BlockSpec((pl.Element(1), pl.Element(band), pl.Element(NA_WIDTH)),
                            lambda b, g: (b, band_start(g) * w, j * NA_WIDTH))

    def ctx_spec(j):
        return pl.BlockSpec((1, n_ctx, NA_WIDTH), lambda b, g: (b, n_lat // n_ctx, j))

    return pl.pallas_call(
        _na_kernel,
        grid=(bsz, rows // NA_Q_ROWS),
        in_specs=[pl.BlockSpec((1, nq, NA_WIDTH), lambda b, g: (b, g, 0)),
                  band_spec(1), band_spec(2), ctx_spec(1), ctx_spec(2),
                  pl.BlockSpec((1, NA_HEADS, nq, band), lambda b, g: (g - band_start(g) // NA_Q_ROWS, 0, 0, 0))],
        out_specs=pl.BlockSpec((1, nq, NA_WIDTH), lambda b, g: (b, g, 0)),
        out_shape=jax.ShapeDtypeStruct((bsz, n_lat, NA_WIDTH), BF16),
        scratch_shapes=[pltpu.VMEM((NA_SCORE_SLOTS, nq, n_all), F32), pltpu.VMEM((NA_SCORE_SLOTS, nq, n_all), BF16)],
        compiler_params=_params(("arbitrary", "arbitrary")),
        name="na_attention",
    )(qkv, qkv, qkv, qkv, qkv, _na_bias(rpb, rows))


def _proj_res_kernel(a_ref, x_ref, mod_ref, w_ref, o_ref):
    m = jnp.dot(a_ref[0], w_ref[...], preferred_element_type=F32)
    o_ref[0] = x_ref[0] + mod_ref[0, 2:3, :] * m


def proj_residual(a, x, mod, w):
    bsz, n, k = a.shape
    d = x.shape[-1]
    tm = TOKEN_TILE
    return pl.pallas_call(
        _proj_res_kernel,
        grid=(bsz, n // tm),
        in_specs=[pl.BlockSpec((1, tm, k), lambda b, j: (b, j, 0)),
                  pl.BlockSpec((1, tm, d), lambda b, j: (b, j, 0)),
                  pl.BlockSpec((1, SUBLANES, d), lambda b, j: (b, 0, 0)),
                  pl.BlockSpec((k, d), lambda b, j: (0, 0))],
        out_specs=pl.BlockSpec((1, tm, d), lambda b, j: (b, j, 0)),
        out_shape=jax.ShapeDtypeStruct((bsz, n, d), F32),
        compiler_params=_params(("arbitrary", "arbitrary")),
        name="proj_residual",
    )(a, x, mod, w.astype(BF16))


def _router_kernel(x_ref, mod_ref, nw_ref, wr_ref, br_ref, tri_ref, h_ref, idx_ref, gate_ref, rank_ref, cnt_ref,
                   run_ref):
    first_step = (pl.program_id(0) == 0) & (pl.program_id(1) == 0)

    @pl.when(first_step)
    def _():
        run_ref[...] = jnp.zeros_like(run_ref)

    h = _norm_mod(x_ref[0], nw_ref[...], mod_ref[0, 3:4, :], mod_ref[0, 4:5, :])
    h_ref[0] = h.astype(h_ref.dtype)
    logits = jnp.dot(h.astype(BF16), wr_ref[...], preferred_element_type=F32) + br_ref[...]
    lane = lax.broadcasted_iota(jnp.int32, logits.shape, 1)
    lane_f = lane.astype(F32)
    vals, idxs = [], []
    cur = logits
    for _ in range(MOE_TOP_K):
        m = jnp.max(cur, axis=-1, keepdims=True)
        first = jnp.min(jnp.where(cur == m, lane_f, float(LANES)), axis=-1, keepdims=True)
        vals.append(m)
        idxs.append(first)
        cur = jnp.where(lane_f == first, MASK_VALUE, cur)
    es = [jnp.exp(v - vals[0]) for v in vals]
    tot = es[0]
    for e in es[1:]:
        tot = tot + e
    onehots = [(lane_f == idxs[k]).astype(F32) for k in range(MOE_TOP_K)]
    multi = onehots[0]
    for oh in onehots[1:]:
        multi = multi + oh
    before = jnp.dot(tri_ref[...], multi.astype(BF16), preferred_element_type=F32) + run_ref[0:1, :]
    idx_out = jnp.zeros(logits.shape, F32)
    gate_out = jnp.zeros(logits.shape, F32)
    rank_out = jnp.zeros(logits.shape, F32)
    for k in range(MOE_TOP_K):
        idx_out = jnp.where(lane == k, idxs[k], idx_out)
        gate_out = jnp.where(lane == k, es[k] / tot, gate_out)
        rank_out = jnp.where(lane == k, jnp.sum(before * onehots[k], axis=-1, keepdims=True), rank_out)
    idx_ref[0] = idx_out.astype(jnp.int32)
    gate_ref[0] = gate_out
    rank_ref[0] = rank_out.astype(jnp.int32)
    run_ref[0:1, :] = run_ref[0:1, :] + jnp.sum(multi, axis=0, keepdims=True)
    cnt_ref[...] = jnp.broadcast_to(run_ref[0:1, :], cnt_ref.shape)


def router(xc, mod, norm_w, w_router, b_router, n_lat):
    bsz, nt, d = xc.shape
    tm = TOKEN_TILE
    wr = jnp.pad(w_router, ((0, 0), (0, LANES - N_EXPERTS))).astype(BF16)
    br = jnp.pad(b_router.astype(F32), (0, LANES - N_EXPERTS), constant_values=MASK_VALUE).reshape(1, LANES)
    tri = jnp.tril(jnp.ones((tm, tm), BF16), k=-1)
    tok = lambda n, dt: jax.ShapeDtypeStruct((bsz, nt, n), dt)
    out = lambda n: pl.BlockSpec((1, tm, n), lambda b, j: (b, j, 0))
    return pl.pallas_call(
        _router_kernel,
        grid=(bsz, nt // tm),
        in_specs=[pl.BlockSpec((1, tm, d), lambda b, j: (b, j, 0)),
                  pl.BlockSpec((1, SUBLANES, d), _mod_index(n_lat // tm, bsz)),
                  pl.BlockSpec((1, d), lambda b, j: (0, 0)),
                  pl.BlockSpec((d, LANES), lambda b, j: (0, 0)),
                  pl.BlockSpec((1, LANES), lambda b, j: (0, 0)),
                  pl.BlockSpec((tm, tm), lambda b, j: (0, 0))],
        out_specs=[out(d), out(LANES), out(LANES), out(LANES),
                   pl.BlockSpec((SUBLANES, LANES), lambda b, j: (0, 0))],
        out_shape=[tok(d, F32), tok(LANES, jnp.int32), tok(LANES, F32), tok(LANES, jnp.int32),
                   jax.ShapeDtypeStruct((SUBLANES, LANES), F32)],
        scratch_shapes=[pltpu.VMEM((SUBLANES, LANES), F32)],
        compiler_params=_params(("arbitrary", "arbitrary")),
        name="router",
    )(xc, mod, norm_w.reshape(1, d), wr, br, tri)


def _moe_kernel(be_ref, nu_ref, x_ref, w1_ref, b1_ref, w2_ref, b2_ref, o_ref, w1b_ref, w2b_ref):
    i = pl.program_id(0)
    ff = w2_ref.shape[2]
    used = i < nu_ref[0]

    @pl.when(used & ((i == 0) | (be_ref[i] != be_ref[jnp.maximum(i - 1, 0)])))
    def _():
        w1b_ref[...] = w1_ref[0, 0].astype(BF16)
        w2b_ref[...] = w2_ref[0, 0].astype(BF16)

    @pl.when(used)
    def _():
        t = jnp.dot(x_ref[...].astype(BF16), w1b_ref[...], preferred_element_type=F32) + b1_ref[0, 0]
        x_glu = jnp.minimum(t[:, :ff], SWIGLU_LIMIT)
        x_lin = jnp.clip(t[:, ff:], -SWIGLU_LIMIT, SWIGLU_LIMIT)
        act = x_glu * jax.nn.sigmoid(SWIGLU_ALPHA * x_glu) * (x_lin + 1)
        y = jnp.dot(act.astype(BF16), w2b_ref[...], preferred_element_type=F32) + b2_ref[0, 0]
        o_ref[...] = y.astype(o_ref.dtype)

    @pl.when(jnp.logical_not(used))
    def _():
        o_ref[...] = jnp.zeros_like(o_ref)


def moe_experts(xs, block_e, n_used, w1, b1, w2, b2, layer):
    n_slots, d = xs.shape
    depth, ne, _, ff2 = w1.shape
    ff = w2.shape[2]
    tm = MOE_TILE
    nb = n_slots // tm
    grid_spec = pltpu.PrefetchScalarGridSpec(
        num_scalar_prefetch=2,
        grid=(nb,),
        in_specs=[pl.BlockSpec((tm, d), lambda i, be, nu: (jnp.where(i < nu[0], i, 0), 0)),
                  pl.BlockSpec((1, 1, d, ff2), lambda i, be, nu: (layer, be[i], 0, 0)),
                  pl.BlockSpec((1, 1, 1, ff2), lambda i, be, nu: (layer, be[i], 0, 0)),
                  pl.BlockSpec((1, 1, ff, d), lambda i, be, nu: (layer, be[i], 0, 0)),
                  pl.BlockSpec((1, 1, 1, d), lambda i, be, nu: (layer, be[i], 0, 0))],
        out_specs=pl.BlockSpec((tm, d), lambda i, be, nu: (i, 0)),
        scratch_shapes=[pltpu.VMEM((d, ff2), BF16), pltpu.VMEM((ff, d), BF16)],
    )
    return pl.pallas_call(
        _moe_kernel,
        grid_spec=grid_spec,
        out_shape=jax.ShapeDtypeStruct((n_slots, d), F32),
        compiler_params=_params(("arbitrary",)),
        name="moe_experts",
    )(block_e, n_used, xs, w1, b1.reshape(depth, ne, 1, ff2), w2, b2.reshape(depth, ne, 1, d))


def _moe_plan(idx, rank, counts, n_tok):
    counts = counts.astype(jnp.int32)
    padded = (counts + MOE_TILE - 1) // MOE_TILE * MOE_TILE
    padded_ends = jnp.cumsum(padded)
    padded_starts = padded_ends - padded
    onehot = idx[..., None] == jnp.arange(N_EXPERTS, dtype=jnp.int32)
    slot_of = rank + jnp.sum(jnp.where(onehot, padded_starts, 0), axis=-1)
    n_blocks = -(-n_tok * MOE_TOP_K // MOE_TILE) + N_EXPERTS
    block_row = jnp.arange(n_blocks, dtype=jnp.int32) * MOE_TILE
    block_e = jnp.minimum(jnp.sum(padded_ends[None, :] <= block_row[:, None], axis=1), N_EXPERTS - 1)
    n_used = padded_ends[-1:] // MOE_TILE
    last_block = jnp.concatenate([jnp.where(counts > 0, padded_ends - MOE_TILE, -1), n_used])
    slot_tiles = slot_of.astype(jnp.int32).reshape(n_tok // TOKEN_TILE, 1, TOKEN_TILE * MOE_TOP_K)
    return slot_tiles, block_e.astype(jnp.int32), n_used.astype(jnp.int32), last_block.astype(jnp.int32), n_blocks


def _dispatch_kernel(slot_ref, last_ref, h_ref, xs_ref, zero_ref, sem):
    tm = h_ref.shape[0]

    @pl.when(pl.program_id(0) == 0)
    def _():
        zero_ref[...] = jnp.zeros_like(zero_ref)

        def zero_copy(e):
            row = pl.multiple_of(jnp.maximum(last_ref[e], 0), MOE_TILE)
            return pltpu.make_async_copy(zero_ref, xs_ref.at[pl.ds(row, MOE_TILE)], sem)

        def block_copy(i):
            return pltpu.make_async_copy(zero_ref, xs_ref.at[pl.ds(pl.multiple_of(i * MOE_TILE, MOE_TILE), MOE_TILE)], sem)

        n_used = last_ref[N_EXPERTS]
        n_blocks = xs_ref.shape[0] // MOE_TILE
        lax.fori_loop(n_used, n_blocks, lambda i, c: (block_copy(i).start(), c)[1], 0)
        for e in range(N_EXPERTS):
            pl.when(last_ref[e] >= 0)(lambda e=e: zero_copy(e).start())
        for e in range(N_EXPERTS):
            pl.when(last_ref[e] >= 0)(lambda e=e: zero_copy(e).wait())
        lax.fori_loop(n_used, n_blocks, lambda i, c: (block_copy(i).wait(), c)[1], 0)

    def issue(t, carry):
        for k in range(MOE_TOP_K):
            s = slot_ref[0, 0, t * MOE_TOP_K + k]
            pltpu.make_async_copy(h_ref.at[pl.ds(t, 1)], xs_ref.at[pl.ds(s, 1)], sem).start(priority=k % 2)
        return carry

    lax.fori_loop(0, tm, issue, 0)
    for k in range(MOE_TOP_K):
        pltpu.make_async_copy(h_ref, xs_ref.at[pl.ds(0, tm)], sem).wait()


def moe_dispatch(h, slot_tiles, last_block, n_blocks):
    n_tok, d = h.shape
    tm = TOKEN_TILE
    return pl.pallas_call(
        _dispatch_kernel,
        grid=(n_tok // tm,),
        in_specs=[pl.BlockSpec((1, 1, tm * MOE_TOP_K), lambda i: (i, 0, 0), memory_space=pltpu.SMEM),
                  pl.BlockSpec(memory_space=pltpu.SMEM),
                  pl.BlockSpec((tm, d), lambda i: (i, 0))],
        out_specs=pl.BlockSpec(memory_space=pl.ANY),
        out_shape=jax.ShapeDtypeStruct((n_blocks * MOE_TILE, d), F32),
        scratch_shapes=[pltpu.VMEM((MOE_TILE, d), F32), pltpu.SemaphoreType.DMA(())],
        compiler_params=_params(("arbitrary",)),
        name="moe_dispatch",
    )(slot_tiles, last_block, h)


def _combine_kernel(slot_ref, x_ref, gate_ref, mod_ref, *rest, final):
    ys_ref, o_ref, ybuf_ref, sem = rest[-4:]
    tm = x_ref.shape[1]

    def issue(t, carry):
        for k in range(MOE_TOP_K):
            s = slot_ref[0, 0, t * MOE_TOP_K + k]
            pltpu.make_async_copy(ys_ref.at[pl.ds(s, 1)], ybuf_ref.at[k, pl.ds(t, 1)], sem).start(priority=k % 2)
        return carry

    lax.fori_loop(0, tm, issue, 0)
    for k in range(MOE_TOP_K):
        pltpu.make_async_copy(ys_ref.at[pl.ds(0, tm)], ybuf_ref.at[k], sem).wait()

    gate = gate_ref[0]
    y = None
    for k in range(MOE_TOP_K):
        term = gate[:, k:k + 1] * ybuf_ref[k]
        y = term if y is None else y + term
    x = x_ref[0] + mod_ref[0, 5:6, :] * y
    if final:
        fw_ref = rest[0]
        x = x * lax.rsqrt(jnp.mean(x * x, axis=-1, keepdims=True) + RMS_EPS) * fw_ref[...]
    o_ref[0] = x


def moe_combine(xc, ys, slot_tiles, gates, mod, n_lat, final_w=None):
    bsz, nt, d = xc.shape
    tm = TOKEN_TILE
    tiles = nt // tm
    in_specs = [pl.BlockSpec((1, 1, tm * MOE_TOP_K), lambda b, j: (b * tiles + j, 0, 0), memory_space=pltpu.SMEM),
                pl.BlockSpec((1, tm, d), lambda b, j: (b, j, 0)),
                pl.BlockSpec((1, tm, LANES), lambda b, j: (b, j, 0)),
                pl.BlockSpec((1, SUBLANES, d), _mod_index(n_lat // tm, bsz))]
    args = [slot_tiles, xc, gates, mod]
    if final_w is not None:
        in_specs.append(pl.BlockSpec((1, d), lambda b, j: (0, 0)))
        args.append(final_w.reshape(1, d))
    in_specs.append(pl.BlockSpec(memory_space=pl.ANY))
    args.append(ys)
    return pl.pallas_call(
        functools.partial(_combine_kernel, final=final_w is not None),
        grid=(bsz, tiles),
        in_specs=in_specs,
        out_specs=pl.BlockSpec((1, tm, d), lambda b, j: (b, j, 0)),
        out_shape=jax.ShapeDtypeStruct((bsz, nt, d), F32),
        scratch_shapes=[pltpu.VMEM((MOE_TOP_K, tm, d), F32), pltpu.SemaphoreType.DMA(())],
        compiler_params=_params(("arbitrary", "arbitrary")),
        name="moe_combine",
    )(*args)


def moe_layer(xc, mod, norm_w, w_router, b_router, experts, n_lat, final_w=None):
    bsz, nt, d = xc.shape
    n_tok = bsz * nt
    h, idx, gates, rank, counts = router(xc, mod, norm_w, w_router, b_router, n_lat)
    top = lambda a: a.reshape(n_tok, LANES)[:, :MOE_TOP_K]
    slot_tiles, block_e, n_used, last_block, n_blocks = _moe_plan(top(idx), top(rank), counts[0, :N_EXPERTS], n_tok)
    xs = moe_dispatch(h.reshape(n_tok, d), slot_tiles, last_block, n_blocks)
    ys = moe_experts(xs, block_e, n_used, *experts)
    return moe_combine(xc, ys, slot_tiles, gates, mod, n_lat, final_w)


def kernel(x, c, ctx, c_ctx, ada_w, ada_b, norm_w, final_norm_w, ev_w_in, ev_w_out, s5_lam_re, s5_lam_im, s5_log_step, s5_b_re, s5_b_im, s5_c_re, s5_c_im, s5_d, s5_w_glu, ret_log_decay, na_w_qkv, na_w_o, na_rpb, moe_w_router, moe_b_router, moe_w1, moe_b1, moe_w2, moe_b2):
    bsz, n_lat, d = x.shape
    n_ctx = ctx.shape[1]
    depth = ada_w.shape[0]
    mod = adaln_table(c, c_ctx, ada_w, ada_b)
    xc = jnp.concatenate([x, ctx], axis=1)
    for i in range(depth):
        last = i == depth - 1
        j = i // 2
        if i % 2 == 0:
            proj = norm_proj(xc, mod[i], norm_w[i, 0], ev_w_in[j], n_lat, 0)
            s5p = _s5_layout(s5_lam_re[j], s5_lam_im[j], s5_log_step[j], s5_b_re[j], s5_b_im[j],
                             s5_c_re[j], s5_c_im[j])
            y_s5 = s5_scan(proj, s5p, n_lat)
            o_ret = retention(proj, ret_log_decay[j], n_lat)
            xc = mix_out(xc, proj, y_s5, o_ret, mod[i], s5_d[j], s5_w_glu[j], ev_w_out[j], n_lat)
        else:
            qkv = norm_proj(xc, mod[i], norm_w[i, 0], na_w_qkv[j], n_lat, 0)
            att = na_attention(qkv, na_rpb[j], n_lat)
            assert last, "an odd layer is only supported as the final layer (no context output needed)"
            xc = proj_residual(att, xc, mod[i], na_w_o[j])
        experts = (moe_w1, moe_b1, moe_w2, moe_b2, i)
        if last:
            return moe_layer(xc[:, :n_lat], mod[i], norm_w[i, 1], moe_w_router[i], moe_b_router[i],
                             experts, n_lat, final_norm_w)
        xc = moe_layer(xc, mod[i], norm_w[i, 1], moe_w_router[i], moe_b_router[i], experts, n_lat)
```

```python
import functools
import math

import numpy as np
import jax
import jax.numpy as jnp
from jax import lax
from jax.experimental import pallas as pl
from jax.experimental.pallas import tpu as pltpu

F32 = jnp.float32
BF16 = jnp.bfloat16

GRID_W = 64
RMS_EPS = 1e-6
S5_WIDTH = 512
S5_GROUP = 16
S5_GROUPS = S5_WIDTH // S5_GROUP
S5_STATE = 64
RET_HEADS = 4
RET_HEAD_DIM = 128
RET_WIDTH = RET_HEADS * RET_HEAD_DIM
ROPE_BASE = 10000.0
NA_HEADS = 16
NA_HEAD_DIM = 64
NA_WIDTH = NA_HEADS * NA_HEAD_DIM
NA_WIN_ROWS = 8
NA_WIN_COLS = 16
N_EXPERTS = 32
MOE_TOP_K = 4
SWIGLU_LIMIT = 7.0
SWIGLU_ALPHA = 1.702

LANES = 128
SUBLANES = 8
MXU_DIM = 256
V7X_VMEM_BYTES = 64 * 1024 * 1024
VMEM_LIMIT = V7X_VMEM_BYTES * 7 // 8

TOKEN_TILE = 256
S5_CHUNK = 256
RET_CHUNK = 256
MOE_TILE = 512
NA_Q_ROWS = 2
NA_BAND_ROWS = 10
NA_SOFTMAX_ROWS = 16
NA_SCORE_SLOTS = 4
assert NA_BAND_ROWS >= NA_WIN_ROWS + NA_Q_ROWS - 1 and (NA_BAND_ROWS - NA_Q_ROWS) % NA_Q_ROWS == 0
MASK_VALUE = -1e30

S5_SLICES = 8
assert S5_SLICES * MXU_DIM == S5_GROUPS * S5_STATE


def _params(sem):
    return pltpu.CompilerParams(dimension_semantics=sem, vmem_limit_bytes=VMEM_LIMIT)


def _adaln_kernel(c_ref, w_ref, b_ref, o_ref):
    c = c_ref[...]
    s = c * jax.nn.sigmoid(c)
    o_ref[0] = jnp.dot(s, w_ref[0], preferred_element_type=F32,
                       precision=lax.Precision.HIGHEST) + b_ref[0]


def adaln_table(c, c_ctx, ada_w, ada_b):
    depth, d, d6 = ada_w.shape
    bsz = c.shape[0]
    cond = jnp.concatenate([c, c_ctx[None, :]], axis=0)
    cond = jnp.pad(cond, ((0, SUBLANES - (bsz + 1)), (0, 0)))
    tn = d6 // 4
    out = pl.pallas_call(
        _adaln_kernel,
        grid=(depth, d6 // tn),
        in_specs=[pl.BlockSpec((SUBLANES, d), lambda i, j: (0, 0)),
                  pl.BlockSpec((1, d, tn), lambda i, j: (i, 0, j)),
                  pl.BlockSpec((1, 1, tn), lambda i, j: (i, 0, j))],
        out_specs=pl.BlockSpec((1, SUBLANES, tn), lambda i, j: (i, 0, j)),
        out_shape=jax.ShapeDtypeStruct((depth, SUBLANES, d6), F32),
        compiler_params=_params(("arbitrary", "arbitrary")),
        name="adaln",
    )(cond, ada_w, ada_b.reshape(depth, 1, d6))
    tab = out[:, :bsz + 1].reshape(depth, bsz + 1, 6, d)
    return jnp.pad(tab, ((0, 0), (0, 0), (0, 2), (0, 0)))


def _norm_mod(x, nw, shift, scale):
    y = x * lax.rsqrt(jnp.mean(x * x, axis=-1, keepdims=True) + RMS_EPS)
    return (y * nw) * (1 + scale) + shift


def _mod_index(n_lat_tiles, bsz):
    def index(b, j):
        return (jnp.where(j >= n_lat_tiles, bsz, b), 0, 0)
    return index


def _proj_kernel(x_ref, mod_ref, nw_ref, w_ref, o_ref, *, shift_row):
    h = _norm_mod(x_ref[0], nw_ref[...], mod_ref[0, shift_row:shift_row + 1, :],
                  mod_ref[0, shift_row + 1:shift_row + 2, :])
    o_ref[0] = jnp.dot(h.astype(BF16), w_ref[...], preferred_element_type=F32).astype(o_ref.dtype)


def norm_proj(xc, mod, norm_w, w, n_lat, shift_row):
    bsz, nt, d = xc.shape
    n = w.shape[1]
    tm = TOKEN_TILE
    return pl.pallas_call(
        functools.partial(_proj_kernel, shift_row=shift_row),
        grid=(bsz, nt // tm),
        in_specs=[pl.BlockSpec((1, tm, d), lambda b, j: (b, j, 0)),
                  pl.BlockSpec((1, SUBLANES, d), _mod_index(n_lat // tm, bsz)),
                  pl.BlockSpec((1, d), lambda b, j: (0, 0)),
                  pl.BlockSpec((d, n), lambda b, j: (0, 0))],
        out_specs=pl.BlockSpec((1, tm, n), lambda b, j: (b, j, 0)),
        out_shape=jax.ShapeDtypeStruct((bsz, nt, n), BF16),
        compiler_params=_params(("arbitrary", "arbitrary")),
        name="norm_proj",
    )(xc, mod, norm_w.reshape(1, d), w.astype(BF16))


def _s5_discretize(lam_re, lam_im, log_step, b_re, b_im):
    lam_re = jnp.minimum(lam_re.astype(F32), -1e-4)
    lam_im = lam_im.astype(F32)
    step = jnp.exp(log_step.astype(F32))[..., None]
    mag = jnp.exp(lam_re * step)
    ang = lam_im * step
    a_re, a_im = mag * jnp.cos(ang), mag * jnp.sin(ang)
    den = lam_re * lam_re + lam_im * lam_im
    n_re, n_im = a_re - 1.0, a_im
    co_re = (n_re * lam_re + n_im * lam_im) / den
    co_im = (n_im * lam_re - n_re * lam_im) / den
    b_re, b_im = b_re.astype(F32), b_im.astype(F32)
    bb_re = co_re[..., None] * b_re - co_im[..., None] * b_im
    bb_im = co_re[..., None] * b_im + co_im[..., None] * b_re
    return a_re, a_im, bb_re, bb_im


def _s5_layout(lam_re, lam_im, log_step, b_re, b_im, c_re, c_im):
    a_re, a_im, bb_re, bb_im = _s5_discretize(lam_re, lam_im, log_step, b_re, b_im)
    gh = S5_GROUPS // 2
    nq = S5_STATE // S5_GROUP
    eye = jnp.eye(gh, dtype=F32)

    def arrange_a(a):
        a = a.reshape(2, 2, gh, nq, S5_GROUP)
        return a.transpose(0, 1, 3, 2, 4).reshape(2, S5_SLICES, MXU_DIM)

    def arrange_b(bb):
        bb = bb.reshape(2, 2, gh, nq, S5_GROUP, S5_GROUP)
        m = jnp.einsum('dhgqni,Gg->dhqGign', bb, eye)
        return m.reshape(2, S5_SLICES, MXU_DIM, MXU_DIM).astype(BF16)

    def arrange_c(c):
        c = c.astype(F32).reshape(2, 2, gh, S5_GROUP, nq, S5_GROUP)
        m = jnp.einsum('dhgiqn,gG->dhqgnGi', c, eye)
        return m.reshape(2, S5_SLICES, MXU_DIM, MXU_DIM).astype(BF16)

    return (arrange_a(a_re), arrange_a(a_im), arrange_b(bb_re), arrange_b(bb_im),
            arrange_c(c_re), arrange_c(-c_im.astype(F32)))


def _s5_kernel(u_ref, bre_ref, bim_ref, cre_ref, cim_ref, are_ref, aim_ref, y_ref, bu_ref, st_ref, h_ref, *, nb):
    t_chunk = S5_CHUNK
    half = MXU_DIM // LANES
    d = pl.program_id(0)

    @pl.when(pl.program_id(1) == 0)
    def _():
        h_ref[...] = jnp.zeros_like(h_ref)

    def slab(b, c, lh):
        return (b * 2 + c) * half + lh

    for b in range(nb):
        for h in range(2):
            ub = u_ref[b, :, h * MXU_DIM:(h + 1) * MXU_DIM]
            for q in range(S5_SLICES // 2):
                k = h * (S5_SLICES // 2) + q
                for c, w_ref in ((0, bre_ref), (1, bim_ref)):
                    r = jnp.dot(ub, w_ref[0, k], preferred_element_type=F32)
                    for lh in range(half):
                        bu_ref[slab(b, c, lh), pl.ds(k, t_chunk, stride=S5_SLICES), :] = r[:, lh * LANES:(lh + 1) * LANES]

    ar = [are_ref[0, :, lh * LANES:(lh + 1) * LANES] for lh in range(half)]
    ai = [aim_ref[0, :, lh * LANES:(lh + 1) * LANES] for lh in range(half)]

    def step(t, carry):
        tt = jnp.where(d == 0, t, t_chunk - 1 - t)
        new = []
        for b in range(nb):
            for lh in range(half):
                hr, hi = carry[2 * (b * half + lh)], carry[2 * (b * half + lh) + 1]
                rows = pl.ds(pl.multiple_of(tt * S5_SLICES, S5_SLICES), S5_SLICES)
                xr = bu_ref[slab(b, 0, lh), rows, :]
                xi = bu_ref[slab(b, 1, lh), rows, :]
                nr = ar[lh] * hr - ai[lh] * hi + xr
                ni = ar[lh] * hi + ai[lh] * hr + xi
                st_ref[slab(b, 0, lh), rows, :] = nr
                st_ref[slab(b, 1, lh), rows, :] = ni
                new += [nr, ni]
        return tuple(new)

    n_state = nb * half * 2
    carry = lax.fori_loop(0, t_chunk, step, tuple(h_ref[j] for j in range(n_state)), unroll=8)
    for j in range(n_state):
        h_ref[j] = carry[j]

    for b in range(nb):
        for h in range(2):
            acc = None
            for q in range(S5_SLICES // 2):
                k = h * (S5_SLICES // 2) + q
                for c, w_ref in ((0, cre_ref), (1, cim_ref)):
                    s = jnp.concatenate(
                        [st_ref[slab(b, c, lh), pl.ds(k, t_chunk, stride=S5_SLICES), :] for lh in range(half)], axis=1)
                    term = jnp.dot(s.astype(BF16), w_ref[0, k], preferred_element_type=F32)
                    acc = term if acc is None else acc + term
            y_ref[0, b, :, h * MXU_DIM:(h + 1) * MXU_DIM] = acc


def _scan_chunk_index(n_chunks, n_ctx_chunks):
    def chunk(d, i):
        fwd = lax.rem(i + (n_chunks - n_ctx_chunks), n_chunks)
        return jnp.where(d == 0, fwd, n_chunks - 1 - i)
    return chunk


def s5_scan(proj, s5p, n_lat):
    bsz, nt, _ = proj.shape
    a_re, a_im, bb_re, bb_im, cc_re, cc_im = s5p
    t = S5_CHUNK
    nc = nt // t
    chunk = _scan_chunk_index(nc, (nt - n_lat) // t)
    wspec = pl.BlockSpec((1, S5_SLICES, MXU_DIM, MXU_DIM), lambda d, i: (d, 0, 0, 0))
    aspec = pl.BlockSpec((1, S5_SLICES, MXU_DIM), lambda d, i: (d, 0, 0))
    n_slab = bsz * 2 * (MXU_DIM // LANES)
    return pl.pallas_call(
        functools.partial(_s5_kernel, nb=bsz),
        grid=(2, nc),
        in_specs=[pl.BlockSpec((bsz, t, S5_WIDTH), lambda d, i: (0, chunk(d, i), 0)),
                  wspec, wspec, wspec, wspec, aspec, aspec],
        out_specs=pl.BlockSpec((1, bsz, t, S5_WIDTH), lambda d, i: (d, 0, chunk(d, i), 0)),
        out_shape=jax.ShapeDtypeStruct((2, bsz, nt, S5_WIDTH), F32),
        scratch_shapes=[pltpu.VMEM((n_slab, S5_SLICES * S5_CHUNK, LANES), F32),
                        pltpu.VMEM((n_slab, S5_SLICES * S5_CHUNK, LANES), F32),
                        pltpu.VMEM((n_slab, S5_SLICES, LANES), F32)],
        compiler_params=_params(("arbitrary", "arbitrary")),
        name="s5_scan",
    )(proj, bb_re, bb_im, cc_re, cc_im, a_re, a_im)


def _rope_tables(n_lat, n_ctx):
    half = RET_HEAD_DIM // 4
    freq = ROPE_BASE ** (-jnp.arange(half, dtype=F32) / half)
    t = jnp.arange(n_lat)
    row = (t // GRID_W).astype(F32)
    col = (t % GRID_W).astype(F32)
    ang_r = row[:, None] * freq[None, :]
    ang_c = col[:, None] * freq[None, :]
    cos = jnp.concatenate([jnp.cos(ang_r)] * 2 + [jnp.cos(ang_c)] * 2, axis=-1)
    sin = jnp.concatenate([jnp.sin(ang_r)] * 2 + [jnp.sin(ang_c)] * 2, axis=-1)
    cos = jnp.concatenate([cos, jnp.ones((n_ctx, RET_HEAD_DIM), F32)], axis=0)
    sin = jnp.concatenate([sin, jnp.zeros((n_ctx, RET_HEAD_DIM), F32)], axis=0)
    return cos, sin


def _rope_rotation_matrix():
    blk = RET_HEAD_DIM // 2
    half = blk // 2
    r = np.zeros((RET_HEAD_DIM, RET_HEAD_DIM), np.float32)
    for base in (0, blk):
        for l in range(half):
            r[base + l + half, base + l] = -1.0
            r[base + l, base + l + half] = 1.0
    return jnp.asarray(r, BF16)


def _ret_decay(log_decay, t):
    lg = log_decay.astype(F32)
    scale = RET_HEAD_DIM ** -0.5
    idx = jnp.arange(t, dtype=F32)
    diff = idx[:, None] - idx[None, :]
    diff = jnp.stack([diff, -diff])[:, None]
    inner = jnp.where(diff >= 0, jnp.exp(lg[:, :, None, None] * jnp.maximum(diff, 0.0)), 0.0) * scale
    pos = jnp.stack([idx, t - 1.0 - idx])
    q_dec = jnp.exp(lg[:, :, None] * (pos[:, None, :] + 1.0))
    k_dec = jnp.exp(lg[:, :, None] * (t - 1.0 - pos[:, None, :])) * scale
    blk = jnp.exp(lg * t)
    bcast = lambda v: jnp.broadcast_to(v[..., None], v.shape + (RET_HEAD_DIM,))
    return inner, bcast(q_dec), bcast(k_dec), jnp.broadcast_to(blk[:, :, None, None], (2, RET_HEADS, SUBLANES, RET_HEAD_DIM))


def _ret_kernel(q_ref, k_ref, v_ref, cos_ref, sin_ref, rot_ref, inner_ref, qd_ref, kd_ref, bd_ref, o_ref, s_ref):
    @pl.when(pl.program_id(2) == 0)
    def _():
        s_ref[...] = jnp.zeros_like(s_ref)

    cos = cos_ref[...]
    sin = sin_ref[...]
    rot = rot_ref[...]
    nt_dims = (((1,), (1,)), ((), ()))
    for hd in range(RET_HEADS):
        sl = slice(hd * RET_HEAD_DIM, (hd + 1) * RET_HEAD_DIM)
        q = q_ref[0, :, sl]
        k = k_ref[0, :, sl]
        v = v_ref[0, :, sl]
        qr = q.astype(F32) * cos + jnp.dot(q, rot, preferred_element_type=F32) * sin
        kr = k.astype(F32) * cos + jnp.dot(k, rot, preferred_element_type=F32) * sin
        qb = qr.astype(BF16)
        att = lax.dot_general(qb, kr.astype(BF16), nt_dims, preferred_element_type=F32) * inner_ref[0, hd]
        s = s_ref[hd]
        o = jnp.dot(att.astype(BF16), v, preferred_element_type=F32)
        o = o + jnp.dot(qb, s.astype(BF16), preferred_element_type=F32) * qd_ref[0, hd]
        kd_t = (kr * kd_ref[0, hd]).T.astype(BF16)
        s_ref[hd] = bd_ref[0, hd, 0:1, :] * s + jnp.dot(kd_t, v, preferred_element_type=F32)
        o_ref[0, 0, :, sl] = o


def retention(proj, log_decay, n_lat):
    bsz, nt, _ = proj.shape
    t = RET_CHUNK
    nc = nt // t
    chunk = _scan_chunk_index(nc, (nt - n_lat) // t)
    cos, sin = _rope_tables(n_lat, nt - n_lat)
    inner, q_dec, k_dec, blk = _ret_decay(log_decay, t)
    w = RET_WIDTH

    def col(j):
        return pl.BlockSpec((1, t, w), lambda d, b, i: (b, chunk(d, i), j))

    tab = pl.BlockSpec((t, RET_HEAD_DIM), lambda d, b, i: (chunk(d, i), 0))
    dec = pl.BlockSpec((1, RET_HEADS, t, RET_HEAD_DIM), lambda d, b, i: (d, 0, 0, 0))
    return pl.pallas_call(
        _ret_kernel,
        grid=(2, bsz, nc),
        in_specs=[col(1), col(2), col(3), tab, tab,
                  pl.BlockSpec((RET_HEAD_DIM, RET_HEAD_DIM), lambda d, b, i: (0, 0)),
                  pl.BlockSpec((1, RET_HEADS, t, t), lambda d, b, i: (d, 0, 0, 0)),
                  dec, dec,
                  pl.BlockSpec((1, RET_HEADS, SUBLANES, RET_HEAD_DIM), lambda d, b, i: (d, 0, 0, 0))],
        out_specs=pl.BlockSpec((1, 1, t, w), lambda d, b, i: (d, b, chunk(d, i), 0)),
        out_shape=jax.ShapeDtypeStruct((2, bsz, nt, w), F32),
        scratch_shapes=[pltpu.VMEM((RET_HEADS, RET_HEAD_DIM, RET_HEAD_DIM), F32)],
        compiler_params=_params(("arbitrary", "arbitrary", "arbitrary")),
        name="retention",
    )(proj, proj, proj, cos, sin, _rope_rotation_matrix(), inner, q_dec, k_dec, blk)


def _mix_out_kernel(u_ref, g_ref, ys_ref, or_ref, x_ref, mod_ref, d_ref, wglu_ref, wo_ref, o_ref):
    y = u_ref[0].astype(F32) * d_ref[...] + ys_ref[0, 0] + ys_ref[1, 0]
    y = jax.nn.gelu(y)
    a = y * jax.nn.sigmoid(jnp.dot(y.astype(BF16), wglu_ref[...], preferred_element_type=F32))
    o = or_ref[0, 0] + or_ref[1, 0]
    heads = []
    for hd in range(RET_HEADS):
        oh = o[:, hd * RET_HEAD_DIM:(hd + 1) * RET_HEAD_DIM]
        heads.append(oh * lax.rsqrt(jnp.mean(oh * oh, axis=-1, keepdims=True) + RMS_EPS))
    r = jnp.concatenate(heads, axis=1) * jax.nn.silu(g_ref[0].astype(F32))
    m = jnp.dot(a.astype(BF16), wo_ref[:S5_WIDTH, :], preferred_element_type=F32)
    m = m + jnp.dot(r.astype(BF16), wo_ref[S5_WIDTH:, :], preferred_element_type=F32)
    o_ref[0] = x_ref[0] + mod_ref[0, 2:3, :] * m


def mix_out(xc, proj, y_s5, o_ret, mod, d_skip, w_glu, w_out, n_lat):
    bsz, nt, d = xc.shape
    tm = TOKEN_TILE
    w = S5_WIDTH
    dirs = pl.BlockSpec((2, 1, tm, w), lambda b, j: (0, b, j, 0))
    return pl.pallas_call(
        _mix_out_kernel,
        grid=(bsz, nt // tm),
        in_specs=[pl.BlockSpec((1, tm, w), lambda b, j: (b, j, 0)),
                  pl.BlockSpec((1, tm, w), lambda b, j: (b, j, 4)),
                  dirs, dirs,
                  pl.BlockSpec((1, tm, d), lambda b, j: (b, j, 0)),
                  pl.BlockSpec((1, SUBLANES, d), _mod_index(n_lat // tm, bsz)),
                  pl.BlockSpec((1, w), lambda b, j: (0, 0)),
                  pl.BlockSpec((w, w), lambda b, j: (0, 0)),
                  pl.BlockSpec((w + RET_WIDTH, d), lambda b, j: (0, 0))],
        out_specs=pl.BlockSpec((1, tm, d), lambda b, j: (b, j, 0)),
        out_shape=jax.ShapeDtypeStruct((bsz, nt, d), F32),
        compiler_params=_params(("arbitrary", "arbitrary")),
        name="mix_out",
    )(proj, proj, y_s5, o_ret, xc, mod, d_skip.reshape(1, w), w_glu.astype(BF16), w_out.astype(BF16))


def _na_band_start(r0, rows):
    return jnp.clip(r0 - NA_WIN_ROWS // 2, 0, rows - NA_BAND_ROWS)


def _na_bias(rpb, rows):
    w = GRID_W
    n_var = (NA_BAND_ROWS - NA_Q_ROWS) // NA_Q_ROWS + 1
    half = NA_WIN_ROWS // 2
    r0 = np.array([v * NA_Q_ROWS if v * NA_Q_ROWS <= half else rows - NA_BAND_ROWS + v * NA_Q_ROWS
                   for v in range(n_var)])
    bs = np.clip(r0 - half, 0, rows - NA_BAND_ROWS)
    assert list(r0 - bs) == [v * NA_Q_ROWS for v in range(n_var)]
    r = r0[:, None] + np.arange(NA_Q_ROWS)[None, :]
    rs = np.clip(r - half, 0, rows - NA_WIN_ROWS)
    a = bs[:, None] + np.arange(NA_BAND_ROWS)[None, :]
    row_ok = (a[:, None, :] >= rs[:, :, None]) & (a[:, None, :] < rs[:, :, None] + NA_WIN_ROWS)
    row_off = np.clip(a[:, None, :] - r[:, :, None] + (NA_WIN_ROWS - 1), 0, 2 * NA_WIN_ROWS - 2)
    col = np.arange(w)
    col_start = np.clip(col - NA_WIN_COLS // 2, 0, w - NA_WIN_COLS)
    col_ok = (col[None, :] >= col_start[:, None]) & (col[None, :] < col_start[:, None] + NA_WIN_COLS)
    col_off = col[None, :] - col[:, None] + (NA_WIN_COLS - 1)
    col_sel = (col_off[:, :, None] == np.arange(2 * NA_WIN_COLS - 1)) & col_ok[:, :, None]
    tiles = jnp.sum(rpb.astype(F32)[:, :, None, None, :] * jnp.asarray(col_sel, F32)[None, None], axis=-1)
    tiles = jnp.where(jnp.asarray(col_ok)[None, None], tiles, MASK_VALUE)
    masked = jnp.full((NA_HEADS, w, w), MASK_VALUE, F32)
    groups = []
    for v in range(n_var):
        per_row = []
        for rr in range(NA_Q_ROWS):
            per_row.append(jnp.concatenate(
                [tiles[:, int(row_off[v, rr, i])] if row_ok[v, rr, i] else masked for i in range(NA_BAND_ROWS)],
                axis=-1))
        groups.append(jnp.concatenate(per_row, axis=1))
    return jnp.stack(groups)


def _na_kernel(q_ref, kb_ref, vb_ref, kc_ref, vc_ref, bias_ref, o_ref, s_ref, p_ref):
    scale = NA_HEAD_DIM ** -0.5
    nt_dims = (((1,), (1,)), ((), ()))
    nq = q_ref.shape[1]
    n_ctx = kc_ref.shape[1]
    n_slot = s_ref.shape[0]
    heads_per_tile = LANES // NA_HEAD_DIM
    lane = lax.broadcasted_iota(jnp.int32, (nq, LANES), 1)
    for j in range(NA_WIDTH // LANES):
        sl = slice(j * LANES, (j + 1) * LANES)
        q2 = q_ref[0, :, sl].astype(F32) * scale
        k2 = kb_ref[0, :, sl]
        v2 = vb_ref[0, :, sl]
        kc2 = kc_ref[0, :, sl]
        vc2 = vc_ref[0, :, sl]
        slot = j % n_slot
        qm = jnp.concatenate(
            [jnp.where((lane >= hh * NA_HEAD_DIM) & (lane < (hh + 1) * NA_HEAD_DIM), q2, 0.0).astype(BF16)
             for hh in range(heads_per_tile)], axis=0)
        s_ref[slot, :, :n_ctx] = lax.dot_general(qm, kc2, nt_dims, preferred_element_type=F32)
        s_loc = lax.dot_general(qm, k2, nt_dims, preferred_element_type=F32)
        for hh in range(heads_per_tile):
            rows = slice(hh * nq, (hh + 1) * nq)
            s_ref[slot, rows, n_ctx:] = s_loc[rows] + bias_ref[0, j * heads_per_tile + hh]
        inv = []
        for c in range(heads_per_tile * nq // NA_SOFTMAX_ROWS):
            rows = slice(c * NA_SOFTMAX_ROWS, (c + 1) * NA_SOFTMAX_ROWS)
            s = s_ref[slot, rows, :]
            p = jnp.exp(s - jnp.max(s, axis=-1, keepdims=True))
            inv.append(1.0 / jnp.sum(p, axis=-1, keepdims=True))
            p_ref[slot, rows, :] = p.astype(BF16)
        o = jnp.dot(p_ref[slot, :, :n_ctx], vc2, preferred_element_type=F32)
        o = o + jnp.dot(p_ref[slot, :, n_ctx:], v2, preferred_element_type=F32)
        o = o * jnp.concatenate(inv, axis=0)
        o2 = o[:nq]
        for hh in range(1, heads_per_tile):
            o2 = jnp.where(lane >= hh * NA_HEAD_DIM, o[hh * nq:(hh + 1) * nq], o2)
        o_ref[0, :, sl] = o2.astype(o_ref.dtype)


def na_attention(qkv, rpb, n_lat):
    bsz, nt, _ = qkv.shape
    w = GRID_W
    rows = n_lat // w
    n_ctx = nt - n_lat
    nq = NA_Q_ROWS * w
    band = NA_BAND_ROWS * w
    n_all = n_ctx + band
    stacked = nq * (LANES // NA_HEAD_DIM)
    assert n_ctx % LANES == 0 and band % LANES == 0 and rows % NA_Q_ROWS == 0 and rows >= NA_BAND_ROWS + NA_WIN_ROWS // 2

    def band_start(g):
        return _na_band_start(g * NA_Q_ROWS, rows)

    def band_spec(j):
        return pl.BlockSpec((pl.Element(1), pl.Element(band), pl.Element(NA_WIDTH)),
                            lambda b, g: (b, band_start(g) * w, j * NA_WIDTH))

    def ctx_spec(j):
        return pl.BlockSpec((1, n_ctx, NA_WIDTH), lambda b, g: (b, n_lat // n_ctx, j))

    return pl.pallas_call(
        _na_kernel,
        grid=(bsz, rows // NA_Q_ROWS),
        in_specs=[pl.BlockSpec((1, nq, NA_WIDTH), lambda b, g: (b, g, 0)),
                  band_spec(1), band_spec(2), ctx_spec(1), ctx_spec(2),
                  pl.BlockSpec((1, NA_HEADS, nq, band), lambda b, g: (g - band_start(g) // NA_Q_ROWS, 0, 0, 0))],
        out_specs=pl.BlockSpec((1, nq, NA_WIDTH), lambda b, g: (b, g, 0)),
        out_shape=jax.ShapeDtypeStruct((bsz, n_lat, NA_WIDTH), BF16),
        scratch_shapes=[pltpu.VMEM((NA_SCORE_SLOTS, stacked, n_all), F32),
                        pltpu.VMEM((NA_SCORE_SLOTS, stacked, n_all), BF16)],
        compiler_params=_params(("arbitrary", "arbitrary")),
        name="na_attention",
    )(qkv, qkv, qkv, qkv, qkv, _na_bias(rpb, rows))


def _proj_res_kernel(a_ref, x_ref, mod_ref, w_ref, o_ref):
    m = jnp.dot(a_ref[0], w_ref[...], preferred_element_type=F32)
    o_ref[0] = x_ref[0] + mod_ref[0, 2:3, :] * m


def proj_residual(a, x, mod, w):
    bsz, n, k = a.shape
    d = x.shape[-1]
    tm = TOKEN_TILE
    return pl.pallas_call(
        _proj_res_kernel,
        grid=(bsz, n // tm),
        in_specs=[pl.BlockSpec((1, tm, k), lambda b, j: (b, j, 0)),
                  pl.BlockSpec((1, tm, d), lambda b, j: (b, j, 0)),
                  pl.BlockSpec((1, SUBLANES, d), lambda b, j: (b, 0, 0)),
                  pl.BlockSpec((k, d), lambda b, j: (0, 0))],
        out_specs=pl.BlockSpec((1, tm, d), lambda b, j: (b, j, 0)),
        out_shape=jax.ShapeDtypeStruct((bsz, n, d), F32),
        compiler_params=_params(("arbitrary", "arbitrary")),
        name="proj_residual",
    )(a, x, mod, w.astype(BF16))


def _store_row_tiled(ref, value, index=()):
    rows, d = value.shape
    parts = d // LANES
    for j in range(parts):
        ref[index + (pl.ds(j, rows, stride=parts), slice(None))] = value[:, j * LANES:(j + 1) * LANES]


def _load_row_tiled(ref, rows, d, index=()):
    parts = d // LANES
    return jnp.concatenate([ref[index + (pl.ds(j, rows, stride=parts), slice(None))] for j in range(parts)], axis=1)


def _router_kernel(x_ref, mod_ref, nw_ref, wr_ref, br_ref, tri_ref, h_ref, idx_ref, gate_ref, rank_ref, cnt_ref,
                   run_ref):
    first_step = (pl.program_id(0) == 0) & (pl.program_id(1) == 0)

    @pl.when(first_step)
    def _():
        run_ref[...] = jnp.zeros_like(run_ref)

    h = _norm_mod(x_ref[0], nw_ref[...], mod_ref[0, 3:4, :], mod_ref[0, 4:5, :])
    _store_row_tiled(h_ref, h)
    logits = jnp.dot(h.astype(BF16), wr_ref[...], preferred_element_type=F32) + br_ref[...]
    lane = lax.broadcasted_iota(jnp.int32, logits.shape, 1)
    lane_f = lane.astype(F32)
    vals, idxs = [], []
    cur = logits
    for _ in range(MOE_TOP_K):
        m = jnp.max(cur, axis=-1, keepdims=True)
        first = jnp.min(jnp.where(cur == m, lane_f, float(LANES)), axis=-1, keepdims=True)
        vals.append(m)
        idxs.append(first)
        cur = jnp.where(lane_f == first, MASK_VALUE, cur)
    es = [jnp.exp(v - vals[0]) for v in vals]
    tot = es[0]
    for e in es[1:]:
        tot = tot + e
    onehots = [(lane_f == idxs[k]).astype(F32) for k in range(MOE_TOP_K)]
    multi = onehots[0]
    for oh in onehots[1:]:
        multi = multi + oh
    before = jnp.dot(tri_ref[...], multi.astype(BF16), preferred_element_type=F32) + run_ref[0:1, :]
    idx_out = jnp.zeros(logits.shape, F32)
    gate_out = jnp.zeros(logits.shape, F32)
    rank_out = jnp.zeros(logits.shape, F32)
    for k in range(MOE_TOP_K):
        idx_out = jnp.where(lane == k, idxs[k], idx_out)
        gate_out = jnp.where(lane == k, es[k] / tot, gate_out)
        rank_out = jnp.where(lane == k, jnp.sum(before * onehots[k], axis=-1, keepdims=True), rank_out)
    idx_ref[0] = idx_out.astype(jnp.int32)
    gate_ref[0] = gate_out
    rank_ref[0] = rank_out.astype(jnp.int32)
    run_ref[0:1, :] = run_ref[0:1, :] + jnp.sum(multi, axis=0, keepdims=True)
    cnt_ref[...] = jnp.broadcast_to(run_ref[0:1, :], cnt_ref.shape)


def router(xc, mod, norm_w, w_router, b_router, n_lat):
    bsz, nt, d = xc.shape
    tm = TOKEN_TILE
    tiles = nt // tm
    parts = d // LANES
    wr = jnp.pad(w_router, ((0, 0), (0, LANES - N_EXPERTS))).astype(BF16)
    br = jnp.pad(b_router.astype(F32), (0, LANES - N_EXPERTS), constant_values=MASK_VALUE).reshape(1, LANES)
    tri = jnp.tril(jnp.ones((tm, tm), BF16), k=-1)
    tok = lambda n, dt: jax.ShapeDtypeStruct((bsz, nt, n), dt)
    out = lambda n: pl.BlockSpec((1, tm, n), lambda b, j: (b, j, 0))
    return pl.pallas_call(
        _router_kernel,
        grid=(bsz, nt // tm),
        in_specs=[pl.BlockSpec((1, tm, d), lambda b, j: (b, j, 0)),
                  pl.BlockSpec((1, SUBLANES, d), _mod_index(n_lat // tm, bsz)),
                  pl.BlockSpec((1, d), lambda b, j: (0, 0)),
                  pl.BlockSpec((d, LANES), lambda b, j: (0, 0)),
                  pl.BlockSpec((1, LANES), lambda b, j: (0, 0)),
                  pl.BlockSpec((tm, tm), lambda b, j: (0, 0))],
        out_specs=[pl.BlockSpec((tm * parts, LANES), lambda b, j: (b * tiles + j, 0)),
                   out(LANES), out(LANES), out(LANES),
                   pl.BlockSpec((SUBLANES, LANES), lambda b, j: (0, 0))],
        out_shape=[jax.ShapeDtypeStruct((bsz * nt * parts, LANES), F32),
                   tok(LANES, jnp.int32), tok(LANES, F32), tok(LANES, jnp.int32),
                   jax.ShapeDtypeStruct((SUBLANES, LANES), F32)],
        scratch_shapes=[pltpu.VMEM((SUBLANES, LANES), F32)],
        compiler_params=_params(("arbitrary", "arbitrary")),
        name="router",
    )(xc, mod, norm_w.reshape(1, d), wr, br, tri)


def _moe_kernel(be_ref, nu_ref, x_ref, w1_ref, b1_ref, w2_ref, b2_ref, o_ref, w1b_ref, w2b_ref):
    i = pl.program_id(0)
    ff = w2_ref.shape[2]
    used = i < nu_ref[0]

    @pl.when(used & ((i == 0) | (be_ref[i] != be_ref[jnp.maximum(i - 1, 0)])))
    def _():
        w1b_ref[...] = w1_ref[0, 0].astype(BF16)
        w2b_ref[...] = w2_ref[0, 0].astype(BF16)

    @pl.when(used)
    def _():
        d = w1b_ref.shape[0]
        x = _load_row_tiled(x_ref, MOE_TILE, d)
        t = jnp.dot(x.astype(BF16), w1b_ref[...], preferred_element_type=F32) + b1_ref[0, 0]
        x_glu = jnp.minimum(t[:, :ff], SWIGLU_LIMIT)
        x_lin = jnp.clip(t[:, ff:], -SWIGLU_LIMIT, SWIGLU_LIMIT)
        act = x_glu * jax.nn.sigmoid(SWIGLU_ALPHA * x_glu) * (x_lin + 1)
        y = jnp.dot(act.astype(BF16), w2b_ref[...], preferred_element_type=F32) + b2_ref[0, 0]
        _store_row_tiled(o_ref, y)

    @pl.when(jnp.logical_not(used))
    def _():
        o_ref[...] = jnp.zeros_like(o_ref)


def moe_experts(xs, block_e, n_used, w1, b1, w2, b2, layer):
    depth, ne, d, ff2 = w1.shape
    ff = w2.shape[2]
    tm = MOE_TILE * (d // LANES)
    nb = xs.shape[0] // tm
    grid_spec = pltpu.PrefetchScalarGridSpec(
        num_scalar_prefetch=2,
        grid=(nb,),
        in_specs=[pl.BlockSpec((tm, LANES), lambda i, be, nu: (jnp.where(i < nu[0], i, 0), 0)),
                  pl.BlockSpec((1, 1, d, ff2), lambda i, be, nu: (layer, be[i], 0, 0)),
                  pl.BlockSpec((1, 1, 1, ff2), lambda i, be, nu: (layer, be[i], 0, 0)),
                  pl.BlockSpec((1, 1, ff, d), lambda i, be, nu: (layer, be[i], 0, 0)),
                  pl.BlockSpec((1, 1, 1, d), lambda i, be, nu: (layer, be[i], 0, 0))],
        out_specs=pl.BlockSpec((tm, LANES), lambda i, be, nu: (i, 0)),
        scratch_shapes=[pltpu.VMEM((d, ff2), BF16), pltpu.VMEM((ff, d), BF16)],
    )
    return pl.pallas_call(
        _moe_kernel,
        grid_spec=grid_spec,
        out_shape=jax.ShapeDtypeStruct(xs.shape, F32),
        compiler_params=_params(("arbitrary",)),
        name="moe_experts",
    )(block_e, n_used, xs, w1, b1.reshape(depth, ne, 1, ff2), w2, b2.reshape(depth, ne, 1, d))


def _moe_plan(idx, rank, counts, n_tok):
    counts = counts.astype(jnp.int32)
    padded = (counts + MOE_TILE - 1) // MOE_TILE * MOE_TILE
    padded_ends = jnp.cumsum(padded)
    padded_starts = padded_ends - padded
    onehot = idx[..., None] == jnp.arange(N_EXPERTS, dtype=jnp.int32)
    slot_of = rank + jnp.sum(jnp.where(onehot, padded_starts, 0), axis=-1)
    n_blocks = -(-n_tok * MOE_TOP_K // MOE_TILE) + N_EXPERTS
    block_row = jnp.arange(n_blocks, dtype=jnp.int32) * MOE_TILE
    block_e = jnp.minimum(jnp.sum(padded_ends[None, :] <= block_row[:, None], axis=1), N_EXPERTS - 1)
    n_used = padded_ends[-1:] // MOE_TILE
    last_block = jnp.concatenate([jnp.where(counts > 0, padded_ends - MOE_TILE, -1), n_used])
    slot_tiles = slot_of.astype(jnp.int32).reshape(n_tok // TOKEN_TILE, 1, TOKEN_TILE * MOE_TOP_K)
    return slot_tiles, block_e.astype(jnp.int32), n_used.astype(jnp.int32), last_block.astype(jnp.int32), n_blocks


def _dispatch_kernel(slot_ref, last_ref, h_ref, xs_ref, zero_ref, sem, *, parts):
    tm = h_ref.shape[0] // parts
    block = MOE_TILE * parts

    @pl.when(pl.program_id(0) == 0)
    def _():
        zero_ref[...] = jnp.zeros_like(zero_ref)

        def zero_copy(e):
            row = pl.multiple_of(jnp.maximum(last_ref[e], 0) * parts, block)
            return pltpu.make_async_copy(zero_ref, xs_ref.at[pl.ds(row, block)], sem)

        def block_copy(i):
            return pltpu.make_async_copy(zero_ref, xs_ref.at[pl.ds(pl.multiple_of(i * block, block), block)], sem)

        n_used = last_ref[N_EXPERTS]
        n_blocks = xs_ref.shape[0] // block
        lax.fori_loop(n_used, n_blocks, lambda i, c: (block_copy(i).start(), c)[1], 0)
        for e in range(N_EXPERTS):
            pl.when(last_ref[e] >= 0)(lambda e=e: zero_copy(e).start())
        for e in range(N_EXPERTS):
            pl.when(last_ref[e] >= 0)(lambda e=e: zero_copy(e).wait())
        lax.fori_loop(n_used, n_blocks, lambda i, c: (block_copy(i).wait(), c)[1], 0)

    def issue(t, carry):
        for k in range(MOE_TOP_K):
            s = slot_ref[0, 0, t * MOE_TOP_K + k]
            src = h_ref.at[pl.ds(pl.multiple_of(t * parts, parts), parts)]
            dst = xs_ref.at[pl.ds(pl.multiple_of(s * parts, parts), parts)]
            pltpu.make_async_copy(src, dst, sem).start(priority=k % 2)
        return carry

    lax.fori_loop(0, tm, issue, 0)
    for k in range(MOE_TOP_K):
        pltpu.make_async_copy(h_ref, xs_ref.at[pl.ds(0, tm * parts)], sem).wait()


def moe_dispatch(h, slot_tiles, last_block, n_blocks, n_tok):
    parts = h.shape[0] // n_tok
    tm = TOKEN_TILE
    return pl.pallas_call(
        functools.partial(_dispatch_kernel, parts=parts),
        grid=(n_tok // tm,),
        in_specs=[pl.BlockSpec((1, 1, tm * MOE_TOP_K), lambda i: (i, 0, 0), memory_space=pltpu.SMEM),
                  pl.BlockSpec(memory_space=pltpu.SMEM),
                  pl.BlockSpec((tm * parts, LANES), lambda i: (i, 0))],
        out_specs=pl.BlockSpec(memory_space=pl.ANY),
        out_shape=jax.ShapeDtypeStruct((n_blocks * MOE_TILE * parts, LANES), F32),
        scratch_shapes=[pltpu.VMEM((MOE_TILE * parts, LANES), F32), pltpu.SemaphoreType.DMA(())],
        compiler_params=_params(("arbitrary",)),
        name="moe_dispatch",
    )(slot_tiles, last_block, h)


def _combine_kernel(slot_ref, x_ref, gate_ref, mod_ref, *rest, final):
    ys_ref, o_ref, ybuf_ref, sem = rest[-4:]
    tm, d = x_ref.shape[1:]
    parts = d // LANES

    def issue(t, carry):
        for k in range(MOE_TOP_K):
            s = slot_ref[0, 0, t * MOE_TOP_K + k]
            src = ys_ref.at[pl.ds(pl.multiple_of(s * parts, parts), parts)]
            dst = ybuf_ref.at[k, pl.ds(pl.multiple_of(t * parts, parts), parts)]
            pltpu.make_async_copy(src, dst, sem).start(priority=k % 2)
        return carry

    lax.fori_loop(0, tm, issue, 0)
    for k in range(MOE_TOP_K):
        pltpu.make_async_copy(ys_ref.at[pl.ds(0, tm * parts)], ybuf_ref.at[k], sem).wait()

    gate = gate_ref[0]
    y = None
    for k in range(MOE_TOP_K):
        term = gate[:, k:k + 1] * _load_row_tiled(ybuf_ref, tm, d, (k,))
        y = term if y is None else y + term
    x = x_ref[0] + mod_ref[0, 5:6, :] * y
    if final:
        fw_ref = rest[0]
        x = x * lax.rsqrt(jnp.mean(x * x, axis=-1, keepdims=True) + RMS_EPS) * fw_ref[...]
    o_ref[0] = x


def moe_combine(xc, ys, slot_tiles, gates, mod, n_lat, final_w=None):
    bsz, nt, d = xc.shape
    tm = TOKEN_TILE
    tiles = nt // tm
    in_specs = [pl.BlockSpec((1, 1, tm * MOE_TOP_K), lambda b, j: (b * tiles + j, 0, 0), memory_space=pltpu.SMEM),
                pl.BlockSpec((1, tm, d), lambda b, j: (b, j, 0)),
                pl.BlockSpec((1, tm, LANES), lambda b, j: (b, j, 0)),
                pl.BlockSpec((1, SUBLANES, d), _mod_index(n_lat // tm, bsz))]
    args = [slot_tiles, xc, gates, mod]
    if final_w is not None:
        in_specs.append(pl.BlockSpec((1, d), lambda b, j: (0, 0)))
        args.append(final_w.reshape(1, d))
    in_specs.append(pl.BlockSpec(memory_space=pl.ANY))
    args.append(ys)
    return pl.pallas_call(
        functools.partial(_combine_kernel, final=final_w is not None),
        grid=(bsz, tiles),
        in_specs=in_specs,
        out_specs=pl.BlockSpec((1, tm, d), lambda b, j: (b, j, 0)),
        out_shape=jax.ShapeDtypeStruct((bsz, nt, d), F32),
        scratch_shapes=[pltpu.VMEM((MOE_TOP_K, tm * (d // LANES), LANES), F32), pltpu.SemaphoreType.DMA(())],
        compiler_params=_params(("arbitrary", "arbitrary")),
        name="moe_combine",
    )(*args)


def moe_layer(xc, mod, norm_w, w_router, b_router, experts, n_lat, final_w=None):
    bsz, nt, d = xc.shape
    n_tok = bsz * nt
    h, idx, gates, rank, counts = router(xc, mod, norm_w, w_router, b_router, n_lat)
    top = lambda a: a.reshape(n_tok, LANES)[:, :MOE_TOP_K]
    slot_tiles, block_e, n_used, last_block, n_blocks = _moe_plan(top(idx), top(rank), counts[0, :N_EXPERTS], n_tok)
    xs = moe_dispatch(h, slot_tiles, last_block, n_blocks, n_tok)
    ys = moe_experts(xs, block_e, n_used, *experts)
    return moe_combine(xc, ys, slot_tiles, gates, mod, n_lat, final_w)


def kernel(x, c, ctx, c_ctx, ada_w, ada_b, norm_w, final_norm_w, ev_w_in, ev_w_out, s5_lam_re, s5_lam_im, s5_log_step, s5_b_re, s5_b_im, s5_c_re, s5_c_im, s5_d, s5_w_glu, ret_log_decay, na_w_qkv, na_w_o, na_rpb, moe_w_router, moe_b_router, moe_w1, moe_b1, moe_w2, moe_b2):
    bsz, n_lat, d = x.shape
    n_ctx = ctx.shape[1]
    depth = ada_w.shape[0]
    mod = adaln_table(c, c_ctx, ada_w, ada_b)
    xc = jnp.concatenate([x, ctx], axis=1)
    for i in range(depth):
        last = i == depth - 1
        j = i // 2
        if i % 2 == 0:
            proj = norm_proj(xc, mod[i], norm_w[i, 0], ev_w_in[j], n_lat, 0)
            s5p = _s5_layout(s5_lam_re[j], s5_lam_im[j], s5_log_step[j], s5_b_re[j], s5_b_im[j],
                             s5_c_re[j], s5_c_im[j])
            y_s5 = s5_scan(proj, s5p, n_lat)
            o_ret = retention(proj, ret_log_decay[j], n_lat)
            xc = mix_out(xc, proj, y_s5, o_ret, mod[i], s5_d[j], s5_w_glu[j], ev_w_out[j], n_lat)
        else:
            qkv = norm_proj(xc, mod[i], norm_w[i, 0], na_w_qkv[j], n_lat, 0)
            att = na_attention(qkv, na_rpb[j], n_lat)
            assert last, "an odd layer is only supported as the final layer (no context output needed)"
            xc = proj_residual(att, xc, mod[i], na_w_o[j])
        experts = (moe_w1, moe_b1, moe_w2, moe_b2, i)
        if last:
            return moe_layer(xc[:, :n_lat], mod[i], norm_w[i, 1], moe_w_router[i], moe_b_router[i],
                             experts, n_lat, final_norm_w)
        xc = moe_layer(xc, mod[i], norm_w[i, 1], moe_w_router[i], moe_b_router[i], experts, n_lat)
```

```python
import functools
import math

import numpy as np
import jax
import jax.numpy as jnp
from jax import lax
from jax.experimental import pallas as pl
from jax.experimental.pallas import tpu as pltpu

F32 = jnp.float32
BF16 = jnp.bfloat16

GRID_W = 64
RMS_EPS = 1e-6
S5_WIDTH = 512
S5_GROUP = 16
S5_GROUPS = S5_WIDTH // S5_GROUP
S5_STATE = 64
RET_HEADS = 4
RET_HEAD_DIM = 128
RET_WIDTH = RET_HEADS * RET_HEAD_DIM
ROPE_BASE = 10000.0
NA_HEADS = 16
NA_HEAD_DIM = 64
NA_WIDTH = NA_HEADS * NA_HEAD_DIM
NA_WIN_ROWS = 8
NA_WIN_COLS = 16
N_EXPERTS = 32
MOE_TOP_K = 4
SWIGLU_LIMIT = 7.0
SWIGLU_ALPHA = 1.702

LANES = 128
SUBLANES = 8
MXU_DIM = 256
V7X_VMEM_BYTES = 64 * 1024 * 1024
VMEM_LIMIT = V7X_VMEM_BYTES * 7 // 8

TOKEN_TILE = 256
S5_CHUNK = 256
RET_CHUNK = 256
MOE_TILE = 512
DMA_ISSUE_UNROLL = 8
NA_Q_ROWS = 2
NA_BAND_ROWS = 10
NA_SOFTMAX_ROWS = 16
NA_SCORE_SLOTS = 4
assert NA_BAND_ROWS >= NA_WIN_ROWS + NA_Q_ROWS - 1 and (NA_BAND_ROWS - NA_Q_ROWS) % NA_Q_ROWS == 0
MASK_VALUE = -1e30

S5_SLICES = 8
assert S5_SLICES * MXU_DIM == S5_GROUPS * S5_STATE


def _params(sem):
    return pltpu.CompilerParams(dimension_semantics=sem, vmem_limit_bytes=VMEM_LIMIT)


def _adaln_kernel(c_ref, w_ref, b_ref, o_ref):
    c = c_ref[...]
    s = c * jax.nn.sigmoid(c)
    o_ref[0] = jnp.dot(s, w_ref[0], preferred_element_type=F32,
                       precision=lax.Precision.HIGHEST) + b_ref[0]


def adaln_table(c, c_ctx, ada_w, ada_b):
    depth, d, d6 = ada_w.shape
    bsz = c.shape[0]
    cond = jnp.concatenate([c, c_ctx[None, :]], axis=0)
    cond = jnp.pad(cond, ((0, SUBLANES - (bsz + 1)), (0, 0)))
    tn = d6 // 4
    out = pl.pallas_call(
        _adaln_kernel,
        grid=(depth, d6 // tn),
        in_specs=[pl.BlockSpec((SUBLANES, d), lambda i, j: (0, 0)),
                  pl.BlockSpec((1, d, tn), lambda i, j: (i, 0, j)),
                  pl.BlockSpec((1, 1, tn), lambda i, j: (i, 0, j))],
        out_specs=pl.BlockSpec((1, SUBLANES, tn), lambda i, j: (i, 0, j)),
        out_shape=jax.ShapeDtypeStruct((depth, SUBLANES, d6), F32),
        compiler_params=_params(("arbitrary", "arbitrary")),
        name="adaln",
    )(cond, ada_w, ada_b.reshape(depth, 1, d6))
    tab = out[:, :bsz + 1].reshape(depth, bsz + 1, 6, d)
    return jnp.pad(tab, ((0, 0), (0, 0), (0, 2), (0, 0)))


def _norm_mod(x, nw, shift, scale):
    y = x * lax.rsqrt(jnp.mean(x * x, axis=-1, keepdims=True) + RMS_EPS)
    return (y * nw) * (1 + scale) + shift


def _mod_index(n_lat_tiles, bsz):
    def index(b, j):
        return (jnp.where(j >= n_lat_tiles, bsz, b), 0, 0)
    return index


def _proj_kernel(x_ref, mod_ref, nw_ref, w_ref, o_ref, *, shift_row):
    h = _norm_mod(x_ref[0], nw_ref[...], mod_ref[0, shift_row:shift_row + 1, :],
                  mod_ref[0, shift_row + 1:shift_row + 2, :])
    o_ref[0] = jnp.dot(h.astype(BF16), w_ref[...], preferred_element_type=F32).astype(o_ref.dtype)


def norm_proj(xc, mod, norm_w, w, n_lat, shift_row):
    bsz, nt, d = xc.shape
    n = w.shape[1]
    tm = TOKEN_TILE
    return pl.pallas_call(
        functools.partial(_proj_kernel, shift_row=shift_row),
        grid=(bsz, nt // tm),
        in_specs=[pl.BlockSpec((1, tm, d), lambda b, j: (b, j, 0)),
                  pl.BlockSpec((1, SUBLANES, d), _mod_index(n_lat // tm, bsz)),
                  pl.BlockSpec((1, d), lambda b, j: (0, 0)),
                  pl.BlockSpec((d, n), lambda b, j: (0, 0))],
        out_specs=pl.BlockSpec((1, tm, n), lambda b, j: (b, j, 0)),
        out_shape=jax.ShapeDtypeStruct((bsz, nt, n), BF16),
        compiler_params=_params(("arbitrary", "arbitrary")),
        name="norm_proj",
    )(xc, mod, norm_w.reshape(1, d), w.astype(BF16))


def _s5_discretize(lam_re, lam_im, log_step, b_re, b_im):
    lam_re = jnp.minimum(lam_re.astype(F32), -1e-4)
    lam_im = lam_im.astype(F32)
    step = jnp.exp(log_step.astype(F32))[..., None]
    mag = jnp.exp(lam_re * step)
    ang = lam_im * step
    a_re, a_im = mag * jnp.cos(ang), mag * jnp.sin(ang)
    den = lam_re * lam_re + lam_im * lam_im
    n_re, n_im = a_re - 1.0, a_im
    co_re = (n_re * lam_re + n_im * lam_im) / den
    co_im = (n_im * lam_re - n_re * lam_im) / den
    b_re, b_im = b_re.astype(F32), b_im.astype(F32)
    bb_re = co_re[..., None] * b_re - co_im[..., None] * b_im
    bb_im = co_re[..., None] * b_im + co_im[..., None] * b_re
    return a_re, a_im, bb_re, bb_im


def _s5_layout(lam_re, lam_im, log_step, b_re, b_im, c_re, c_im):
    a_re, a_im, bb_re, bb_im = _s5_discretize(lam_re, lam_im, log_step, b_re, b_im)
    gh = S5_GROUPS // 2
    nq = S5_STATE // S5_GROUP
    eye = jnp.eye(gh, dtype=F32)

    def arrange_a(a):
        a = a.reshape(2, 2, gh, nq, S5_GROUP)
        return a.transpose(0, 1, 3, 2, 4).reshape(2, S5_SLICES, MXU_DIM)

    def arrange_b(bb):
        bb = bb.reshape(2, 2, gh, nq, S5_GROUP, S5_GROUP)
        m = jnp.einsum('dhgqni,Gg->dhqGign', bb, eye)
        return m.reshape(2, S5_SLICES, MXU_DIM, MXU_DIM).astype(BF16)

    def arrange_c(c):
        c = c.astype(F32).reshape(2, 2, gh, S5_GROUP, nq, S5_GROUP)
        m = jnp.einsum('dhgiqn,gG->dhqgnGi', c, eye)
        return m.reshape(2, S5_SLICES, MXU_DIM, MXU_DIM).astype(BF16)

    return (arrange_a(a_re), arrange_a(a_im), arrange_b(bb_re), arrange_b(bb_im),
            arrange_c(c_re), arrange_c(-c_im.astype(F32)))


def _s5_kernel(u_ref, bre_ref, bim_ref, cre_ref, cim_ref, are_ref, aim_ref, y_ref, bu_ref, st_ref, h_ref, *, nb):
    t_chunk = S5_CHUNK
    half = MXU_DIM // LANES
    d = pl.program_id(0)

    @pl.when(pl.program_id(1) == 0)
    def _():
        h_ref[...] = jnp.zeros_like(h_ref)

    def slab(b, c, lh):
        return (b * 2 + c) * half + lh

    for b in range(nb):
        for h in range(2):
            ub = u_ref[b, :, h * MXU_DIM:(h + 1) * MXU_DIM]
            for q in range(S5_SLICES // 2):
                k = h * (S5_SLICES // 2) + q
                for c, w_ref in ((0, bre_ref), (1, bim_ref)):
                    r = jnp.dot(ub, w_ref[0, k], preferred_element_type=F32)
                    for lh in range(half):
                        bu_ref[slab(b, c, lh), pl.ds(k, t_chunk, stride=S5_SLICES), :] = r[:, lh * LANES:(lh + 1) * LANES]

    ar = [are_ref[0, :, lh * LANES:(lh + 1) * LANES] for lh in range(half)]
    ai = [aim_ref[0, :, lh * LANES:(lh + 1) * LANES] for lh in range(half)]

    def step(t, carry):
        tt = jnp.where(d == 0, t, t_chunk - 1 - t)
        new = []
        for b in range(nb):
            for lh in range(half):
                hr, hi = carry[2 * (b * half + lh)], carry[2 * (b * half + lh) + 1]
                rows = pl.ds(pl.multiple_of(tt * S5_SLICES, S5_SLICES), S5_SLICES)
                xr = bu_ref[slab(b, 0, lh), rows, :]
                xi = bu_ref[slab(b, 1, lh), rows, :]
                nr = ar[lh] * hr - ai[lh] * hi + xr
                ni = ar[lh] * hi + ai[lh] * hr + xi
                st_ref[slab(b, 0, lh), rows, :] = nr
                st_ref[slab(b, 1, lh), rows, :] = ni
                new += [nr, ni]
        return tuple(new)

    n_state = nb * half * 2
    carry = lax.fori_loop(0, t_chunk, step, tuple(h_ref[j] for j in range(n_state)), unroll=8)
    for j in range(n_state):
        h_ref[j] = carry[j]

    for b in range(nb):
        for h in range(2):
            acc = None
            for q in range(S5_SLICES // 2):
                k = h * (S5_SLICES // 2) + q
                for c, w_ref in ((0, cre_ref), (1, cim_ref)):
                    s = jnp.concatenate(
                        [st_ref[slab(b, c, lh), pl.ds(k, t_chunk, stride=S5_SLICES), :] for lh in range(half)], axis=1)
                    term = jnp.dot(s.astype(BF16), w_ref[0, k], preferred_element_type=F32)
                    acc = term if acc is None else acc + term
            y_ref[0, b, :, h * MXU_DIM:(h + 1) * MXU_DIM] = acc


def _scan_chunk_index(n_chunks, n_ctx_chunks):
    def chunk(d, i):
        fwd = lax.rem(i + (n_chunks - n_ctx_chunks), n_chunks)
        return jnp.where(d == 0, fwd, n_chunks - 1 - i)
    return chunk


def s5_scan(proj, s5p, n_lat):
    bsz, nt, _ = proj.shape
    a_re, a_im, bb_re, bb_im, cc_re, cc_im = s5p
    t = S5_CHUNK
    nc = nt // t
    chunk = _scan_chunk_index(nc, (nt - n_lat) // t)
    wspec = pl.BlockSpec((1, S5_SLICES, MXU_DIM, MXU_DIM), lambda d, i: (d, 0, 0, 0))
    aspec = pl.BlockSpec((1, S5_SLICES, MXU_DIM), lambda d, i: (d, 0, 0))
    n_slab = bsz * 2 * (MXU_DIM // LANES)
    return pl.pallas_call(
        functools.partial(_s5_kernel, nb=bsz),
        grid=(2, nc),
        in_specs=[pl.BlockSpec((bsz, t, S5_WIDTH), lambda d, i: (0, chunk(d, i), 0)),
                  wspec, wspec, wspec, wspec, aspec, aspec],
        out_specs=pl.BlockSpec((1, bsz, t, S5_WIDTH), lambda d, i: (d, 0, chunk(d, i), 0)),
        out_shape=jax.ShapeDtypeStruct((2, bsz, nt, S5_WIDTH), F32),
        scratch_shapes=[pltpu.VMEM((n_slab, S5_SLICES * S5_CHUNK, LANES), F32),
                        pltpu.VMEM((n_slab, S5_SLICES * S5_CHUNK, LANES), F32),
                        pltpu.VMEM((n_slab, S5_SLICES, LANES), F32)],
        compiler_params=_params(("arbitrary", "arbitrary")),
        name="s5_scan",
    )(proj, bb_re, bb_im, cc_re, cc_im, a_re, a_im)


def _rope_tables(n_lat, n_ctx):
    half = RET_HEAD_DIM // 4
    freq = ROPE_BASE ** (-jnp.arange(half, dtype=F32) / half)
    rows = n_lat // GRID_W
    ang_r = jnp.arange(rows, dtype=F32)[:, None] * freq[None, :]
    ang_c = jnp.arange(GRID_W, dtype=F32)[:, None] * freq[None, :]

    def table(fn):
        by_row = jnp.broadcast_to(fn(ang_r)[:, None, :], (rows, GRID_W, half))
        by_col = jnp.broadcast_to(fn(ang_c)[None, :, :], (rows, GRID_W, half))
        return jnp.concatenate([by_row, by_row, by_col, by_col], axis=-1).reshape(n_lat, RET_HEAD_DIM)

    cos, sin = table(jnp.cos), table(jnp.sin)
    cos = jnp.concatenate([cos, jnp.ones((n_ctx, RET_HEAD_DIM), F32)], axis=0)
    sin = jnp.concatenate([sin, jnp.zeros((n_ctx, RET_HEAD_DIM), F32)], axis=0)
    return cos, sin


def _rope_rotation_matrix():
    blk = RET_HEAD_DIM // 2
    half = blk // 2
    r = np.zeros((RET_HEAD_DIM, RET_HEAD_DIM), np.float32)
    for base in (0, blk):
        for l in range(half):
            r[base + l + half, base + l] = -1.0
            r[base + l, base + l + half] = 1.0
    return jnp.asarray(r, BF16)


def _ret_decay(log_decay, t):
    lg = log_decay.astype(F32)
    scale = RET_HEAD_DIM ** -0.5
    idx = jnp.arange(t, dtype=F32)
    diff = idx[:, None] - idx[None, :]
    diff = jnp.stack([diff, -diff])[:, None]
    inner = jnp.where(diff >= 0, jnp.exp(lg[:, :, None, None] * jnp.maximum(diff, 0.0)), 0.0) * scale
    pos = jnp.stack([idx, t - 1.0 - idx])
    q_dec = jnp.exp(lg[:, :, None] * (pos[:, None, :] + 1.0))
    k_dec = jnp.exp(lg[:, :, None] * (t - 1.0 - pos[:, None, :])) * scale
    blk = jnp.exp(lg * t)
    bcast = lambda v: jnp.broadcast_to(v[..., None], v.shape + (RET_HEAD_DIM,))
    return inner, bcast(q_dec), bcast(k_dec), jnp.broadcast_to(blk[:, :, None, None], (2, RET_HEADS, SUBLANES, RET_HEAD_DIM))


def _ret_kernel(q_ref, k_ref, v_ref, cos_ref, sin_ref, rot_ref, inner_ref, qd_ref, kd_ref, bd_ref, o_ref, s_ref):
    @pl.when(pl.program_id(1) == 0)
    def _():
        s_ref[...] = jnp.zeros_like(s_ref)

    cos = cos_ref[...]
    sin = sin_ref[...]
    rot = rot_ref[...]
    nt_dims = (((1,), (1,)), ((), ()))
    for b in range(q_ref.shape[0]):
        for hd in range(RET_HEADS):
            sl = slice(hd * RET_HEAD_DIM, (hd + 1) * RET_HEAD_DIM)
            q = q_ref[b, :, sl]
            k = k_ref[b, :, sl]
            v = v_ref[b, :, sl]
            qr = q.astype(F32) * cos + jnp.dot(q, rot, preferred_element_type=F32) * sin
            kr = k.astype(F32) * cos + jnp.dot(k, rot, preferred_element_type=F32) * sin
            qb = qr.astype(BF16)
            att = lax.dot_general(qb, kr.astype(BF16), nt_dims, preferred_element_type=F32) * inner_ref[0, hd]
            s = s_ref[b * RET_HEADS + hd]
            o = jnp.dot(att.astype(BF16), v, preferred_element_type=F32)
            o = o + jnp.dot(qb, s.astype(BF16), preferred_element_type=F32) * qd_ref[0, hd]
            kd_t = (kr * kd_ref[0, hd]).T.astype(BF16)
            s_ref[b * RET_HEADS + hd] = bd_ref[0, hd, 0:1, :] * s + jnp.dot(kd_t, v, preferred_element_type=F32)
            o_ref[0, b, :, sl] = o


def retention(proj, log_decay, n_lat):
    bsz, nt, _ = proj.shape
    t = RET_CHUNK
    nc = nt // t
    chunk = _scan_chunk_index(nc, (nt - n_lat) // t)
    cos, sin = _rope_tables(n_lat, nt - n_lat)
    inner, q_dec, k_dec, blk = _ret_decay(log_decay, t)
    w = RET_WIDTH

    def col(j):
        return pl.BlockSpec((bsz, t, w), lambda d, i: (0, chunk(d, i), j))

    tab = pl.BlockSpec((t, RET_HEAD_DIM), lambda d, i: (chunk(d, i), 0))
    dec = pl.BlockSpec((1, RET_HEADS, t, RET_HEAD_DIM), lambda d, i: (d, 0, 0, 0))
    return pl.pallas_call(
        _ret_kernel,
        grid=(2, nc),
        in_specs=[col(1), col(2), col(3), tab, tab,
                  pl.BlockSpec((RET_HEAD_DIM, RET_HEAD_DIM), lambda d, i: (0, 0)),
                  pl.BlockSpec((1, RET_HEADS, t, t), lambda d, i: (d, 0, 0, 0)),
                  dec, dec,
                  pl.BlockSpec((1, RET_HEADS, SUBLANES, RET_HEAD_DIM), lambda d, i: (d, 0, 0, 0))],
        out_specs=pl.BlockSpec((1, bsz, t, w), lambda d, i: (d, 0, chunk(d, i), 0)),
        out_shape=jax.ShapeDtypeStruct((2, bsz, nt, w), F32),
        scratch_shapes=[pltpu.VMEM((bsz * RET_HEADS, RET_HEAD_DIM, RET_HEAD_DIM), F32)],
        compiler_params=_params(("arbitrary", "arbitrary")),
        name="retention",
    )(proj, proj, proj, cos, sin, _rope_rotation_matrix(), inner, q_dec, k_dec, blk)


def _mix_out_kernel(u_ref, g_ref, ys_ref, or_ref, x_ref, mod_ref, d_ref, wglu_ref, wo_ref, o_ref):
    y = u_ref[0].astype(F32) * d_ref[...] + ys_ref[0, 0] + ys_ref[1, 0]
    y = jax.nn.gelu(y)
    a = y * jax.nn.sigmoid(jnp.dot(y.astype(BF16), wglu_ref[...], preferred_element_type=F32))
    o = or_ref[0, 0] + or_ref[1, 0]
    heads = []
    for hd in range(RET_HEADS):
        oh = o[:, hd * RET_HEAD_DIM:(hd + 1) * RET_HEAD_DIM]
        heads.append(oh * lax.rsqrt(jnp.mean(oh * oh, axis=-1, keepdims=True) + RMS_EPS))
    r = jnp.concatenate(heads, axis=1) * jax.nn.silu(g_ref[0].astype(F32))
    m = jnp.dot(a.astype(BF16), wo_ref[:S5_WIDTH, :], preferred_element_type=F32)
    m = m + jnp.dot(r.astype(BF16), wo_ref[S5_WIDTH:, :], preferred_element_type=F32)
    o_ref[0] = x_ref[0] + mod_ref[0, 2:3, :] * m


def mix_out(xc, proj, y_s5, o_ret, mod, d_skip, w_glu, w_out, n_lat):
    bsz, nt, d = xc.shape
    tm = TOKEN_TILE
    w = S5_WIDTH
    dirs = pl.BlockSpec((2, 1, tm, w), lambda b, j: (0, b, j, 0))
    return pl.pallas_call(
        _mix_out_kernel,
        grid=(bsz, nt // tm),
        in_specs=[pl.BlockSpec((1, tm, w), lambda b, j: (b, j, 0)),
                  pl.BlockSpec((1, tm, w), lambda b, j: (b, j, 4)),
                  dirs, dirs,
                  pl.BlockSpec((1, tm, d), lambda b, j: (b, j, 0)),
                  pl.BlockSpec((1, SUBLANES, d), _mod_index(n_lat // tm, bsz)),
                  pl.BlockSpec((1, w), lambda b, j: (0, 0)),
                  pl.BlockSpec((w, w), lambda b, j: (0, 0)),
                  pl.BlockSpec((w + RET_WIDTH, d), lambda b, j: (0, 0))],
        out_specs=pl.BlockSpec((1, tm, d), lambda b, j: (b, j, 0)),
        out_shape=jax.ShapeDtypeStruct((bsz, nt, d), F32),
        compiler_params=_params(("arbitrary", "arbitrary")),
        name="mix_out",
    )(proj, proj, y_s5, o_ret, xc, mod, d_skip.reshape(1, w), w_glu.astype(BF16), w_out.astype(BF16))


def _na_band_start(r0, rows):
    return jnp.clip(r0 - NA_WIN_ROWS // 2, 0, rows - NA_BAND_ROWS)


def _na_bias(rpb, rows):
    w = GRID_W
    n_var = (NA_BAND_ROWS - NA_Q_ROWS) // NA_Q_ROWS + 1
    half = NA_WIN_ROWS // 2
    r0 = np.array([v * NA_Q_ROWS if v * NA_Q_ROWS <= half else rows - NA_BAND_ROWS + v * NA_Q_ROWS
                   for v in range(n_var)])
    bs = np.clip(r0 - half, 0, rows - NA_BAND_ROWS)
    assert list(r0 - bs) == [v * NA_Q_ROWS for v in range(n_var)]
    r = r0[:, None] + np.arange(NA_Q_ROWS)[None, :]
    rs = np.clip(r - half, 0, rows - NA_WIN_ROWS)
    a = bs[:, None] + np.arange(NA_BAND_ROWS)[None, :]
    row_ok = (a[:, None, :] >= rs[:, :, None]) & (a[:, None, :] < rs[:, :, None] + NA_WIN_ROWS)
    row_off = np.clip(a[:, None, :] - r[:, :, None] + (NA_WIN_ROWS - 1), 0, 2 * NA_WIN_ROWS - 2)
    col = np.arange(w)
    col_start = np.clip(col - NA_WIN_COLS // 2, 0, w - NA_WIN_COLS)
    col_ok = (col[None, :] >= col_start[:, None]) & (col[None, :] < col_start[:, None] + NA_WIN_COLS)
    col_off = col[None, :] - col[:, None] + (NA_WIN_COLS - 1)
    col_sel = (col_off[:, :, None] == np.arange(2 * NA_WIN_COLS - 1)) & col_ok[:, :, None]
    tiles = jnp.sum(rpb.astype(F32)[:, :, None, None, :] * jnp.asarray(col_sel, F32)[None, None], axis=-1)
    tiles = jnp.where(jnp.asarray(col_ok)[None, None], tiles, MASK_VALUE)
    masked = jnp.full((NA_HEADS, w, w), MASK_VALUE, F32)
    groups = []
    for v in range(n_var):
        per_row = []
        for rr in range(NA_Q_ROWS):
            per_row.append(jnp.concatenate(
                [tiles[:, int(row_off[v, rr, i])] if row_ok[v, rr, i] else masked for i in range(NA_BAND_ROWS)],
                axis=-1))
        groups.append(jnp.concatenate(per_row, axis=1))
    return jnp.stack(groups)


def _na_kernel(q_ref, kb_ref, vb_ref, kc_ref, vc_ref, bias_ref, o_ref, s_ref, p_ref):
    scale = NA_HEAD_DIM ** -0.5
    nt_dims = (((1,), (1,)), ((), ()))
    nq = q_ref.shape[1]
    n_ctx = kc_ref.shape[1]
    n_slot = s_ref.shape[0]
    heads_per_tile = LANES // NA_HEAD_DIM
    lane = lax.broadcasted_iota(jnp.int32, (nq, LANES), 1)
    for j in range(NA_WIDTH // LANES):
        sl = slice(j * LANES, (j + 1) * LANES)
        q2 = q_ref[0, :, sl].astype(F32) * scale
        k2 = kb_ref[0, :, sl]
        v2 = vb_ref[0, :, sl]
        kc2 = kc_ref[0, :, sl]
        vc2 = vc_ref[0, :, sl]
        slot = j % n_slot
        qm = jnp.concatenate(
            [jnp.where((lane >= hh * NA_HEAD_DIM) & (lane < (hh + 1) * NA_HEAD_DIM), q2, 0.0).astype(BF16)
             for hh in range(heads_per_tile)], axis=0)
        s_ref[slot, :, :n_ctx] = lax.dot_general(qm, kc2, nt_dims, preferred_element_type=F32)
        s_loc = lax.dot_general(qm, k2, nt_dims, preferred_element_type=F32)
        for hh in range(heads_per_tile):
            rows = slice(hh * nq, (hh + 1) * nq)
            s_ref[slot, rows, n_ctx:] = s_loc[rows] + bias_ref[0, j * heads_per_tile + hh]
        inv = []
        for c in range(heads_per_tile * nq // NA_SOFTMAX_ROWS):
            rows = slice(c * NA_SOFTMAX_ROWS, (c + 1) * NA_SOFTMAX_ROWS)
            s = s_ref[slot, rows, :]
            p = jnp.exp(s - jnp.max(s, axis=-1, keepdims=True))
            inv.append(1.0 / jnp.sum(p, axis=-1, keepdims=True))
            p_ref[slot, rows, :] = p.astype(BF16)
        o = jnp.dot(p_ref[slot, :, :n_ctx], vc2, preferred_element_type=F32)
        o = o + jnp.dot(p_ref[slot, :, n_ctx:], v2, preferred_element_type=F32)
        o = o * jnp.concatenate(inv, axis=0)
        o2 = o[:nq]
        for hh in range(1, heads_per_tile):
            o2 = jnp.where(lane >= hh * NA_HEAD_DIM, o[hh * nq:(hh + 1) * nq], o2)
        o_ref[0, :, sl] = o2.astype(o_ref.dtype)


def na_attention(qkv, rpb, n_lat):
    bsz, nt, _ = qkv.shape
    w = GRID_W
    rows = n_lat // w
    n_ctx = nt - n_lat
    nq = NA_Q_ROWS * w
    band = NA_BAND_ROWS * w
    n_all = n_ctx + band
    stacked = nq * (LANES // NA_HEAD_DIM)
    assert n_ctx % LANES == 0 and band % LANES == 0 and rows % NA_Q_ROWS == 0 and rows >= NA_BAND_ROWS + NA_WIN_ROWS // 2

    def band_start(g):
        return _na_band_start(g * NA_Q_ROWS, rows)

    def band_spec(j):
        return pl.BlockSpec((pl.Element(1), pl.Element(band), pl.Element(NA_WIDTH)),
                            lambda b, g: (b, band_start(g) * w, j * NA_WIDTH))

    def ctx_spec(j):
        return pl.BlockSpec((1, n_ctx, NA_WIDTH), lambda b, g: (b, n_lat // n_ctx, j))

    return pl.pallas_call(
        _na_kernel,
        grid=(bsz, rows // NA_Q_ROWS),
        in_specs=[pl.BlockSpec((1, nq, NA_WIDTH), lambda b, g: (b, g, 0)),
                  band_spec(1), band_spec(2), ctx_spec(1), ctx_spec(2),
                  pl.BlockSpec((1, NA_HEADS, nq, band), lambda b, g: (g - band_start(g) // NA_Q_ROWS, 0, 0, 0))],
        out_specs=pl.BlockSpec((1, nq, NA_WIDTH), lambda b, g: (b, g, 0)),
        out_shape=jax.ShapeDtypeStruct((bsz, n_lat, NA_WIDTH), BF16),
        scratch_shapes=[pltpu.VMEM((NA_SCORE_SLOTS, stacked, n_all), F32),
                        pltpu.VMEM((NA_SCORE_SLOTS, stacked, n_all), BF16)],
        compiler_params=_params(("arbitrary", "arbitrary")),
        name="na_attention",
    )(qkv, qkv, qkv, qkv, qkv, _na_bias(rpb, rows))


def _proj_res_kernel(a_ref, x_ref, mod_ref, w_ref, o_ref):
    m = jnp.dot(a_ref[0], w_ref[...], preferred_element_type=F32)
    o_ref[0] = x_ref[0] + mod_ref[0, 2:3, :] * m


def proj_residual(a, x, mod, w):
    bsz, n, k = a.shape
    d = x.shape[-1]
    tm = TOKEN_TILE
    return pl.pallas_call(
        _proj_res_kernel,
        grid=(bsz, n // tm),
        in_specs=[pl.BlockSpec((1, tm, k), lambda b, j: (b, j, 0)),
                  pl.BlockSpec((1, tm, d), lambda b, j: (b, j, 0)),
                  pl.BlockSpec((1, SUBLANES, d), lambda b, j: (b, 0, 0)),
                  pl.BlockSpec((k, d), lambda b, j: (0, 0))],
        out_specs=pl.BlockSpec((1, tm, d), lambda b, j: (b, j, 0)),
        out_shape=jax.ShapeDtypeStruct((bsz, n, d), F32),
        compiler_params=_params(("arbitrary", "arbitrary")),
        name="proj_residual",
    )(a, x, mod, w.astype(BF16))


def _store_row_tiled(ref, value, index=()):
    rows, d = value.shape
    parts = d // LANES
    for j in range(parts):
        ref[index + (pl.ds(j, rows, stride=parts), slice(None))] = value[:, j * LANES:(j + 1) * LANES]


def _load_row_tiled(ref, rows, d, index=()):
    parts = d // LANES
    return jnp.concatenate([ref[index + (pl.ds(j, rows, stride=parts), slice(None))] for j in range(parts)], axis=1)


def _router_kernel(x_ref, mod_ref, nw_ref, wr_ref, br_ref, tri_ref, h_ref, idx_ref, gate_ref, rank_ref, cnt_ref,
                   run_ref):
    first_step = (pl.program_id(0) == 0) & (pl.program_id(1) == 0)

    @pl.when(first_step)
    def _():
        run_ref[...] = jnp.zeros_like(run_ref)

    h = _norm_mod(x_ref[0], nw_ref[...], mod_ref[0, 3:4, :], mod_ref[0, 4:5, :])
    _store_row_tiled(h_ref, h)
    logits = jnp.dot(h.astype(BF16), wr_ref[...], preferred_element_type=F32) + br_ref[...]
    lane = lax.broadcasted_iota(jnp.int32, logits.shape, 1)
    lane_f = lane.astype(F32)
    vals, idxs = [], []
    cur = logits
    for _ in range(MOE_TOP_K):
        m = jnp.max(cur, axis=-1, keepdims=True)
        first = jnp.min(jnp.where(cur == m, lane_f, float(LANES)), axis=-1, keepdims=True)
        vals.append(m)
        idxs.append(first)
        cur = jnp.where(lane_f == first, MASK_VALUE, cur)
    es = [jnp.exp(v - vals[0]) for v in vals]
    tot = es[0]
    for e in es[1:]:
        tot = tot + e
    onehots = [(lane_f == idxs[k]).astype(F32) for k in range(MOE_TOP_K)]
    multi = onehots[0]
    for oh in onehots[1:]:
        multi = multi + oh
    before = jnp.dot(tri_ref[...], multi.astype(BF16), preferred_element_type=F32) + run_ref[0:1, :]
    idx_out = jnp.zeros(logits.shape, F32)
    gate_out = jnp.zeros(logits.shape, F32)
    rank_out = jnp.zeros(logits.shape, F32)
    for k in range(MOE_TOP_K):
        idx_out = jnp.where(lane == k, idxs[k], idx_out)
        gate_out = jnp.where(lane == k, es[k] / tot, gate_out)
        rank_out = jnp.where(lane == k, jnp.sum(before * onehots[k], axis=-1, keepdims=True), rank_out)
    idx_ref[0] = idx_out.astype(jnp.int32)
    gate_ref[0] = gate_out
    rank_ref[0] = rank_out.astype(jnp.int32)
    run_ref[0:1, :] = run_ref[0:1, :] + jnp.sum(multi, axis=0, keepdims=True)
    cnt_ref[...] = jnp.broadcast_to(run_ref[0:1, :], cnt_ref.shape)


def router(xc, mod, norm_w, w_router, b_router, n_lat):
    bsz, nt, d = xc.shape
    tm = TOKEN_TILE
    tiles = nt // tm
    parts = d // LANES
    wr = jnp.pad(w_router, ((0, 0), (0, LANES - N_EXPERTS))).astype(BF16)
    br = jnp.pad(b_router.astype(F32), (0, LANES - N_EXPERTS), constant_values=MASK_VALUE).reshape(1, LANES)
    tri = jnp.tril(jnp.ones((tm, tm), BF16), k=-1)
    tok = lambda n, dt: jax.ShapeDtypeStruct((bsz, nt, n), dt)
    out = lambda n: pl.BlockSpec((1, tm, n), lambda b, j: (b, j, 0))
    return pl.pallas_call(
        _router_kernel,
        grid=(bsz, nt // tm),
        in_specs=[pl.BlockSpec((1, tm, d), lambda b, j: (b, j, 0)),
                  pl.BlockSpec((1, SUBLANES, d), _mod_index(n_lat // tm, bsz)),
                  pl.BlockSpec((1, d), lambda b, j: (0, 0)),
                  pl.BlockSpec((d, LANES), lambda b, j: (0, 0)),
                  pl.BlockSpec((1, LANES), lambda b, j: (0, 0)),
                  pl.BlockSpec((tm, tm), lambda b, j: (0, 0))],
        out_specs=[pl.BlockSpec((tm * parts, LANES), lambda b, j: (b * tiles + j, 0)),
                   out(LANES), out(LANES), out(LANES),
                   pl.BlockSpec((SUBLANES, LANES), lambda b, j: (0, 0))],
        out_shape=[jax.ShapeDtypeStruct((bsz * nt * parts, LANES), F32),
                   tok(LANES, jnp.int32), tok(LANES, F32), tok(LANES, jnp.int32),
                   jax.ShapeDtypeStruct((SUBLANES, LANES), F32)],
        scratch_shapes=[pltpu.VMEM((SUBLANES, LANES), F32)],
        compiler_params=_params(("arbitrary", "arbitrary")),
        name="router",
    )(xc, mod, norm_w.reshape(1, d), wr, br, tri)


def _moe_kernel(be_ref, nu_ref, x_ref, w1_ref, b1_ref, w2_ref, b2_ref, o_ref, w1b_ref, w2b_ref):
    i = pl.program_id(0)
    ff = w2_ref.shape[2]
    used = i < nu_ref[0]

    @pl.when(used & ((i == 0) | (be_ref[i] != be_ref[jnp.maximum(i - 1, 0)])))
    def _():
        w1b_ref[...] = w1_ref[0, 0].astype(BF16)
        w2b_ref[...] = w2_ref[0, 0].astype(BF16)

    @pl.when(used)
    def _():
        d = w1b_ref.shape[0]
        x = _load_row_tiled(x_ref, MOE_TILE, d)
        t = jnp.dot(x.astype(BF16), w1b_ref[...], preferred_element_type=F32) + b1_ref[0, 0]
        x_glu = jnp.minimum(t[:, :ff], SWIGLU_LIMIT)
        x_lin = jnp.clip(t[:, ff:], -SWIGLU_LIMIT, SWIGLU_LIMIT)
        act = x_glu * jax.nn.sigmoid(SWIGLU_ALPHA * x_glu) * (x_lin + 1)
        y = jnp.dot(act.astype(BF16), w2b_ref[...], preferred_element_type=F32) + b2_ref[0, 0]
        _store_row_tiled(o_ref, y)

    @pl.when(jnp.logical_not(used))
    def _():
        o_ref[...] = jnp.zeros_like(o_ref)


def moe_experts(xs, block_e, n_used, w1, b1, w2, b2, layer):
    depth, ne, d, ff2 = w1.shape
    ff = w2.shape[2]
    tm = MOE_TILE * (d // LANES)
    nb = xs.shape[0] // tm
    grid_spec = pltpu.PrefetchScalarGridSpec(
        num_scalar_prefetch=2,
        grid=(nb,),
        in_specs=[pl.BlockSpec((tm, LANES), lambda i, be, nu: (jnp.where(i < nu[0], i, 0), 0)),
                  pl.BlockSpec((1, 1, d, ff2), lambda i, be, nu: (layer, be[i], 0, 0)),
                  pl.BlockSpec((1, 1, 1, ff2), lambda i, be, nu: (layer, be[i], 0, 0)),
                  pl.BlockSpec((1, 1, ff, d), lambda i, be, nu: (layer, be[i], 0, 0)),
                  pl.BlockSpec((1, 1, 1, d), lambda i, be, nu: (layer, be[i], 0, 0))],
        out_specs=pl.BlockSpec((tm, LANES), lambda i, be, nu: (i, 0)),
        scratch_shapes=[pltpu.VMEM((d, ff2), BF16), pltpu.VMEM((ff, d), BF16)],
    )
    return pl.pallas_call(
        _moe_kernel,
        grid_spec=grid_spec,
        out_shape=jax.ShapeDtypeStruct(xs.shape, F32),
        compiler_params=_params(("arbitrary",)),
        name="moe_experts",
    )(block_e, n_used, xs, w1, b1.reshape(depth, ne, 1, ff2), w2, b2.reshape(depth, ne, 1, d))


def _moe_plan(idx, rank, counts, n_tok):
    counts = counts.astype(jnp.int32)
    padded = (counts + MOE_TILE - 1) // MOE_TILE * MOE_TILE
    padded_ends = jnp.cumsum(padded)
    padded_starts = padded_ends - padded
    onehot = idx[..., None] == jnp.arange(N_EXPERTS, dtype=jnp.int32)
    slot_of = rank + jnp.sum(jnp.where(onehot, padded_starts, 0), axis=-1)
    n_blocks = -(-n_tok * MOE_TOP_K // MOE_TILE) + N_EXPERTS
    block_row = jnp.arange(n_blocks, dtype=jnp.int32) * MOE_TILE
    block_e = jnp.minimum(jnp.sum(padded_ends[None, :] <= block_row[:, None], axis=1), N_EXPERTS - 1)
    n_used = padded_ends[-1:] // MOE_TILE
    last_block = jnp.concatenate([jnp.where(counts > 0, padded_ends - MOE_TILE, -1), n_used])
    slot_tiles = slot_of.astype(jnp.int32).reshape(n_tok // TOKEN_TILE, 1, TOKEN_TILE * MOE_TOP_K)
    return slot_tiles, block_e.astype(jnp.int32), n_used.astype(jnp.int32), last_block.astype(jnp.int32), n_blocks


def _dispatch_kernel(slot_ref, last_ref, h_ref, xs_ref, zero_ref, sem, *, parts):
    tm = h_ref.shape[0] // parts
    block = MOE_TILE * parts

    @pl.when(pl.program_id(0) == 0)
    def _():
        zero_ref[...] = jnp.zeros_like(zero_ref)

        def zero_copy(e):
            row = pl.multiple_of(jnp.maximum(last_ref[e], 0) * parts, block)
            return pltpu.make_async_copy(zero_ref, xs_ref.at[pl.ds(row, block)], sem)

        def block_copy(i):
            return pltpu.make_async_copy(zero_ref, xs_ref.at[pl.ds(pl.multiple_of(i * block, block), block)], sem)

        n_used = last_ref[N_EXPERTS]
        n_blocks = xs_ref.shape[0] // block
        lax.fori_loop(n_used, n_blocks, lambda i, c: (block_copy(i).start(), c)[1], 0)
        for e in range(N_EXPERTS):
            pl.when(last_ref[e] >= 0)(lambda e=e: zero_copy(e).start())
        for e in range(N_EXPERTS):
            pl.when(last_ref[e] >= 0)(lambda e=e: zero_copy(e).wait())
        lax.fori_loop(n_used, n_blocks, lambda i, c: (block_copy(i).wait(), c)[1], 0)

    def issue(t, carry):
        for k in range(MOE_TOP_K):
            s = slot_ref[0, 0, t * MOE_TOP_K + k]
            src = h_ref.at[pl.ds(pl.multiple_of(t * parts, parts), parts)]
            dst = xs_ref.at[pl.ds(pl.multiple_of(s * parts, parts), parts)]
            pltpu.make_async_copy(src, dst, sem).start(priority=k % 2)
        return carry

    lax.fori_loop(0, tm, issue, 0, unroll=DMA_ISSUE_UNROLL)
    for k in range(MOE_TOP_K):
        pltpu.make_async_copy(h_ref, xs_ref.at[pl.ds(0, tm * parts)], sem).wait()


def moe_dispatch(h, slot_tiles, last_block, n_blocks, n_tok):
    parts = h.shape[0] // n_tok
    tm = TOKEN_TILE
    return pl.pallas_call(
        functools.partial(_dispatch_kernel, parts=parts),
        grid=(n_tok // tm,),
        in_specs=[pl.BlockSpec((1, 1, tm * MOE_TOP_K), lambda i: (i, 0, 0), memory_space=pltpu.SMEM),
                  pl.BlockSpec(memory_space=pltpu.SMEM),
                  pl.BlockSpec((tm * parts, LANES), lambda i: (i, 0))],
        out_specs=pl.BlockSpec(memory_space=pl.ANY),
        out_shape=jax.ShapeDtypeStruct((n_blocks * MOE_TILE * parts, LANES), F32),
        scratch_shapes=[pltpu.VMEM((MOE_TILE * parts, LANES), F32), pltpu.SemaphoreType.DMA(())],
        compiler_params=_params(("arbitrary",)),
        name="moe_dispatch",
    )(slot_tiles, last_block, h)


def _combine_kernel(slot_ref, next_slot_ref, x_ref, gate_ref, mod_ref, *rest, final):
    ys_ref, o_ref, ybuf_ref, sems = rest[-4:]
    tm, d = x_ref.shape[1:]
    parts = d // LANES
    g = pl.program_id(0) * pl.num_programs(1) + pl.program_id(1)
    n_steps = pl.num_programs(0) * pl.num_programs(1)

    def fetch(slots, buf):
        def issue(t, carry):
            for k in range(MOE_TOP_K):
                s = slots[0, 0, t * MOE_TOP_K + k]
                src = ys_ref.at[pl.ds(pl.multiple_of(s * parts, parts), parts)]
                dst = ybuf_ref.at[buf * MOE_TOP_K + k, pl.ds(pl.multiple_of(t * parts, parts), parts)]
                pltpu.make_async_copy(src, dst, sems.at[buf]).start(priority=k % 2)
            return carry

        lax.fori_loop(0, tm, issue, 0, unroll=DMA_ISSUE_UNROLL)

    def combine(buf):
        for k in range(MOE_TOP_K):
            pltpu.make_async_copy(ys_ref.at[pl.ds(0, tm * parts)], ybuf_ref.at[buf * MOE_TOP_K + k], sems.at[buf]).wait()
        gate = gate_ref[0]
        y = None
        for k in range(MOE_TOP_K):
            term = gate[:, k:k + 1] * _load_row_tiled(ybuf_ref, tm, d, (buf * MOE_TOP_K + k,))
            y = term if y is None else y + term
        x = x_ref[0] + mod_ref[0, 5:6, :] * y
        if final:
            fw_ref = rest[0]
            x = x * lax.rsqrt(jnp.mean(x * x, axis=-1, keepdims=True) + RMS_EPS) * fw_ref[...]
        o_ref[0] = x

    pl.when(g == 0)(lambda: fetch(slot_ref, 0))
    for buf in range(2):
        pl.when((g + 1 < n_steps) & ((g + 1) % 2 == buf))(lambda buf=buf: fetch(next_slot_ref, buf))
    for buf in range(2):
        pl.when(g % 2 == buf)(lambda buf=buf: combine(buf))


def moe_combine(xc, ys, slot_tiles, gates, mod, n_lat, final_w=None):
    bsz, nt, d = xc.shape
    tm = TOKEN_TILE
    tiles = nt // tm
    last = bsz * tiles - 1
    slot_spec = lambda ahead: pl.BlockSpec((1, 1, tm * MOE_TOP_K),
                                           lambda b, j: (jnp.minimum(b * tiles + j + ahead, last), 0, 0),
                                           memory_space=pltpu.SMEM)
    in_specs = [slot_spec(0), slot_spec(1),
                pl.BlockSpec((1, tm, d), lambda b, j: (b, j, 0)),
                pl.BlockSpec((1, tm, LANES), lambda b, j: (b, j, 0)),
                pl.BlockSpec((1, SUBLANES, d), _mod_index(n_lat // tm, bsz))]
    args = [slot_tiles, slot_tiles, xc, gates, mod]
    if final_w is not None:
        in_specs.append(pl.BlockSpec((1, d), lambda b, j: (0, 0)))
        args.append(final_w.reshape(1, d))
    in_specs.append(pl.BlockSpec(memory_space=pl.ANY))
    args.append(ys)
    return pl.pallas_call(
        functools.partial(_combine_kernel, final=final_w is not None),
        grid=(bsz, tiles),
        in_specs=in_specs,
        out_specs=pl.BlockSpec((1, tm, d), lambda b, j: (b, j, 0)),
        out_shape=jax.ShapeDtypeStruct((bsz, nt, d), F32),
        scratch_shapes=[pltpu.VMEM((2 * MOE_TOP_K, tm * (d // LANES), LANES), F32), pltpu.SemaphoreType.DMA((2,))],
        compiler_params=_params(("arbitrary", "arbitrary")),
        name="moe_combine",
    )(*args)


def moe_layer(xc, mod, norm_w, w_router, b_router, experts, n_lat, final_w=None):
    bsz, nt, d = xc.shape
    n_tok = bsz * nt
    h, idx, gates, rank, counts = router(xc, mod, norm_w, w_router, b_router, n_lat)
    top = lambda a: a.reshape(n_tok, LANES)[:, :MOE_TOP_K]
    slot_tiles, block_e, n_used, last_block, n_blocks = _moe_plan(top(idx), top(rank), counts[0, :N_EXPERTS], n_tok)
    xs = moe_dispatch(h, slot_tiles, last_block, n_blocks, n_tok)
    ys = moe_experts(xs, block_e, n_used, *experts)
    return moe_combine(xc, ys, slot_tiles, gates, mod, n_lat, final_w)


def kernel(x, c, ctx, c_ctx, ada_w, ada_b, norm_w, final_norm_w, ev_w_in, ev_w_out, s5_lam_re, s5_lam_im, s5_log_step, s5_b_re, s5_b_im, s5_c_re, s5_c_im, s5_d, s5_w_glu, ret_log_decay, na_w_qkv, na_w_o, na_rpb, moe_w_router, moe_b_router, moe_w1, moe_b1, moe_w2, moe_b2):
    bsz, n_lat, d = x.shape
    n_ctx = ctx.shape[1]
    depth = ada_w.shape[0]
    mod = adaln_table(c, c_ctx, ada_w, ada_b)
    xc = jnp.concatenate([x, ctx], axis=1)
    for i in range(depth):
        last = i == depth - 1
        j = i // 2
        if i % 2 == 0:
            proj = norm_proj(xc, mod[i], norm_w[i, 0], ev_w_in[j], n_lat, 0)
            s5p = _s5_layout(s5_lam_re[j], s5_lam_im[j], s5_log_step[j], s5_b_re[j], s5_b_im[j],
                             s5_c_re[j], s5_c_im[j])
            y_s5 = s5_scan(proj, s5p, n_lat)
            o_ret = retention(proj, ret_log_decay[j], n_lat)
            xc = mix_out(xc, proj, y_s5, o_ret, mod[i], s5_d[j], s5_w_glu[j], ev_w_out[j], n_lat)
        else:
            qkv = norm_proj(xc, mod[i], norm_w[i, 0], na_w_qkv[j], n_lat, 0)
            att = na_attention(qkv, na_rpb[j], n_lat)
            assert last, "an odd layer is only supported as the final layer (no context output needed)"
            xc = proj_residual(att, xc, mod[i], na_w_o[j])
        experts = (moe_w1, moe_b1, moe_w2, moe_b2, i)
        if last:
            return moe_layer(xc[:, :n_lat], mod[i], norm_w[i, 1], moe_w_router[i], moe_b_router[i],
                             experts, n_lat, final_norm_w)
        xc = moe_layer(xc, mod[i], norm_w[i, 1], moe_w_router[i], moe_b_router[i], experts, n_lat)
```

```python
import functools
import math

import numpy as np
import jax
import jax.numpy as jnp
from jax import lax
from jax.experimental import pallas as pl
from jax.experimental.pallas import tpu as pltpu

F32 = jnp.float32
BF16 = jnp.bfloat16

GRID_W = 64
RMS_EPS = 1e-6
S5_WIDTH = 512
S5_GROUP = 16
S5_GROUPS = S5_WIDTH // S5_GROUP
S5_STATE = 64
RET_HEADS = 4
RET_HEAD_DIM = 128
RET_WIDTH = RET_HEADS * RET_HEAD_DIM
ROPE_BASE = 10000.0
NA_HEADS = 16
NA_HEAD_DIM = 64
NA_WIDTH = NA_HEADS * NA_HEAD_DIM
NA_WIN_ROWS = 8
NA_WIN_COLS = 16
N_EXPERTS = 32
MOE_TOP_K = 4
SWIGLU_LIMIT = 7.0
SWIGLU_ALPHA = 1.702

LANES = 128
SUBLANES = 8
MXU_DIM = 256
V7X_VMEM_BYTES = 64 * 1024 * 1024
VMEM_LIMIT = V7X_VMEM_BYTES * 7 // 8

TOKEN_TILE = 256
S5_CHUNK = 256
RET_CHUNK = 256
MOE_TILE = 512
DMA_ISSUE_UNROLL = 8
NA_Q_ROWS = 4
NA_BAND_ROWS = 12
NA_SOFTMAX_ROWS = 16
NA_SCORE_SLOTS = 4
assert NA_BAND_ROWS >= NA_WIN_ROWS + NA_Q_ROWS - 1 and (NA_BAND_ROWS - NA_Q_ROWS) % NA_Q_ROWS == 0
MASK_VALUE = -1e30

S5_SLICES = 8
assert S5_SLICES * MXU_DIM == S5_GROUPS * S5_STATE


def _params(sem):
    return pltpu.CompilerParams(dimension_semantics=sem, vmem_limit_bytes=VMEM_LIMIT)


def _adaln_kernel(c_ref, w_ref, b_ref, o_ref):
    c = c_ref[...]
    s = c * jax.nn.sigmoid(c)
    o_ref[0] = jnp.dot(s, w_ref[0], preferred_element_type=F32,
                       precision=lax.Precision.HIGHEST) + b_ref[0]


def adaln_table(c, c_ctx, ada_w, ada_b):
    depth, d, d6 = ada_w.shape
    bsz = c.shape[0]
    cond = jnp.concatenate([c, c_ctx[None, :]], axis=0)
    cond = jnp.pad(cond, ((0, SUBLANES - (bsz + 1)), (0, 0)))
    tn = d6 // 4
    out = pl.pallas_call(
        _adaln_kernel,
        grid=(depth, d6 // tn),
        in_specs=[pl.BlockSpec((SUBLANES, d), lambda i, j: (0, 0)),
                  pl.BlockSpec((1, d, tn), lambda i, j: (i, 0, j)),
                  pl.BlockSpec((1, 1, tn), lambda i, j: (i, 0, j))],
        out_specs=pl.BlockSpec((1, SUBLANES, tn), lambda i, j: (i, 0, j)),
        out_shape=jax.ShapeDtypeStruct((depth, SUBLANES, d6), F32),
        compiler_params=_params(("arbitrary", "arbitrary")),
        name="adaln",
    )(cond, ada_w, ada_b.reshape(depth, 1, d6))
    tab = out[:, :bsz + 1].reshape(depth, bsz + 1, 6, d)
    return jnp.pad(tab, ((0, 0), (0, 0), (0, 2), (0, 0)))


def _norm_mod(x, nw, shift, scale):
    y = x * lax.rsqrt(jnp.mean(x * x, axis=-1, keepdims=True) + RMS_EPS)
    return (y * nw) * (1 + scale) + shift


def _mod_index(n_lat_tiles, bsz):
    def index(b, j):
        return (jnp.where(j >= n_lat_tiles, bsz, b), 0, 0)
    return index


def _proj_kernel(x_ref, mod_ref, nw_ref, w_ref, o_ref, *, shift_row):
    h = _norm_mod(x_ref[0], nw_ref[...], mod_ref[0, shift_row:shift_row + 1, :],
                  mod_ref[0, shift_row + 1:shift_row + 2, :])
    o_ref[0] = jnp.dot(h.astype(BF16), w_ref[...], preferred_element_type=F32).astype(o_ref.dtype)


def norm_proj(xc, mod, norm_w, w, n_lat, shift_row):
    bsz, nt, d = xc.shape
    n = w.shape[1]
    tm = TOKEN_TILE
    return pl.pallas_call(
        functools.partial(_proj_kernel, shift_row=shift_row),
        grid=(bsz, nt // tm),
        in_specs=[pl.BlockSpec((1, tm, d), lambda b, j: (b, j, 0)),
                  pl.BlockSpec((1, SUBLANES, d), _mod_index(n_lat // tm, bsz)),
                  pl.BlockSpec((1, d), lambda b, j: (0, 0)),
                  pl.BlockSpec((d, n), lambda b, j: (0, 0))],
        out_specs=pl.BlockSpec((1, tm, n), lambda b, j: (b, j, 0)),
        out_shape=jax.ShapeDtypeStruct((bsz, nt, n), BF16),
        compiler_params=_params(("arbitrary", "arbitrary")),
        name="norm_proj",
    )(xc, mod, norm_w.reshape(1, d), w.astype(BF16))


def _s5_discretize(lam_re, lam_im, log_step, b_re, b_im):
    lam_re = jnp.minimum(lam_re.astype(F32), -1e-4)
    lam_im = lam_im.astype(F32)
    step = jnp.exp(log_step.astype(F32))[..., None]
    mag = jnp.exp(lam_re * step)
    ang = lam_im * step
    a_re, a_im = mag * jnp.cos(ang), mag * jnp.sin(ang)
    den = lam_re * lam_re + lam_im * lam_im
    n_re, n_im = a_re - 1.0, a_im
    co_re = (n_re * lam_re + n_im * lam_im) / den
    co_im = (n_im * lam_re - n_re * lam_im) / den
    b_re, b_im = b_re.astype(F32), b_im.astype(F32)
    bb_re = co_re[..., None] * b_re - co_im[..., None] * b_im
    bb_im = co_re[..., None] * b_im + co_im[..., None] * b_re
    return a_re, a_im, bb_re, bb_im


def _s5_layout(lam_re, lam_im, log_step, b_re, b_im, c_re, c_im):
    a_re, a_im, bb_re, bb_im = _s5_discretize(lam_re, lam_im, log_step, b_re, b_im)
    gh = S5_GROUPS // 2
    nq = S5_STATE // S5_GROUP
    eye = jnp.eye(gh, dtype=F32)

    def arrange_a(a):
        a = a.reshape(2, 2, gh, nq, S5_GROUP)
        return a.transpose(0, 1, 3, 2, 4).reshape(2, S5_SLICES, MXU_DIM)

    def arrange_b(bb):
        bb = bb.reshape(2, 2, gh, nq, S5_GROUP, S5_GROUP)
        m = jnp.einsum('dhgqni,Gg->dhqGign', bb, eye)
        return m.reshape(2, S5_SLICES, MXU_DIM, MXU_DIM).astype(BF16)

    def arrange_c(c):
        c = c.astype(F32).reshape(2, 2, gh, S5_GROUP, nq, S5_GROUP)
        m = jnp.einsum('dhgiqn,gG->dhqgnGi', c, eye)
        return m.reshape(2, S5_SLICES, MXU_DIM, MXU_DIM).astype(BF16)

    return (arrange_a(a_re), arrange_a(a_im), arrange_b(bb_re), arrange_b(bb_im),
            arrange_c(c_re), arrange_c(-c_im.astype(F32)))


def _s5_kernel(u_ref, bre_ref, bim_ref, cre_ref, cim_ref, are_ref, aim_ref, y_ref, bu_ref, st_ref, h_ref, *, nb):
    t_chunk = S5_CHUNK
    half = MXU_DIM // LANES
    d = pl.program_id(0)

    @pl.when(pl.program_id(1) == 0)
    def _():
        h_ref[...] = jnp.zeros_like(h_ref)

    def slab(b, c, lh):
        return (b * 2 + c) * half + lh

    for b in range(nb):
        for h in range(2):
            ub = u_ref[b, :, h * MXU_DIM:(h + 1) * MXU_DIM]
            for q in range(S5_SLICES // 2):
                k = h * (S5_SLICES // 2) + q
                for c, w_ref in ((0, bre_ref), (1, bim_ref)):
                    r = jnp.dot(ub, w_ref[0, k], preferred_element_type=F32)
                    for lh in range(half):
                        bu_ref[slab(b, c, lh), pl.ds(k, t_chunk, stride=S5_SLICES), :] = r[:, lh * LANES:(lh + 1) * LANES]

    ar = [are_ref[0, :, lh * LANES:(lh + 1) * LANES] for lh in range(half)]
    ai = [aim_ref[0, :, lh * LANES:(lh + 1) * LANES] for lh in range(half)]

    def step(t, carry):
        tt = jnp.where(d == 0, t, t_chunk - 1 - t)
        new = []
        for b in range(nb):
            for lh in range(half):
                hr, hi = carry[2 * (b * half + lh)], carry[2 * (b * half + lh) + 1]
                rows = pl.ds(pl.multiple_of(tt * S5_SLICES, S5_SLICES), S5_SLICES)
                xr = bu_ref[slab(b, 0, lh), rows, :]
                xi = bu_ref[slab(b, 1, lh), rows, :]
                nr = ar[lh] * hr - ai[lh] * hi + xr
                ni = ar[lh] * hi + ai[lh] * hr + xi
                st_ref[slab(b, 0, lh), rows, :] = nr
                st_ref[slab(b, 1, lh), rows, :] = ni
                new += [nr, ni]
        return tuple(new)

    n_state = nb * half * 2
    carry = lax.fori_loop(0, t_chunk, step, tuple(h_ref[j] for j in range(n_state)), unroll=8)
    for j in range(n_state):
        h_ref[j] = carry[j]

    for b in range(nb):
        for h in range(2):
            acc = None
            for q in range(S5_SLICES // 2):
                k = h * (S5_SLICES // 2) + q
                for c, w_ref in ((0, cre_ref), (1, cim_ref)):
                    s = jnp.concatenate(
                        [st_ref[slab(b, c, lh), pl.ds(k, t_chunk, stride=S5_SLICES), :] for lh in range(half)], axis=1)
                    term = jnp.dot(s.astype(BF16), w_ref[0, k], preferred_element_type=F32)
                    acc = term if acc is None else acc + term
            y_ref[0, b, :, h * MXU_DIM:(h + 1) * MXU_DIM] = acc.astype(y_ref.dtype)


def _scan_chunk_index(n_chunks, n_ctx_chunks):
    def chunk(d, i):
        fwd = lax.rem(i + (n_chunks - n_ctx_chunks), n_chunks)
        return jnp.where(d == 0, fwd, n_chunks - 1 - i)
    return chunk


def s5_scan(proj, s5p, n_lat):
    bsz, nt, _ = proj.shape
    a_re, a_im, bb_re, bb_im, cc_re, cc_im = s5p
    t = S5_CHUNK
    nc = nt // t
    chunk = _scan_chunk_index(nc, (nt - n_lat) // t)
    wspec = pl.BlockSpec((1, S5_SLICES, MXU_DIM, MXU_DIM), lambda d, i: (d, 0, 0, 0))
    aspec = pl.BlockSpec((1, S5_SLICES, MXU_DIM), lambda d, i: (d, 0, 0))
    n_slab = bsz * 2 * (MXU_DIM // LANES)
    return pl.pallas_call(
        functools.partial(_s5_kernel, nb=bsz),
        grid=(2, nc),
        in_specs=[pl.BlockSpec((bsz, t, S5_WIDTH), lambda d, i: (0, chunk(d, i), 0)),
                  wspec, wspec, wspec, wspec, aspec, aspec],
        out_specs=pl.BlockSpec((1, bsz, t, S5_WIDTH), lambda d, i: (d, 0, chunk(d, i), 0)),
        out_shape=jax.ShapeDtypeStruct((2, bsz, nt, S5_WIDTH), BF16),
        scratch_shapes=[pltpu.VMEM((n_slab, S5_SLICES * S5_CHUNK, LANES), F32),
                        pltpu.VMEM((n_slab, S5_SLICES * S5_CHUNK, LANES), F32),
                        pltpu.VMEM((n_slab, S5_SLICES, LANES), F32)],
        compiler_params=_params(("arbitrary", "arbitrary")),
        name="s5_scan",
    )(proj, bb_re, bb_im, cc_re, cc_im, a_re, a_im)


def _rope_tables(n_lat, n_ctx):
    half = RET_HEAD_DIM // 4
    freq = ROPE_BASE ** (-jnp.arange(half, dtype=F32) / half)
    rows = n_lat // GRID_W
    ang_r = jnp.arange(rows, dtype=F32)[:, None] * freq[None, :]
    ang_c = jnp.arange(GRID_W, dtype=F32)[:, None] * freq[None, :]

    def table(fn):
        by_row = jnp.broadcast_to(fn(ang_r)[:, None, :], (rows, GRID_W, half))
        by_col = jnp.broadcast_to(fn(ang_c)[None, :, :], (rows, GRID_W, half))
        return jnp.concatenate([by_row, by_row, by_col, by_col], axis=-1).reshape(n_lat, RET_HEAD_DIM)

    cos, sin = table(jnp.cos), table(jnp.sin)
    cos = jnp.concatenate([cos, jnp.ones((n_ctx, RET_HEAD_DIM), F32)], axis=0)
    sin = jnp.concatenate([sin, jnp.zeros((n_ctx, RET_HEAD_DIM), F32)], axis=0)
    return cos, sin


def _rope_rotation_matrix():
    blk = RET_HEAD_DIM // 2
    half = blk // 2
    r = np.zeros((RET_HEAD_DIM, RET_HEAD_DIM), np.float32)
    for base in (0, blk):
        for l in range(half):
            r[base + l + half, base + l] = -1.0
            r[base + l, base + l + half] = 1.0
    return jnp.asarray(r, BF16)


def _ret_decay(log_decay, t):
    lg = log_decay.astype(F32)
    scale = RET_HEAD_DIM ** -0.5
    idx = jnp.arange(t, dtype=F32)
    diff = idx[:, None] - idx[None, :]
    diff = jnp.stack([diff, -diff])[:, None]
    inner = jnp.where(diff >= 0, jnp.exp(lg[:, :, None, None] * jnp.maximum(diff, 0.0)), 0.0) * scale
    pos = jnp.stack([idx, t - 1.0 - idx])
    q_dec = jnp.exp(lg[:, :, None] * (pos[:, None, :] + 1.0))
    k_dec = jnp.exp(lg[:, :, None] * (t - 1.0 - pos[:, None, :])) * scale
    blk = jnp.exp(lg * t)
    bcast = lambda v: jnp.broadcast_to(v[..., None], v.shape + (RET_HEAD_DIM,))
    return inner, bcast(q_dec), bcast(k_dec), jnp.broadcast_to(blk[:, :, None, None], (2, RET_HEADS, SUBLANES, RET_HEAD_DIM))


def _ret_kernel(q_ref, k_ref, v_ref, cos_ref, sin_ref, rot_ref, inner_ref, qd_ref, kd_ref, bd_ref, o_ref, s_ref):
    @pl.when(pl.program_id(1) == 0)
    def _():
        s_ref[...] = jnp.zeros_like(s_ref)

    cos = cos_ref[...]
    sin = sin_ref[...]
    rot = rot_ref[...]
    nt_dims = (((1,), (1,)), ((), ()))
    for b in range(q_ref.shape[0]):
        for hd in range(RET_HEADS):
            sl = slice(hd * RET_HEAD_DIM, (hd + 1) * RET_HEAD_DIM)
            q = q_ref[b, :, sl]
            k = k_ref[b, :, sl]
            v = v_ref[b, :, sl]
            qr = q.astype(F32) * cos + jnp.dot(q, rot, preferred_element_type=F32) * sin
            kr = k.astype(F32) * cos + jnp.dot(k, rot, preferred_element_type=F32) * sin
            qb = qr.astype(BF16)
            att = lax.dot_general(qb, kr.astype(BF16), nt_dims, preferred_element_type=F32) * inner_ref[0, hd]
            s = s_ref[b * RET_HEADS + hd]
            o = jnp.dot(att.astype(BF16), v, preferred_element_type=F32)
            o = o + jnp.dot(qb, s.astype(BF16), preferred_element_type=F32) * qd_ref[0, hd]
            kd_t = (kr * kd_ref[0, hd]).T.astype(BF16)
            s_ref[b * RET_HEADS + hd] = bd_ref[0, hd, 0:1, :] * s + jnp.dot(kd_t, v, preferred_element_type=F32)
            o_ref[0, b, :, sl] = o.astype(o_ref.dtype)


def retention(proj, log_decay, n_lat):
    bsz, nt, _ = proj.shape
    t = RET_CHUNK
    nc = nt // t
    chunk = _scan_chunk_index(nc, (nt - n_lat) // t)
    cos, sin = _rope_tables(n_lat, nt - n_lat)
    inner, q_dec, k_dec, blk = _ret_decay(log_decay, t)
    w = RET_WIDTH

    def col(j):
        return pl.BlockSpec((bsz, t, w), lambda d, i: (0, chunk(d, i), j))

    tab = pl.BlockSpec((t, RET_HEAD_DIM), lambda d, i: (chunk(d, i), 0))
    dec = pl.BlockSpec((1, RET_HEADS, t, RET_HEAD_DIM), lambda d, i: (d, 0, 0, 0))
    return pl.pallas_call(
        _ret_kernel,
        grid=(2, nc),
        in_specs=[col(1), col(2), col(3), tab, tab,
                  pl.BlockSpec((RET_HEAD_DIM, RET_HEAD_DIM), lambda d, i: (0, 0)),
                  pl.BlockSpec((1, RET_HEADS, t, t), lambda d, i: (d, 0, 0, 0)),
                  dec, dec,
                  pl.BlockSpec((1, RET_HEADS, SUBLANES, RET_HEAD_DIM), lambda d, i: (d, 0, 0, 0))],
        out_specs=pl.BlockSpec((1, bsz, t, w), lambda d, i: (d, 0, chunk(d, i), 0)),
        out_shape=jax.ShapeDtypeStruct((2, bsz, nt, w), BF16),
        scratch_shapes=[pltpu.VMEM((bsz * RET_HEADS, RET_HEAD_DIM, RET_HEAD_DIM), F32)],
        compiler_params=_params(("arbitrary", "arbitrary")),
        name="retention",
    )(proj, proj, proj, cos, sin, _rope_rotation_matrix(), inner, q_dec, k_dec, blk)


def _mix_out_kernel(u_ref, g_ref, ys_ref, or_ref, x_ref, mod_ref, d_ref, wglu_ref, wo_ref, o_ref):
    y = u_ref[0].astype(F32) * d_ref[...] + ys_ref[0, 0].astype(F32) + ys_ref[1, 0].astype(F32)
    y = jax.nn.gelu(y)
    a = y * jax.nn.sigmoid(jnp.dot(y.astype(BF16), wglu_ref[...], preferred_element_type=F32))
    o = or_ref[0, 0].astype(F32) + or_ref[1, 0].astype(F32)
    heads = []
    for hd in range(RET_HEADS):
        oh = o[:, hd * RET_HEAD_DIM:(hd + 1) * RET_HEAD_DIM]
        heads.append(oh * lax.rsqrt(jnp.mean(oh * oh, axis=-1, keepdims=True) + RMS_EPS))
    r = jnp.concatenate(heads, axis=1) * jax.nn.silu(g_ref[0].astype(F32))
    m = jnp.dot(a.astype(BF16), wo_ref[:S5_WIDTH, :], preferred_element_type=F32)
    m = m + jnp.dot(r.astype(BF16), wo_ref[S5_WIDTH:, :], preferred_element_type=F32)
    o_ref[0] = x_ref[0] + mod_ref[0, 2:3, :] * m


def mix_out(xc, proj, y_s5, o_ret, mod, d_skip, w_glu, w_out, n_lat):
    bsz, nt, d = xc.shape
    tm = TOKEN_TILE
    w = S5_WIDTH
    dirs = pl.BlockSpec((2, 1, tm, w), lambda b, j: (0, b, j, 0))
    return pl.pallas_call(
        _mix_out_kernel,
        grid=(bsz, nt // tm),
        in_specs=[pl.BlockSpec((1, tm, w), lambda b, j: (b, j, 0)),
                  pl.BlockSpec((1, tm, w), lambda b, j: (b, j, 4)),
                  dirs, dirs,
                  pl.BlockSpec((1, tm, d), lambda b, j: (b, j, 0)),
                  pl.BlockSpec((1, SUBLANES, d), _mod_index(n_lat // tm, bsz)),
                  pl.BlockSpec((1, w), lambda b, j: (0, 0)),
                  pl.BlockSpec((w, w), lambda b, j: (0, 0)),
                  pl.BlockSpec((w + RET_WIDTH, d), lambda b, j: (0, 0))],
        out_specs=pl.BlockSpec((1, tm, d), lambda b, j: (b, j, 0)),
        out_shape=jax.ShapeDtypeStruct((bsz, nt, d), F32),
        compiler_params=_params(("arbitrary", "arbitrary")),
        name="mix_out",
    )(proj, proj, y_s5, o_ret, xc, mod, d_skip.reshape(1, w), w_glu.astype(BF16), w_out.astype(BF16))


def _na_band_start(r0, rows):
    return jnp.clip(r0 - NA_WIN_ROWS // 2, 0, rows - NA_BAND_ROWS)


def _na_bias(rpb, rows):
    w = GRID_W
    n_var = (NA_BAND_ROWS - NA_Q_ROWS) // NA_Q_ROWS + 1
    half = NA_WIN_ROWS // 2
    r0 = np.array([v * NA_Q_ROWS if v * NA_Q_ROWS <= half else rows - NA_BAND_ROWS + v * NA_Q_ROWS
                   for v in range(n_var)])
    bs = np.clip(r0 - half, 0, rows - NA_BAND_ROWS)
    assert list(r0 - bs) == [v * NA_Q_ROWS for v in range(n_var)]
    r = r0[:, None] + np.arange(NA_Q_ROWS)[None, :]
    rs = np.clip(r - half, 0, rows - NA_WIN_ROWS)
    a = bs[:, None] + np.arange(NA_BAND_ROWS)[None, :]
    row_ok = (a[:, None, :] >= rs[:, :, None]) & (a[:, None, :] < rs[:, :, None] + NA_WIN_ROWS)
    row_off = np.clip(a[:, None, :] - r[:, :, None] + (NA_WIN_ROWS - 1), 0, 2 * NA_WIN_ROWS - 2)
    col = np.arange(w)
    col_start = np.clip(col - NA_WIN_COLS // 2, 0, w - NA_WIN_COLS)
    col_ok = (col[None, :] >= col_start[:, None]) & (col[None, :] < col_start[:, None] + NA_WIN_COLS)
    col_off = col[None, :] - col[:, None] + (NA_WIN_COLS - 1)
    col_sel = (col_off[:, :, None] == np.arange(2 * NA_WIN_COLS - 1)) & col_ok[:, :, None]
    tiles = jnp.sum(rpb.astype(F32)[:, :, None, None, :] * jnp.asarray(col_sel, F32)[None, None], axis=-1)
    tiles = jnp.where(jnp.asarray(col_ok)[None, None], tiles, MASK_VALUE)
    masked = jnp.full((NA_HEADS, w, w), MASK_VALUE, F32)
    groups = []
    for v in range(n_var):
        per_row = []
        for rr in range(NA_Q_ROWS):
            per_row.append(jnp.concatenate(
                [tiles[:, int(row_off[v, rr, i])] if row_ok[v, rr, i] else masked for i in range(NA_BAND_ROWS)],
                axis=-1))
        groups.append(jnp.concatenate(per_row, axis=1))
    return jnp.stack(groups)


def _na_kernel(q_ref, kb_ref, vb_ref, kc_ref, vc_ref, bias_ref, o_ref, s_ref, p_ref):
    scale = NA_HEAD_DIM ** -0.5
    nt_dims = (((1,), (1,)), ((), ()))
    nq = q_ref.shape[1]
    n_ctx = kc_ref.shape[1]
    n_slot = s_ref.shape[0]
    heads_per_tile = LANES // NA_HEAD_DIM
    lane = lax.broadcasted_iota(jnp.int32, (nq, LANES), 1)
    for j in range(NA_WIDTH // LANES):
        sl = slice(j * LANES, (j + 1) * LANES)
        q2 = q_ref[0, :, sl].astype(F32) * scale
        k2 = kb_ref[0, :, sl]
        v2 = vb_ref[0, :, sl]
        kc2 = kc_ref[0, :, sl]
        vc2 = vc_ref[0, :, sl]
        slot = j % n_slot
        qm = jnp.concatenate(
            [jnp.where((lane >= hh * NA_HEAD_DIM) & (lane < (hh + 1) * NA_HEAD_DIM), q2, 0.0).astype(BF16)
             for hh in range(heads_per_tile)], axis=0)
        s_ref[slot, :, :n_ctx] = lax.dot_general(qm, kc2, nt_dims, preferred_element_type=F32)
        s_loc = lax.dot_general(qm, k2, nt_dims, preferred_element_type=F32)
        for hh in range(heads_per_tile):
            rows = slice(hh * nq, (hh + 1) * nq)
            s_ref[slot, rows, n_ctx:] = s_loc[rows] + bias_ref[0, j * heads_per_tile + hh]
        inv = []
        for c in range(heads_per_tile * nq // NA_SOFTMAX_ROWS):
            rows = slice(c * NA_SOFTMAX_ROWS, (c + 1) * NA_SOFTMAX_ROWS)
            s = s_ref[slot, rows, :]
            p = jnp.exp(s - jnp.max(s, axis=-1, keepdims=True))
            inv.append(1.0 / jnp.sum(p, axis=-1, keepdims=True))
            p_ref[slot, rows, :] = p.astype(BF16)
        o = jnp.dot(p_ref[slot, :, :n_ctx], vc2, preferred_element_type=F32)
        o = o + jnp.dot(p_ref[slot, :, n_ctx:], v2, preferred_element_type=F32)
        o = o * jnp.concatenate(inv, axis=0)
        o2 = o[:nq]
        for hh in range(1, heads_per_tile):
            o2 = jnp.where(lane >= hh * NA_HEAD_DIM, o[hh * nq:(hh + 1) * nq], o2)
        o_ref[0, :, sl] = o2.astype(o_ref.dtype)


def na_attention(qkv, rpb, n_lat):
    bsz, nt, _ = qkv.shape
    w = GRID_W
    rows = n_lat // w
    n_ctx = nt - n_lat
    nq = NA_Q_ROWS * w
    band = NA_BAND_ROWS * w
    n_all = n_ctx + band
    stacked = nq * (LANES // NA_HEAD_DIM)
    assert n_ctx % LANES == 0 and band % LANES == 0 and rows % NA_Q_ROWS == 0 and rows >= NA_BAND_ROWS + NA_WIN_ROWS // 2

    def band_start(g):
        return _na_band_start(g * NA_Q_ROWS, rows)

    def band_spec(j):
        return pl.BlockSpec((pl.Element(1), pl.Element(band), pl.Element(NA_WIDTH)),
                            lambda b, g: (b, band_start(g) * w, j * NA_WIDTH))

    def ctx_spec(j):
        return pl.BlockSpec((1, n_ctx, NA_WIDTH), lambda b, g: (b, n_lat // n_ctx, j))

    return pl.pallas_call(
        _na_kernel,
        grid=(bsz, rows // NA_Q_ROWS),
        in_specs=[pl.BlockSpec((1, nq, NA_WIDTH), lambda b, g: (b, g, 0)),
                  band_spec(1), band_spec(2), ctx_spec(1), ctx_spec(2),
                  pl.BlockSpec((1, NA_HEADS, nq, band), lambda b, g: (g - band_start(g) // NA_Q_ROWS, 0, 0, 0))],
        out_specs=pl.BlockSpec((1, nq, NA_WIDTH), lambda b, g: (b, g, 0)),
        out_shape=jax.ShapeDtypeStruct((bsz, n_lat, NA_WIDTH), BF16),
        scratch_shapes=[pltpu.VMEM((NA_SCORE_SLOTS, stacked, n_all), F32),
                        pltpu.VMEM((NA_SCORE_SLOTS, stacked, n_all), BF16)],
        compiler_params=_params(("arbitrary", "arbitrary")),
        name="na_attention",
    )(qkv, qkv, qkv, qkv, qkv, _na_bias(rpb, rows))


def _proj_res_kernel(a_ref, x_ref, mod_ref, w_ref, o_ref):
    m = jnp.dot(a_ref[0], w_ref[...], preferred_element_type=F32)
    o_ref[0] = x_ref[0] + mod_ref[0, 2:3, :] * m


def proj_residual(a, x, mod, w):
    bsz, n, k = a.shape
    d = x.shape[-1]
    tm = TOKEN_TILE
    return pl.pallas_call(
        _proj_res_kernel,
        grid=(bsz, n // tm),
        in_specs=[pl.BlockSpec((1, tm, k), lambda b, j: (b, j, 0)),
                  pl.BlockSpec((1, tm, d), lambda b, j: (b, j, 0)),
                  pl.BlockSpec((1, SUBLANES, d), lambda b, j: (b, 0, 0)),
                  pl.BlockSpec((k, d), lambda b, j: (0, 0))],
        out_specs=pl.BlockSpec((1, tm, d), lambda b, j: (b, j, 0)),
        out_shape=jax.ShapeDtypeStruct((bsz, n, d), F32),
        compiler_params=_params(("arbitrary", "arbitrary")),
        name="proj_residual",
    )(a, x, mod, w.astype(BF16))


def _store_row_tiled(ref, value, index=()):
    rows, d = value.shape
    parts = d // LANES
    for j in range(parts):
        ref[index + (pl.ds(j, rows, stride=parts), slice(None))] = value[:, j * LANES:(j + 1) * LANES]


def _load_row_tiled(ref, rows, d, index=()):
    parts = d // LANES
    return jnp.concatenate([ref[index + (pl.ds(j, rows, stride=parts), slice(None))] for j in range(parts)], axis=1)


def _router_kernel(x_ref, mod_ref, nw_ref, wr_ref, br_ref, tri_ref, h_ref, idx_ref, gate_ref, rank_ref, cnt_ref,
                   run_ref):
    first_step = (pl.program_id(0) == 0) & (pl.program_id(1) == 0)

    @pl.when(first_step)
    def _():
        run_ref[...] = jnp.zeros_like(run_ref)

    h = _norm_mod(x_ref[0], nw_ref[...], mod_ref[0, 3:4, :], mod_ref[0, 4:5, :])
    _store_row_tiled(h_ref, h)
    logits = jnp.dot(h.astype(BF16), wr_ref[...], preferred_element_type=F32) + br_ref[...]
    lane = lax.broadcasted_iota(jnp.int32, logits.shape, 1)
    lane_f = lane.astype(F32)
    vals, idxs = [], []
    cur = logits
    for _ in range(MOE_TOP_K):
        m = jnp.max(cur, axis=-1, keepdims=True)
        first = jnp.min(jnp.where(cur == m, lane_f, float(LANES)), axis=-1, keepdims=True)
        vals.append(m)
        idxs.append(first)
        cur = jnp.where(lane_f == first, MASK_VALUE, cur)
    es = [jnp.exp(v - vals[0]) for v in vals]
    tot = es[0]
    for e in es[1:]:
        tot = tot + e
    onehots = [(lane_f == idxs[k]).astype(F32) for k in range(MOE_TOP_K)]
    multi = onehots[0]
    for oh in onehots[1:]:
        multi = multi + oh
    before = jnp.dot(tri_ref[...], multi.astype(BF16), preferred_element_type=F32) + run_ref[0:1, :]
    idx_out = jnp.zeros(logits.shape, F32)
    gate_out = jnp.zeros(logits.shape, F32)
    rank_out = jnp.zeros(logits.shape, F32)
    for k in range(MOE_TOP_K):
        idx_out = jnp.where(lane == k, idxs[k], idx_out)
        gate_out = jnp.where(lane == k, es[k] / tot, gate_out)
        rank_out = jnp.where(lane == k, jnp.sum(before * onehots[k], axis=-1, keepdims=True), rank_out)
    idx_ref[0] = idx_out.astype(jnp.int32)
    gate_ref[0] = gate_out
    rank_ref[0] = rank_out.astype(jnp.int32)
    run_ref[0:1, :] = run_ref[0:1, :] + jnp.sum(multi, axis=0, keepdims=True)
    cnt_ref[...] = jnp.broadcast_to(run_ref[0:1, :], cnt_ref.shape)


def router(xc, mod, norm_w, w_router, b_router, n_lat):
    bsz, nt, d = xc.shape
    tm = TOKEN_TILE
    tiles = nt // tm
    parts = d // LANES
    wr = jnp.pad(w_router, ((0, 0), (0, LANES - N_EXPERTS))).astype(BF16)
    br = jnp.pad(b_router.astype(F32), (0, LANES - N_EXPERTS), constant_values=MASK_VALUE).reshape(1, LANES)
    tri = jnp.tril(jnp.ones((tm, tm), BF16), k=-1)
    tok = lambda n, dt: jax.ShapeDtypeStruct((bsz, nt, n), dt)
    out = lambda n: pl.BlockSpec((1, tm, n), lambda b, j: (b, j, 0))
    return pl.pallas_call(
        _router_kernel,
        grid=(bsz, nt // tm),
        in_specs=[pl.BlockSpec((1, tm, d), lambda b, j: (b, j, 0)),
                  pl.BlockSpec((1, SUBLANES, d), _mod_index(n_lat // tm, bsz)),
                  pl.BlockSpec((1, d), lambda b, j: (0, 0)),
                  pl.BlockSpec((d, LANES), lambda b, j: (0, 0)),
                  pl.BlockSpec((1, LANES), lambda b, j: (0, 0)),
                  pl.BlockSpec((tm, tm), lambda b, j: (0, 0))],
        out_specs=[pl.BlockSpec((tm * parts, LANES), lambda b, j: (b * tiles + j, 0)),
                   out(LANES), out(LANES), out(LANES),
                   pl.BlockSpec((SUBLANES, LANES), lambda b, j: (0, 0))],
        out_shape=[jax.ShapeDtypeStruct((bsz * nt * parts, LANES), F32),
                   tok(LANES, jnp.int32), tok(LANES, F32), tok(LANES, jnp.int32),
                   jax.ShapeDtypeStruct((SUBLANES, LANES), F32)],
        scratch_shapes=[pltpu.VMEM((SUBLANES, LANES), F32)],
        compiler_params=_params(("arbitrary", "arbitrary")),
        name="router",
    )(xc, mod, norm_w.reshape(1, d), wr, br, tri)


def _moe_kernel(be_ref, nu_ref, x_ref, w1_ref, b1_ref, w2_ref, b2_ref, o_ref, w1b_ref, w2b_ref):
    i = pl.program_id(0)
    ff = w2_ref.shape[2]
    used = i < nu_ref[0]

    @pl.when(used & ((i == 0) | (be_ref[i] != be_ref[jnp.maximum(i - 1, 0)])))
    def _():
        w1b_ref[...] = w1_ref[0, 0].astype(BF16)
        w2b_ref[...] = w2_ref[0, 0].astype(BF16)

    @pl.when(used)
    def _():
        d = w1b_ref.shape[0]
        x = _load_row_tiled(x_ref, MOE_TILE, d)
        t = jnp.dot(x.astype(BF16), w1b_ref[...], preferred_element_type=F32) + b1_ref[0, 0]
        x_glu = jnp.minimum(t[:, :ff], SWIGLU_LIMIT)
        x_lin = jnp.clip(t[:, ff:], -SWIGLU_LIMIT, SWIGLU_LIMIT)
        act = x_glu * jax.nn.sigmoid(SWIGLU_ALPHA * x_glu) * (x_lin + 1)
        y = jnp.dot(act.astype(BF16), w2b_ref[...], preferred_element_type=F32) + b2_ref[0, 0]
        _store_row_tiled(o_ref, y)

    @pl.when(jnp.logical_not(used))
    def _():
        o_ref[...] = jnp.zeros_like(o_ref)


def moe_experts(xs, block_e, n_used, w1, b1, w2, b2, layer):
    depth, ne, d, ff2 = w1.shape
    ff = w2.shape[2]
    tm = MOE_TILE * (d // LANES)
    nb = xs.shape[0] // tm
    grid_spec = pltpu.PrefetchScalarGridSpec(
        num_scalar_prefetch=2,
        grid=(nb,),
        in_specs=[pl.BlockSpec((tm, LANES), lambda i, be, nu: (jnp.where(i < nu[0], i, 0), 0)),
                  pl.BlockSpec((1, 1, d, ff2), lambda i, be, nu: (layer, be[i], 0, 0)),
                  pl.BlockSpec((1, 1, 1, ff2), lambda i, be, nu: (layer, be[i], 0, 0)),
                  pl.BlockSpec((1, 1, ff, d), lambda i, be, nu: (layer, be[i], 0, 0)),
                  pl.BlockSpec((1, 1, 1, d), lambda i, be, nu: (layer, be[i], 0, 0))],
        out_specs=pl.BlockSpec((tm, LANES), lambda i, be, nu: (i, 0)),
        scratch_shapes=[pltpu.VMEM((d, ff2), BF16), pltpu.VMEM((ff, d), BF16)],
    )
    return pl.pallas_call(
        _moe_kernel,
        grid_spec=grid_spec,
        out_shape=jax.ShapeDtypeStruct(xs.shape, F32),
        compiler_params=_params(("arbitrary",)),
        name="moe_experts",
    )(block_e, n_used, xs, w1, b1.reshape(depth, ne, 1, ff2), w2, b2.reshape(depth, ne, 1, d))


def _moe_plan(idx, rank, counts, n_tok):
    counts = counts.astype(jnp.int32)
    padded = (counts + MOE_TILE - 1) // MOE_TILE * MOE_TILE
    padded_ends = jnp.cumsum(padded)
    padded_starts = padded_ends - padded
    onehot = idx[..., None] == jnp.arange(N_EXPERTS, dtype=jnp.int32)
    slot_of = rank + jnp.sum(jnp.where(onehot, padded_starts, 0), axis=-1)
    n_blocks = -(-n_tok * MOE_TOP_K // MOE_TILE) + N_EXPERTS
    block_row = jnp.arange(n_blocks, dtype=jnp.int32) * MOE_TILE
    block_e = jnp.minimum(jnp.sum(padded_ends[None, :] <= block_row[:, None], axis=1), N_EXPERTS - 1)
    n_used = padded_ends[-1:] // MOE_TILE
    last_block = jnp.concatenate([jnp.where(counts > 0, padded_ends - MOE_TILE, -1), n_used])
    slot_tiles = slot_of.astype(jnp.int32).reshape(n_tok // TOKEN_TILE, 1, TOKEN_TILE * MOE_TOP_K)
    return slot_tiles, block_e.astype(jnp.int32), n_used.astype(jnp.int32), last_block.astype(jnp.int32), n_blocks


def _dispatch_kernel(slot_ref, last_ref, h_ref, xs_ref, zero_ref, sem, *, parts):
    tm = h_ref.shape[0] // parts
    block = MOE_TILE * parts

    @pl.when(pl.program_id(0) == 0)
    def _():
        zero_ref[...] = jnp.zeros_like(zero_ref)

        def zero_copy(e):
            row = pl.multiple_of(jnp.maximum(last_ref[e], 0) * parts, block)
            return pltpu.make_async_copy(zero_ref, xs_ref.at[pl.ds(row, block)], sem)

        def block_copy(i):
            return pltpu.make_async_copy(zero_ref, xs_ref.at[pl.ds(pl.multiple_of(i * block, block), block)], sem)

        n_used = last_ref[N_EXPERTS]
        n_blocks = xs_ref.shape[0] // block
        lax.fori_loop(n_used, n_blocks, lambda i, c: (block_copy(i).start(), c)[1], 0)
        for e in range(N_EXPERTS):
            pl.when(last_ref[e] >= 0)(lambda e=e: zero_copy(e).start())
        for e in range(N_EXPERTS):
            pl.when(last_ref[e] >= 0)(lambda e=e: zero_copy(e).wait())
        lax.fori_loop(n_used, n_blocks, lambda i, c: (block_copy(i).wait(), c)[1], 0)

    def issue(t, carry):
        for k in range(MOE_TOP_K):
            s = slot_ref[0, 0, t * MOE_TOP_K + k]
            src = h_ref.at[pl.ds(pl.multiple_of(t * parts, parts), parts)]
            dst = xs_ref.at[pl.ds(pl.multiple_of(s * parts, parts), parts)]
            pltpu.make_async_copy(src, dst, sem).start(priority=k % 2)
        return carry

    lax.fori_loop(0, tm, issue, 0, unroll=DMA_ISSUE_UNROLL)
    for k in range(MOE_TOP_K):
        pltpu.make_async_copy(h_ref, xs_ref.at[pl.ds(0, tm * parts)], sem).wait()


def moe_dispatch(h, slot_tiles, last_block, n_blocks, n_tok):
    parts = h.shape[0] // n_tok
    tm = TOKEN_TILE
    return pl.pallas_call(
        functools.partial(_dispatch_kernel, parts=parts),
        grid=(n_tok // tm,),
        in_specs=[pl.BlockSpec((1, 1, tm * MOE_TOP_K), lambda i: (i, 0, 0), memory_space=pltpu.SMEM),
                  pl.BlockSpec(memory_space=pltpu.SMEM),
                  pl.BlockSpec((tm * parts, LANES), lambda i: (i, 0))],
        out_specs=pl.BlockSpec(memory_space=pl.ANY),
        out_shape=jax.ShapeDtypeStruct((n_blocks * MOE_TILE * parts, LANES), F32),
        scratch_shapes=[pltpu.VMEM((MOE_TILE * parts, LANES), F32), pltpu.SemaphoreType.DMA(())],
        compiler_params=_params(("arbitrary",)),
        name="moe_dispatch",
    )(slot_tiles, last_block, h)


def _combine_kernel(slot_ref, next_slot_ref, x_ref, gate_ref, mod_ref, *rest, final):
    ys_ref, o_ref, ybuf_ref, sems = rest[-4:]
    tm, d = x_ref.shape[1:]
    parts = d // LANES
    g = pl.program_id(0) * pl.num_programs(1) + pl.program_id(1)
    n_steps = pl.num_programs(0) * pl.num_programs(1)

    def fetch(slots, buf):
        def issue(t, carry):
            for k in range(MOE_TOP_K):
                s = slots[0, 0, t * MOE_TOP_K + k]
                src = ys_ref.at[pl.ds(pl.multiple_of(s * parts, parts), parts)]
                dst = ybuf_ref.at[buf * MOE_TOP_K + k, pl.ds(pl.multiple_of(t * parts, parts), parts)]
                pltpu.make_async_copy(src, dst, sems.at[buf]).start(priority=k % 2)
            return carry

        lax.fori_loop(0, tm, issue, 0, unroll=DMA_ISSUE_UNROLL)

    def combine(buf):
        for k in range(MOE_TOP_K):
            pltpu.make_async_copy(ys_ref.at[pl.ds(0, tm * parts)], ybuf_ref.at[buf * MOE_TOP_K + k], sems.at[buf]).wait()
        gate = gate_ref[0]
        y = None
        for k in range(MOE_TOP_K):
            term = gate[:, k:k + 1] * _load_row_tiled(ybuf_ref, tm, d, (buf * MOE_TOP_K + k,))
            y = term if y is None else y + term
        x = x_ref[0] + mod_ref[0, 5:6, :] * y
        if final:
            fw_ref = rest[0]
            x = x * lax.rsqrt(jnp.mean(x * x, axis=-1, keepdims=True) + RMS_EPS) * fw_ref[...]
        o_ref[0] = x

    pl.when(g == 0)(lambda: fetch(slot_ref, 0))
    for buf in range(2):
        pl.when((g + 1 < n_steps) & ((g + 1) % 2 == buf))(lambda buf=buf: fetch(next_slot_ref, buf))
    for buf in range(2):
        pl.when(g % 2 == buf)(lambda buf=buf: combine(buf))


def moe_combine(xc, ys, slot_tiles, gates, mod, n_lat, final_w=None):
    bsz, nt, d = xc.shape
    tm = TOKEN_TILE
    tiles = nt // tm
    last = bsz * tiles - 1
    slot_spec = lambda ahead: pl.BlockSpec((1, 1, tm * MOE_TOP_K),
                                           lambda b, j: (jnp.minimum(b * tiles + j + ahead, last), 0, 0),
                                           memory_space=pltpu.SMEM)
    in_specs = [slot_spec(0), slot_spec(1),
                pl.BlockSpec((1, tm, d), lambda b, j: (b, j, 0)),
                pl.BlockSpec((1, tm, LANES), lambda b, j: (b, j, 0)),
                pl.BlockSpec((1, SUBLANES, d), _mod_index(n_lat // tm, bsz))]
    args = [slot_tiles, slot_tiles, xc, gates, mod]
    if final_w is not None:
        in_specs.append(pl.BlockSpec((1, d), lambda b, j: (0, 0)))
        args.append(final_w.reshape(1, d))
    in_specs.append(pl.BlockSpec(memory_space=pl.ANY))
    args.append(ys)
    return pl.pallas_call(
        functools.partial(_combine_kernel, final=final_w is not None),
        grid=(bsz, tiles),
        in_specs=in_specs,
        out_specs=pl.BlockSpec((1, tm, d), lambda b, j: (b, j, 0)),
        out_shape=jax.ShapeDtypeStruct((bsz, nt, d), F32),
        scratch_shapes=[pltpu.VMEM((2 * MOE_TOP_K, tm * (d // LANES), LANES), F32), pltpu.SemaphoreType.DMA((2,))],
        compiler_params=_params(("arbitrary", "arbitrary")),
        name="moe_combine",
    )(*args)


def moe_layer(xc, mod, norm_w, w_router, b_router, experts, n_lat, final_w=None):
    bsz, nt, d = xc.shape
    n_tok = bsz * nt
    h, idx, gates, rank, counts = router(xc, mod, norm_w, w_router, b_router, n_lat)
    top = lambda a: a.reshape(n_tok, LANES)[:, :MOE_TOP_K]
    slot_tiles, block_e, n_used, last_block, n_blocks = _moe_plan(top(idx), top(rank), counts[0, :N_EXPERTS], n_tok)
    xs = moe_dispatch(h, slot_tiles, last_block, n_blocks, n_tok)
    ys = moe_experts(xs, block_e, n_used, *experts)
    return moe_combine(xc, ys, slot_tiles, gates, mod, n_lat, final_w)


def kernel(x, c, ctx, c_ctx, ada_w, ada_b, norm_w, final_norm_w, ev_w_in, ev_w_out, s5_lam_re, s5_lam_im, s5_log_step, s5_b_re, s5_b_im, s5_c_re, s5_c_im, s5_d, s5_w_glu, ret_log_decay, na_w_qkv, na_w_o, na_rpb, moe_w_router, moe_b_router, moe_w1, moe_b1, moe_w2, moe_b2):
    bsz, n_lat, d = x.shape
    n_ctx = ctx.shape[1]
    depth = ada_w.shape[0]
    mod = adaln_table(c, c_ctx, ada_w, ada_b)
    xc = jnp.concatenate([x, ctx], axis=1)
    for i in range(depth):
        last = i == depth - 1
        j = i // 2
        if i % 2 == 0:
            proj = norm_proj(xc, mod[i], norm_w[i, 0], ev_w_in[j], n_lat, 0)
            s5p = _s5_layout(s5_lam_re[j], s5_lam_im[j], s5_log_step[j], s5_b_re[j], s5_b_im[j],
                             s5_c_re[j], s5_c_im[j])
            y_s5 = s5_scan(proj, s5p, n_lat)
            o_ret = retention(proj, ret_log_decay[j], n_lat)
            xc = mix_out(xc, proj, y_s5, o_ret, mod[i], s5_d[j], s5_w_glu[j], ev_w_out[j], n_lat)
        else:
            qkv = norm_proj(xc, mod[i], norm_w[i, 0], na_w_qkv[j], n_lat, 0)
            att = na_attention(qkv, na_rpb[j], n_lat)
            assert last, "an odd layer is only supported as the final layer (no context output needed)"
            xc = proj_residual(att, xc, mod[i], na_w_o[j])
        experts = (moe_w1, moe_b1, moe_w2, moe_b2, i)
        if last:
            return moe_layer(xc[:, :n_lat], mod[i], norm_w[i, 1], moe_w_router[i], moe_b_router[i],
                             experts, n_lat, final_norm_w)
        xc = moe_layer(xc, mod[i], norm_w[i, 1], moe_w_router[i], moe_b_router[i], experts, n_lat)
```

```python
import functools
import math

import numpy as np
import jax
import jax.numpy as jnp
from jax import lax
from jax.experimental import pallas as pl
from jax.experimental.pallas import tpu as pltpu

F32 = jnp.float32
BF16 = jnp.bfloat16

GRID_W = 64
RMS_EPS = 1e-6
S5_WIDTH = 512
S5_GROUP = 16
S5_GROUPS = S5_WIDTH // S5_GROUP
S5_STATE = 64
RET_HEADS = 4
RET_HEAD_DIM = 128
RET_WIDTH = RET_HEADS * RET_HEAD_DIM
ROPE_BASE = 10000.0
NA_HEADS = 16
NA_HEAD_DIM = 64
NA_WIDTH = NA_HEADS * NA_HEAD_DIM
NA_WIN_ROWS = 8
NA_WIN_COLS = 16
N_EXPERTS = 32
MOE_TOP_K = 4
SWIGLU_LIMIT = 7.0
SWIGLU_ALPHA = 1.702

LANES = 128
SUBLANES = 8
MXU_DIM = 256
V7X_VMEM_BYTES = 64 * 1024 * 1024
VMEM_LIMIT = V7X_VMEM_BYTES * 7 // 8

MAX_TOKEN_TILE = 512
S5_CHUNK = 256
RET_CHUNK = 256
MOE_TILE = 512
DMA_ISSUE_UNROLL = 8
NA_Q_ROWS = 4
NA_BAND_ROWS = 12
NA_SOFTMAX_ROWS = 16
NA_SCORE_SLOTS = 4
assert NA_BAND_ROWS >= NA_WIN_ROWS + NA_Q_ROWS - 1 and (NA_BAND_ROWS - NA_Q_ROWS) % NA_Q_ROWS == 0
MASK_VALUE = -1e30

S5_SLICES = 8
assert S5_SLICES * MXU_DIM == S5_GROUPS * S5_STATE


def _params(sem):
    return pltpu.CompilerParams(dimension_semantics=sem, vmem_limit_bytes=VMEM_LIMIT)


def _adaln_kernel(c_ref, w_ref, b_ref, o_ref):
    c = c_ref[...]
    s = c * jax.nn.sigmoid(c)
    o_ref[0] = jnp.dot(s, w_ref[0], preferred_element_type=F32,
                       precision=lax.Precision.HIGHEST) + b_ref[0]


def adaln_table(c, c_ctx, ada_w, ada_b):
    depth, d, d6 = ada_w.shape
    bsz = c.shape[0]
    cond = jnp.concatenate([c, c_ctx[None, :]], axis=0)
    cond = jnp.pad(cond, ((0, SUBLANES - (bsz + 1)), (0, 0)))
    tn = d6 // 4
    out = pl.pallas_call(
        _adaln_kernel,
        grid=(depth, d6 // tn),
        in_specs=[pl.BlockSpec((SUBLANES, d), lambda i, j: (0, 0)),
                  pl.BlockSpec((1, d, tn), lambda i, j: (i, 0, j)),
                  pl.BlockSpec((1, 1, tn), lambda i, j: (i, 0, j))],
        out_specs=pl.BlockSpec((1, SUBLANES, tn), lambda i, j: (i, 0, j)),
        out_shape=jax.ShapeDtypeStruct((depth, SUBLANES, d6), F32),
        compiler_params=_params(("arbitrary", "arbitrary")),
        name="adaln",
    )(cond, ada_w, ada_b.reshape(depth, 1, d6))
    tab = out[:, :bsz + 1].reshape(depth, bsz + 1, 6, d)
    return jnp.pad(tab, ((0, 0), (0, 0), (0, 2), (0, 0)))


def _norm_mod(x, nw, shift, scale):
    y = x * lax.rsqrt(jnp.mean(x * x, axis=-1, keepdims=True) + RMS_EPS)
    return (y * nw) * (1 + scale) + shift


def _token_tile(n_lat, nt):
    tm = MAX_TOKEN_TILE
    while n_lat % tm or (nt - n_lat) % tm:
        tm //= 2
    return tm


def _mod_index(n_lat_tiles, bsz):
    def index(b, j):
        return (jnp.where(j >= n_lat_tiles, bsz, b), 0, 0)
    return index


def _stream_specs(streams, tm, n_lat):
    d = streams[0].shape[-1]
    lat_tiles = n_lat // tm
    if len(streams) == 1:
        return [pl.BlockSpec((1, tm, d), lambda b, j: (b, j, 0))]
    return [pl.BlockSpec((1, tm, d), lambda b, j: (b, jnp.minimum(j, lat_tiles - 1), 0)),
            pl.BlockSpec((1, tm, d), lambda b, j: (b, jnp.maximum(j - lat_tiles, 0), 0))]


def _stream_tile(refs, lat_tiles):
    if len(refs) == 1:
        return refs[0][0]
    return jnp.where(pl.program_id(1) >= lat_tiles, refs[1][0], refs[0][0])


def _proj_kernel(*refs, shift_row, n_streams, lat_tiles):
    mod_ref, nw_ref, w_ref, o_ref = refs[n_streams:]
    h = _norm_mod(_stream_tile(refs[:n_streams], lat_tiles), nw_ref[...], mod_ref[0, shift_row:shift_row + 1, :],
                  mod_ref[0, shift_row + 1:shift_row + 2, :])
    o_ref[0] = jnp.dot(h.astype(BF16), w_ref[...], preferred_element_type=F32).astype(o_ref.dtype)


def norm_proj(streams, mod, norm_w, w, n_lat, nt, shift_row):
    bsz, _, d = streams[0].shape
    n = w.shape[1]
    tm = _token_tile(n_lat, nt)
    return pl.pallas_call(
        functools.partial(_proj_kernel, shift_row=shift_row, n_streams=len(streams), lat_tiles=n_lat // tm),
        grid=(bsz, nt // tm),
        in_specs=_stream_specs(streams, tm, n_lat) + [
            pl.BlockSpec((1, SUBLANES, d), _mod_index(n_lat // tm, bsz)),
            pl.BlockSpec((1, d), lambda b, j: (0, 0)),
            pl.BlockSpec((d, n), lambda b, j: (0, 0))],
        out_specs=pl.BlockSpec((1, tm, n), lambda b, j: (b, j, 0)),
        out_shape=jax.ShapeDtypeStruct((bsz, nt, n), BF16),
        compiler_params=_params(("arbitrary", "arbitrary")),
        name="norm_proj",
    )(*streams, mod, norm_w.reshape(1, d), w.astype(BF16))


def _s5_discretize(lam_re, lam_im, log_step, b_re, b_im):
    lam_re = jnp.minimum(lam_re.astype(F32), -1e-4)
    lam_im = lam_im.astype(F32)
    step = jnp.exp(log_step.astype(F32))[..., None]
    mag = jnp.exp(lam_re * step)
    ang = lam_im * step
    a_re, a_im = mag * jnp.cos(ang), mag * jnp.sin(ang)
    den = lam_re * lam_re + lam_im * lam_im
    n_re, n_im = a_re - 1.0, a_im
    co_re = (n_re * lam_re + n_im * lam_im) / den
    co_im = (n_im * lam_re - n_re * lam_im) / den
    b_re, b_im = b_re.astype(F32), b_im.astype(F32)
    bb_re = co_re[..., None] * b_re - co_im[..., None] * b_im
    bb_im = co_re[..., None] * b_im + co_im[..., None] * b_re
    return a_re, a_im, bb_re, bb_im


def _s5_layout(lam_re, lam_im, log_step, b_re, b_im, c_re, c_im):
    a_re, a_im, bb_re, bb_im = _s5_discretize(lam_re, lam_im, log_step, b_re, b_im)
    gh = S5_GROUPS // 2
    nq = S5_STATE // S5_GROUP
    replicate = jnp.asarray(np.tile(np.eye(S5_GROUP, dtype=np.float32), (1, gh)))
    blk = np.arange(MXU_DIM) // S5_GROUP
    own_group = jnp.asarray(blk[:, None] == blk[None, :])

    def block_diag(t):
        full = jnp.einsum('dkrc,cl->dkrl', t, replicate, precision=lax.Precision.HIGHEST)
        return jnp.where(own_group, full, 0.0).astype(BF16)

    def arrange_a(a):
        a = a.reshape(2, 2, gh, nq, S5_GROUP)
        return a.transpose(0, 1, 3, 2, 4).reshape(2, S5_SLICES, MXU_DIM)

    def arrange_b(bb):
        bb = bb.reshape(2, 2, gh, nq, S5_GROUP, S5_GROUP)
        return block_diag(bb.transpose(0, 1, 3, 2, 5, 4).reshape(2, S5_SLICES, MXU_DIM, S5_GROUP))

    def arrange_c(c):
        c = c.astype(F32).reshape(2, 2, gh, S5_GROUP, nq, S5_GROUP)
        return block_diag(c.transpose(0, 1, 4, 2, 5, 3).reshape(2, S5_SLICES, MXU_DIM, S5_GROUP))

    return (arrange_a(a_re), arrange_a(a_im), arrange_b(bb_re), arrange_b(bb_im),
            arrange_c(c_re), arrange_c(-c_im.astype(F32)))


def _s5_kernel(u_ref, bre_ref, bim_ref, cre_ref, cim_ref, are_ref, aim_ref, y_ref, bu_ref, st_ref, h_ref, *, nb):
    t_chunk = S5_CHUNK
    half = MXU_DIM // LANES
    d = pl.program_id(0)

    @pl.when(pl.program_id(1) == 0)
    def _():
        h_ref[...] = jnp.zeros_like(h_ref)

    def slab(b, c, lh):
        return (b * 2 + c) * half + lh

    for b in range(nb):
        for h in range(2):
            ub = u_ref[b, :, h * MXU_DIM:(h + 1) * MXU_DIM]
            for q in range(S5_SLICES // 2):
                k = h * (S5_SLICES // 2) + q
                for c, w_ref in ((0, bre_ref), (1, bim_ref)):
                    r = jnp.dot(ub, w_ref[0, k], preferred_element_type=F32)
                    for lh in range(half):
                        bu_ref[slab(b, c, lh), pl.ds(k, t_chunk, stride=S5_SLICES), :] = r[:, lh * LANES:(lh + 1) * LANES]

    ar = [are_ref[0, :, lh * LANES:(lh + 1) * LANES] for lh in range(half)]
    ai = [aim_ref[0, :, lh * LANES:(lh + 1) * LANES] for lh in range(half)]

    def step(t, carry):
        tt = jnp.where(d == 0, t, t_chunk - 1 - t)
        new = []
        for b in range(nb):
            for lh in range(half):
                hr, hi = carry[2 * (b * half + lh)], carry[2 * (b * half + lh) + 1]
                rows = pl.ds(pl.multiple_of(tt * S5_SLICES, S5_SLICES), S5_SLICES)
                xr = bu_ref[slab(b, 0, lh), rows, :]
                xi = bu_ref[slab(b, 1, lh), rows, :]
                nr = ar[lh] * hr - ai[lh] * hi + xr
                ni = ar[lh] * hi + ai[lh] * hr + xi
                st_ref[slab(b, 0, lh), rows, :] = nr
                st_ref[slab(b, 1, lh), rows, :] = ni
                new += [nr, ni]
        return tuple(new)

    n_state = nb * half * 2
    carry = lax.fori_loop(0, t_chunk, step, tuple(h_ref[j] for j in range(n_state)), unroll=8)
    for j in range(n_state):
        h_ref[j] = carry[j]

    for b in range(nb):
        for h in range(2):
            acc = None
            for q in range(S5_SLICES // 2):
                k = h * (S5_SLICES // 2) + q
                for c, w_ref in ((0, cre_ref), (1, cim_ref)):
                    s = jnp.concatenate(
                        [st_ref[slab(b, c, lh), pl.ds(k, t_chunk, stride=S5_SLICES), :] for lh in range(half)], axis=1)
                    term = jnp.dot(s.astype(BF16), w_ref[0, k], preferred_element_type=F32)
                    acc = term if acc is None else acc + term
            y_ref[0, b, :, h * MXU_DIM:(h + 1) * MXU_DIM] = acc.astype(y_ref.dtype)


def _scan_chunk_index(n_chunks, n_ctx_chunks):
    def chunk(d, i):
        fwd = lax.rem(i + (n_chunks - n_ctx_chunks), n_chunks)
        return jnp.where(d == 0, fwd, n_chunks - 1 - i)
    return chunk


def s5_scan(proj, s5p, n_lat):
    bsz, nt, _ = proj.shape
    a_re, a_im, bb_re, bb_im, cc_re, cc_im = s5p
    t = S5_CHUNK
    nc = nt // t
    chunk = _scan_chunk_index(nc, (nt - n_lat) // t)
    wspec = pl.BlockSpec((1, S5_SLICES, MXU_DIM, MXU_DIM), lambda d, i: (d, 0, 0, 0))
    aspec = pl.BlockSpec((1, S5_SLICES, MXU_DIM), lambda d, i: (d, 0, 0))
    n_slab = bsz * 2 * (MXU_DIM // LANES)
    return pl.pallas_call(
        functools.partial(_s5_kernel, nb=bsz),
        grid=(2, nc),
        in_specs=[pl.BlockSpec((bsz, t, S5_WIDTH), lambda d, i: (0, chunk(d, i), 0)),
                  wspec, wspec, wspec, wspec, aspec, aspec],
        out_specs=pl.BlockSpec((1, bsz, t, S5_WIDTH), lambda d, i: (d, 0, chunk(d, i), 0)),
        out_shape=jax.ShapeDtypeStruct((2, bsz, nt, S5_WIDTH), BF16),
        scratch_shapes=[pltpu.VMEM((n_slab, S5_SLICES * S5_CHUNK, LANES), F32),
                        pltpu.VMEM((n_slab, S5_SLICES * S5_CHUNK, LANES), F32),
                        pltpu.VMEM((n_slab, S5_SLICES, LANES), F32)],
        compiler_params=_params(("arbitrary", "arbitrary")),
        name="s5_scan",
    )(proj, bb_re, bb_im, cc_re, cc_im, a_re, a_im)


def _rope_tables(n_lat, n_ctx):
    half = RET_HEAD_DIM // 4
    freq = ROPE_BASE ** (-jnp.arange(half, dtype=F32) / half)
    rows = n_lat // GRID_W
    ang_r = jnp.arange(rows, dtype=F32)[:, None] * freq[None, :]
    ang_c = jnp.arange(GRID_W, dtype=F32)[:, None] * freq[None, :]

    def table(fn):
        by_row = jnp.broadcast_to(fn(ang_r)[:, None, :], (rows, GRID_W, half))
        by_col = jnp.broadcast_to(fn(ang_c)[None, :, :], (rows, GRID_W, half))
        return jnp.concatenate([by_row, by_row, by_col, by_col], axis=-1).reshape(n_lat, RET_HEAD_DIM)

    cos, sin = table(jnp.cos), table(jnp.sin)
    cos = jnp.concatenate([cos, jnp.ones((n_ctx, RET_HEAD_DIM), F32)], axis=0)
    sin = jnp.concatenate([sin, jnp.zeros((n_ctx, RET_HEAD_DIM), F32)], axis=0)
    return cos, sin


def _rope_rotation_matrix():
    blk = RET_HEAD_DIM // 2
    half = blk // 2
    r = np.zeros((RET_HEAD_DIM, RET_HEAD_DIM), np.float32)
    for base in (0, blk):
        for l in range(half):
            r[base + l + half, base + l] = -1.0
            r[base + l, base + l + half] = 1.0
    return jnp.asarray(r, BF16)


def _ret_decay(log_decay, t):
    lg = log_decay.astype(F32)
    scale = RET_HEAD_DIM ** -0.5
    idx = jnp.arange(t, dtype=F32)
    diff = idx[:, None] - idx[None, :]
    diff = jnp.stack([diff, -diff])[:, None]
    inner = jnp.where(diff >= 0, jnp.exp(lg[:, :, None, None] * jnp.maximum(diff, 0.0)), 0.0) * scale
    pos = jnp.stack([idx, t - 1.0 - idx])
    q_dec = jnp.exp(lg[:, :, None] * (pos[:, None, :] + 1.0))
    k_dec = jnp.exp(lg[:, :, None] * (t - 1.0 - pos[:, None, :])) * scale
    blk = jnp.exp(lg * t)
    bcast = lambda v: jnp.broadcast_to(v[..., None], v.shape + (RET_HEAD_DIM,))
    return inner, bcast(q_dec), bcast(k_dec), jnp.broadcast_to(blk[:, :, None, None], (2, RET_HEADS, SUBLANES, RET_HEAD_DIM))


def _ret_kernel(q_ref, k_ref, v_ref, cos_ref, sin_ref, rot_ref, inner_ref, qd_ref, kd_ref, bd_ref, o_ref, s_ref):
    @pl.when(pl.program_id(1) == 0)
    def _():
        s_ref[...] = jnp.zeros_like(s_ref)

    cos = cos_ref[...]
    sin = sin_ref[...]
    rot = rot_ref[...]
    nt_dims = (((1,), (1,)), ((), ()))
    for b in range(q_ref.shape[0]):
        for hd in range(RET_HEADS):
            sl = slice(hd * RET_HEAD_DIM, (hd + 1) * RET_HEAD_DIM)
            q = q_ref[b, :, sl]
            k = k_ref[b, :, sl]
            v = v_ref[b, :, sl]
            qr = q.astype(F32) * cos + jnp.dot(q, rot, preferred_element_type=F32) * sin
            kr = k.astype(F32) * cos + jnp.dot(k, rot, preferred_element_type=F32) * sin
            qb = qr.astype(BF16)
            att = lax.dot_general(qb, kr.astype(BF16), nt_dims, preferred_element_type=F32) * inner_ref[0, hd]
            s = s_ref[b * RET_HEADS + hd]
            o = jnp.dot(att.astype(BF16), v, preferred_element_type=F32)
            o = o + jnp.dot(qb, s.astype(BF16), preferred_element_type=F32) * qd_ref[0, hd]
            kd_t = (kr * kd_ref[0, hd]).T.astype(BF16)
            s_ref[b * RET_HEADS + hd] = bd_ref[0, hd, 0:1, :] * s + jnp.dot(kd_t, v, preferred_element_type=F32)
            o_ref[0, b, :, sl] = o.astype(o_ref.dtype)


def retention(proj, log_decay, n_lat):
    bsz, nt, _ = proj.shape
    t = RET_CHUNK
    nc = nt // t
    chunk = _scan_chunk_index(nc, (nt - n_lat) // t)
    cos, sin = _rope_tables(n_lat, nt - n_lat)
    inner, q_dec, k_dec, blk = _ret_decay(log_decay, t)
    w = RET_WIDTH

    def col(j):
        return pl.BlockSpec((bsz, t, w), lambda d, i: (0, chunk(d, i), j))

    tab = pl.BlockSpec((t, RET_HEAD_DIM), lambda d, i: (chunk(d, i), 0))
    dec = pl.BlockSpec((1, RET_HEADS, t, RET_HEAD_DIM), lambda d, i: (d, 0, 0, 0))
    return pl.pallas_call(
        _ret_kernel,
        grid=(2, nc),
        in_specs=[col(1), col(2), col(3), tab, tab,
                  pl.BlockSpec((RET_HEAD_DIM, RET_HEAD_DIM), lambda d, i: (0, 0)),
                  pl.BlockSpec((1, RET_HEADS, t, t), lambda d, i: (d, 0, 0, 0)),
                  dec, dec,
                  pl.BlockSpec((1, RET_HEADS, SUBLANES, RET_HEAD_DIM), lambda d, i: (d, 0, 0, 0))],
        out_specs=pl.BlockSpec((1, bsz, t, w), lambda d, i: (d, 0, chunk(d, i), 0)),
        out_shape=jax.ShapeDtypeStruct((2, bsz, nt, w), BF16),
        scratch_shapes=[pltpu.VMEM((bsz * RET_HEADS, RET_HEAD_DIM, RET_HEAD_DIM), F32)],
        compiler_params=_params(("arbitrary", "arbitrary")),
        name="retention",
    )(proj, proj, proj, cos, sin, _rope_rotation_matrix(), inner, q_dec, k_dec, blk)


def _mix_out_kernel(u_ref, g_ref, ys_ref, or_ref, *refs, n_streams, lat_tiles):
    mod_ref, d_ref, wglu_ref, wo_ref, o_ref = refs[n_streams:]
    y = u_ref[0].astype(F32) * d_ref[...] + ys_ref[0, 0].astype(F32) + ys_ref[1, 0].astype(F32)
    y = jax.nn.gelu(y)
    a = y * jax.nn.sigmoid(jnp.dot(y.astype(BF16), wglu_ref[...], preferred_element_type=F32))
    o = or_ref[0, 0].astype(F32) + or_ref[1, 0].astype(F32)
    heads = []
    for hd in range(RET_HEADS):
        oh = o[:, hd * RET_HEAD_DIM:(hd + 1) * RET_HEAD_DIM]
        heads.append(oh * lax.rsqrt(jnp.mean(oh * oh, axis=-1, keepdims=True) + RMS_EPS))
    r = jnp.concatenate(heads, axis=1) * jax.nn.silu(g_ref[0].astype(F32))
    m = jnp.dot(a.astype(BF16), wo_ref[:S5_WIDTH, :], preferred_element_type=F32)
    m = m + jnp.dot(r.astype(BF16), wo_ref[S5_WIDTH:, :], preferred_element_type=F32)
    o_ref[0] = _stream_tile(refs[:n_streams], lat_tiles) + mod_ref[0, 2:3, :] * m


def mix_out(streams, proj, y_s5, o_ret, mod, d_skip, w_glu, w_out, n_lat):
    bsz, nt, _ = proj.shape
    d = streams[0].shape[-1]
    tm = _token_tile(n_lat, nt)
    w = S5_WIDTH
    dirs = pl.BlockSpec((2, 1, tm, w), lambda b, j: (0, b, j, 0))
    return pl.pallas_call(
        functools.partial(_mix_out_kernel, n_streams=len(streams), lat_tiles=n_lat // tm),
        grid=(bsz, nt // tm),
        in_specs=[pl.BlockSpec((1, tm, w), lambda b, j: (b, j, 0)),
                  pl.BlockSpec((1, tm, w), lambda b, j: (b, j, 4)),
                  dirs, dirs] + _stream_specs(streams, tm, n_lat) + [
                  pl.BlockSpec((1, SUBLANES, d), _mod_index(n_lat // tm, bsz)),
                  pl.BlockSpec((1, w), lambda b, j: (0, 0)),
                  pl.BlockSpec((w, w), lambda b, j: (0, 0)),
                  pl.BlockSpec((w + RET_WIDTH, d), lambda b, j: (0, 0))],
        out_specs=pl.BlockSpec((1, tm, d), lambda b, j: (b, j, 0)),
        out_shape=jax.ShapeDtypeStruct((bsz, nt, d), F32),
        compiler_params=_params(("arbitrary", "arbitrary")),
        name="mix_out",
    )(proj, proj, y_s5, o_ret, *streams, mod, d_skip.reshape(1, w), w_glu.astype(BF16), w_out.astype(BF16))


def _na_band_start(r0, rows):
    return jnp.clip(r0 - NA_WIN_ROWS // 2, 0, rows - NA_BAND_ROWS)


def _na_bias(rpb, rows):
    w = GRID_W
    n_var = (NA_BAND_ROWS - NA_Q_ROWS) // NA_Q_ROWS + 1
    half = NA_WIN_ROWS // 2
    r0 = np.array([v * NA_Q_ROWS if v * NA_Q_ROWS <= half else rows - NA_BAND_ROWS + v * NA_Q_ROWS
                   for v in range(n_var)])
    bs = np.clip(r0 - half, 0, rows - NA_BAND_ROWS)
    assert list(r0 - bs) == [v * NA_Q_ROWS for v in range(n_var)]
    r = r0[:, None] + np.arange(NA_Q_ROWS)[None, :]
    rs = np.clip(r - half, 0, rows - NA_WIN_ROWS)
    a = bs[:, None] + np.arange(NA_BAND_ROWS)[None, :]
    row_ok = (a[:, None, :] >= rs[:, :, None]) & (a[:, None, :] < rs[:, :, None] + NA_WIN_ROWS)
    row_off = np.clip(a[:, None, :] - r[:, :, None] + (NA_WIN_ROWS - 1), 0, 2 * NA_WIN_ROWS - 2)
    col = np.arange(w)
    col_start = np.clip(col - NA_WIN_COLS // 2, 0, w - NA_WIN_COLS)
    col_ok = (col[None, :] >= col_start[:, None]) & (col[None, :] < col_start[:, None] + NA_WIN_COLS)
    col_off = col[None, :] - col[:, None] + (NA_WIN_COLS - 1)
    col_sel = (col_off[:, :, None] == np.arange(2 * NA_WIN_COLS - 1)) & col_ok[:, :, None]
    tiles = jnp.sum(rpb.astype(F32)[:, :, None, None, :] * jnp.asarray(col_sel, F32)[None, None], axis=-1)
    tiles = jnp.where(jnp.asarray(col_ok)[None, None], tiles, MASK_VALUE)
    masked = jnp.full((NA_HEADS, w, w), MASK_VALUE, F32)
    groups = []
    for v in range(n_var):
        per_row = []
        for rr in range(NA_Q_ROWS):
            per_row.append(jnp.concatenate(
                [tiles[:, int(row_off[v, rr, i])] if row_ok[v, rr, i] else masked for i in range(NA_BAND_ROWS)],
                axis=-1))
        groups.append(jnp.concatenate(per_row, axis=1))
    return jnp.stack(groups)


def _na_kernel(q_ref, kb_ref, vb_ref, kc_ref, vc_ref, bias_ref, o_ref, s_ref, p_ref):
    scale = NA_HEAD_DIM ** -0.5
    nt_dims = (((1,), (1,)), ((), ()))
    nq = q_ref.shape[1]
    n_ctx = kc_ref.shape[1]
    n_slot = s_ref.shape[0]
    heads_per_tile = LANES // NA_HEAD_DIM
    lane = lax.broadcasted_iota(jnp.int32, (nq, LANES), 1)
    for j in range(NA_WIDTH // LANES):
        sl = slice(j * LANES, (j + 1) * LANES)
        q2 = q_ref[0, :, sl].astype(F32) * scale
        k2 = kb_ref[0, :, sl]
        v2 = vb_ref[0, :, sl]
        kc2 = kc_ref[0, :, sl]
        vc2 = vc_ref[0, :, sl]
        slot = j % n_slot
        qm = jnp.concatenate(
            [jnp.where((lane >= hh * NA_HEAD_DIM) & (lane < (hh + 1) * NA_HEAD_DIM), q2, 0.0).astype(BF16)
             for hh in range(heads_per_tile)], axis=0)
        s_ref[slot, :, :n_ctx] = lax.dot_general(qm, kc2, nt_dims, preferred_element_type=F32)
        s_loc = lax.dot_general(qm, k2, nt_dims, preferred_element_type=F32)
        for hh in range(heads_per_tile):
            rows = slice(hh * nq, (hh + 1) * nq)
            s_ref[slot, rows, n_ctx:] = s_loc[rows] + bias_ref[0, j * heads_per_tile + hh]
        inv = []
        for c in range(heads_per_tile * nq // NA_SOFTMAX_ROWS):
            rows = slice(c * NA_SOFTMAX_ROWS, (c + 1) * NA_SOFTMAX_ROWS)
            s = s_ref[slot, rows, :]
            p = jnp.exp(s - jnp.max(s, axis=-1, keepdims=True))
            inv.append(1.0 / jnp.sum(p, axis=-1, keepdims=True))
            p_ref[slot, rows, :] = p.astype(BF16)
        o = jnp.dot(p_ref[slot, :, :n_ctx], vc2, preferred_element_type=F32)
        o = o + jnp.dot(p_ref[slot, :, n_ctx:], v2, preferred_element_type=F32)
        o = o * jnp.concatenate(inv, axis=0)
        o2 = o[:nq]
        for hh in range(1, heads_per_tile):
            o2 = jnp.where(lane >= hh * NA_HEAD_DIM, o[hh * nq:(hh + 1) * nq], o2)
        o_ref[0, :, sl] = o2.astype(o_ref.dtype)


def na_attention(qkv, rpb, n_lat):
    bsz, nt, _ = qkv.shape
    w = GRID_W
    rows = n_lat // w
    n_ctx = nt - n_lat
    nq = NA_Q_ROWS * w
    band = NA_BAND_ROWS * w
    n_all = n_ctx + band
    stacked = nq * (LANES // NA_HEAD_DIM)
    assert n_ctx % LANES == 0 and band % LANES == 0 and rows % NA_Q_ROWS == 0 and rows >= NA_BAND_ROWS + NA_WIN_ROWS // 2

    def band_start(g):
        return _na_band_start(g * NA_Q_ROWS, rows)

    def band_spec(j):
        return pl.BlockSpec((pl.Element(1), pl.Element(band), pl.Element(NA_WIDTH)),
                            lambda b, g: (b, band_start(g) * w, j * NA_WIDTH))

    def ctx_spec(j):
        return pl.BlockSpec((1, n_ctx, NA_WIDTH), lambda b, g: (b, n_lat // n_ctx, j))

    return pl.pallas_call(
        _na_kernel,
        grid=(bsz, rows // NA_Q_ROWS),
        in_specs=[pl.BlockSpec((1, nq, NA_WIDTH), lambda b, g: (b, g, 0)),
                  band_spec(1), band_spec(2), ctx_spec(1), ctx_spec(2),
                  pl.BlockSpec((1, NA_HEADS, nq, band), lambda b, g: (g - band_start(g) // NA_Q_ROWS, 0, 0, 0))],
        out_specs=pl.BlockSpec((1, nq, NA_WIDTH), lambda b, g: (b, g, 0)),
        out_shape=jax.ShapeDtypeStruct((bsz, n_lat, NA_WIDTH), BF16),
        scratch_shapes=[pltpu.VMEM((NA_SCORE_SLOTS, stacked, n_all), F32),
                        pltpu.VMEM((NA_SCORE_SLOTS, stacked, n_all), BF16)],
        compiler_params=_params(("arbitrary", "arbitrary")),
        name="na_attention",
    )(qkv, qkv, qkv, qkv, qkv, _na_bias(rpb, rows))


def _proj_res_kernel(a_ref, x_ref, mod_ref, w_ref, o_ref):
    m = jnp.dot(a_ref[0], w_ref[...], preferred_element_type=F32)
    o_ref[0] = x_ref[0] + mod_ref[0, 2:3, :] * m


def proj_residual(a, x, mod, w):
    bsz, n, k = a.shape
    d = x.shape[-1]
    tm = _token_tile(n, n)
    return pl.pallas_call(
        _proj_res_kernel,
        grid=(bsz, n // tm),
        in_specs=[pl.BlockSpec((1, tm, k), lambda b, j: (b, j, 0)),
                  pl.BlockSpec((1, tm, d), lambda b, j: (b, j, 0)),
                  pl.BlockSpec((1, SUBLANES, d), lambda b, j: (b, 0, 0)),
                  pl.BlockSpec((k, d), lambda b, j: (0, 0))],
        out_specs=pl.BlockSpec((1, tm, d), lambda b, j: (b, j, 0)),
        out_shape=jax.ShapeDtypeStruct((bsz, n, d), F32),
        compiler_params=_params(("arbitrary", "arbitrary")),
        name="proj_residual",
    )(a, x, mod, w.astype(BF16))


def _store_row_tiled(ref, value, index=()):
    rows, d = value.shape
    parts = d // LANES
    for j in range(parts):
        ref[index + (pl.ds(j, rows, stride=parts), slice(None))] = value[:, j * LANES:(j + 1) * LANES]


def _load_row_tiled(ref, rows, d, index=()):
    parts = d // LANES
    return jnp.concatenate([ref[index + (pl.ds(j, rows, stride=parts), slice(None))] for j in range(parts)], axis=1)


def _router_kernel(x_ref, mod_ref, nw_ref, wr_ref, br_ref, tri_ref, h_ref, idx_ref, gate_ref, rank_ref, cnt_ref,
                   run_ref):
    first_step = (pl.program_id(0) == 0) & (pl.program_id(1) == 0)

    @pl.when(first_step)
    def _():
        run_ref[...] = jnp.zeros_like(run_ref)

    h = _norm_mod(x_ref[0], nw_ref[...], mod_ref[0, 3:4, :], mod_ref[0, 4:5, :])
    _store_row_tiled(h_ref, h)
    logits = jnp.dot(h.astype(BF16), wr_ref[...], preferred_element_type=F32) + br_ref[...]
    lane = lax.broadcasted_iota(jnp.int32, logits.shape, 1)
    lane_f = lane.astype(F32)
    vals, idxs = [], []
    cur = logits
    for _ in range(MOE_TOP_K):
        m = jnp.max(cur, axis=-1, keepdims=True)
        first = jnp.min(jnp.where(cur == m, lane_f, float(LANES)), axis=-1, keepdims=True)
        vals.append(m)
        idxs.append(first)
        cur = jnp.where(lane_f == first, MASK_VALUE, cur)
    es = [jnp.exp(v - vals[0]) for v in vals]
    tot = es[0]
    for e in es[1:]:
        tot = tot + e
    onehots = [(lane_f == idxs[k]).astype(F32) for k in range(MOE_TOP_K)]
    multi = onehots[0]
    for oh in onehots[1:]:
        multi = multi + oh
    before = jnp.dot(tri_ref[...], multi.astype(BF16), preferred_element_type=F32) + run_ref[0:1, :]
    idx_out = jnp.zeros(logits.shape, F32)
    gate_out = jnp.zeros(logits.shape, F32)
    rank_out = jnp.zeros(logits.shape, F32)
    for k in range(MOE_TOP_K):
        idx_out = jnp.where(lane == k, idxs[k], idx_out)
        gate_out = jnp.where(lane == k, es[k] / tot, gate_out)
        rank_out = jnp.where(lane == k, jnp.sum(before * onehots[k], axis=-1, keepdims=True), rank_out)
    idx_ref[0] = idx_out.astype(jnp.int32)
    gate_ref[0] = gate_out
    rank_ref[0] = rank_out.astype(jnp.int32)
    run_ref[0:1, :] = run_ref[0:1, :] + jnp.sum(multi, axis=0, keepdims=True)
    cnt_ref[...] = jnp.broadcast_to(run_ref[0:1, :], cnt_ref.shape)


def router(xc, mod, norm_w, w_router, b_router, n_lat):
    bsz, nt, d = xc.shape
    tm = _token_tile(n_lat, nt)
    tiles = nt // tm
    parts = d // LANES
    wr = jnp.pad(w_router, ((0, 0), (0, LANES - N_EXPERTS))).astype(BF16)
    br = jnp.pad(b_router.astype(F32), (0, LANES - N_EXPERTS), constant_values=MASK_VALUE).reshape(1, LANES)
    tri = jnp.tril(jnp.ones((tm, tm), BF16), k=-1)
    tok = lambda n, dt: jax.ShapeDtypeStruct((bsz, nt, n), dt)
    out = lambda n: pl.BlockSpec((1, tm, n), lambda b, j: (b, j, 0))
    return pl.pallas_call(
        _router_kernel,
        grid=(bsz, nt // tm),
        in_specs=[pl.BlockSpec((1, tm, d), lambda b, j: (b, j, 0)),
                  pl.BlockSpec((1, SUBLANES, d), _mod_index(n_lat // tm, bsz)),
                  pl.BlockSpec((1, d), lambda b, j: (0, 0)),
                  pl.BlockSpec((d, LANES), lambda b, j: (0, 0)),
                  pl.BlockSpec((1, LANES), lambda b, j: (0, 0)),
                  pl.BlockSpec((tm, tm), lambda b, j: (0, 0))],
        out_specs=[pl.BlockSpec((tm * parts, LANES), lambda b, j: (b * tiles + j, 0)),
                   out(LANES), out(LANES), out(LANES),
                   pl.BlockSpec((SUBLANES, LANES), lambda b, j: (0, 0))],
        out_shape=[jax.ShapeDtypeStruct((bsz * nt * parts, LANES), F32),
                   tok(LANES, jnp.int32), tok(LANES, F32), tok(LANES, jnp.int32),
                   jax.ShapeDtypeStruct((SUBLANES, LANES), F32)],
        scratch_shapes=[pltpu.VMEM((SUBLANES, LANES), F32)],
        compiler_params=_params(("arbitrary", "arbitrary")),
        name="router",
    )(xc, mod, norm_w.reshape(1, d), wr, br, tri)


def _moe_kernel(be_ref, nu_ref, x_ref, w1_ref, b1_ref, w2_ref, b2_ref, o_ref, w1b_ref, w2b_ref):
    i = pl.program_id(0)
    ff = w2_ref.shape[2]
    used = i < nu_ref[0]

    @pl.when(used & ((i == 0) | (be_ref[i] != be_ref[jnp.maximum(i - 1, 0)])))
    def _():
        w1b_ref[...] = w1_ref[0, 0].astype(BF16)
        w2b_ref[...] = w2_ref[0, 0].astype(BF16)

    @pl.when(used)
    def _():
        d = w1b_ref.shape[0]
        x = _load_row_tiled(x_ref, MOE_TILE, d)
        t = jnp.dot(x.astype(BF16), w1b_ref[...], preferred_element_type=F32) + b1_ref[0, 0]
        x_glu = jnp.minimum(t[:, :ff], SWIGLU_LIMIT)
        x_lin = jnp.clip(t[:, ff:], -SWIGLU_LIMIT, SWIGLU_LIMIT)
        act = x_glu * jax.nn.sigmoid(SWIGLU_ALPHA * x_glu) * (x_lin + 1)
        y = jnp.dot(act.astype(BF16), w2b_ref[...], preferred_element_type=F32) + b2_ref[0, 0]
        _store_row_tiled(o_ref, y)

    @pl.when(jnp.logical_not(used))
    def _():
        o_ref[...] = jnp.zeros_like(o_ref)


def moe_experts(xs, block_e, n_used, w1, b1, w2, b2, layer):
    depth, ne, d, ff2 = w1.shape
    ff = w2.shape[2]
    tm = MOE_TILE * (d // LANES)
    nb = xs.shape[0] // tm
    grid_spec = pltpu.PrefetchScalarGridSpec(
        num_scalar_prefetch=2,
        grid=(nb,),
        in_specs=[pl.BlockSpec((tm, LANES), lambda i, be, nu: (jnp.where(i < nu[0], i, 0), 0)),
                  pl.BlockSpec((1, 1, d, ff2), lambda i, be, nu: (layer, be[i], 0, 0)),
                  pl.BlockSpec((1, 1, 1, ff2), lambda i, be, nu: (layer, be[i], 0, 0)),
                  pl.BlockSpec((1, 1, ff, d), lambda i, be, nu: (layer, be[i], 0, 0)),
                  pl.BlockSpec((1, 1, 1, d), lambda i, be, nu: (layer, be[i], 0, 0))],
        out_specs=pl.BlockSpec((tm, LANES), lambda i, be, nu: (i, 0)),
        scratch_shapes=[pltpu.VMEM((d, ff2), BF16), pltpu.VMEM((ff, d), BF16)],
    )
    return pl.pallas_call(
        _moe_kernel,
        grid_spec=grid_spec,
        out_shape=jax.ShapeDtypeStruct(xs.shape, F32),
        compiler_params=_params(("arbitrary",)),
        name="moe_experts",
    )(block_e, n_used, xs, w1, b1.reshape(depth, ne, 1, ff2), w2, b2.reshape(depth, ne, 1, d))


def _moe_plan(idx, rank, counts, n_tok, tm):
    counts = counts.astype(jnp.int32)
    padded = (counts + MOE_TILE - 1) // MOE_TILE * MOE_TILE
    padded_ends = jnp.cumsum(padded)
    padded_starts = padded_ends - padded
    onehot = idx[..., None] == jnp.arange(N_EXPERTS, dtype=jnp.int32)
    slot_of = rank + jnp.sum(jnp.where(onehot, padded_starts, 0), axis=-1)
    n_blocks = -(-n_tok * MOE_TOP_K // MOE_TILE) + N_EXPERTS
    block_row = jnp.arange(n_blocks, dtype=jnp.int32) * MOE_TILE
    block_e = jnp.minimum(jnp.sum(padded_ends[None, :] <= block_row[:, None], axis=1), N_EXPERTS - 1)
    n_used = padded_ends[-1:] // MOE_TILE
    last_block = jnp.concatenate([jnp.where(counts > 0, padded_ends - MOE_TILE, -1), n_used])
    slot_tiles = slot_of.astype(jnp.int32).reshape(n_tok // tm, 1, tm * MOE_TOP_K)
    return slot_tiles, block_e.astype(jnp.int32), n_used.astype(jnp.int32), last_block.astype(jnp.int32), n_blocks


def _dispatch_kernel(slot_ref, last_ref, h_ref, xs_ref, zero_ref, sem, *, parts):
    tm = h_ref.shape[0] // parts
    block = MOE_TILE * parts

    @pl.when(pl.program_id(0) == 0)
    def _():
        zero_ref[...] = jnp.zeros_like(zero_ref)

        def zero_copy(e):
            row = pl.multiple_of(jnp.maximum(last_ref[e], 0) * parts, block)
            return pltpu.make_async_copy(zero_ref, xs_ref.at[pl.ds(row, block)], sem)

        def block_copy(i):
            return pltpu.make_async_copy(zero_ref, xs_ref.at[pl.ds(pl.multiple_of(i * block, block), block)], sem)

        n_used = last_ref[N_EXPERTS]
        n_blocks = xs_ref.shape[0] // block
        lax.fori_loop(n_used, n_blocks, lambda i, c: (block_copy(i).start(), c)[1], 0)
        for e in range(N_EXPERTS):
            pl.when(last_ref[e] >= 0)(lambda e=e: zero_copy(e).start())
        for e in range(N_EXPERTS):
            pl.when(last_ref[e] >= 0)(lambda e=e: zero_copy(e).wait())
        lax.fori_loop(n_used, n_blocks, lambda i, c: (block_copy(i).wait(), c)[1], 0)

    def issue(t, carry):
        for k in range(MOE_TOP_K):
            s = slot_ref[0, 0, t * MOE_TOP_K + k]
            src = h_ref.at[pl.ds(pl.multiple_of(t * parts, parts), parts)]
            dst = xs_ref.at[pl.ds(pl.multiple_of(s * parts, parts), parts)]
            pltpu.make_async_copy(src, dst, sem).start(priority=k % 2)
        return carry

    lax.fori_loop(0, tm, issue, 0, unroll=DMA_ISSUE_UNROLL)
    for k in range(MOE_TOP_K):
        pltpu.make_async_copy(h_ref, xs_ref.at[pl.ds(0, tm * parts)], sem).wait()


def moe_dispatch(h, slot_tiles, last_block, n_blocks, n_tok):
    parts = h.shape[0] // n_tok
    tm = slot_tiles.shape[-1] // MOE_TOP_K
    return pl.pallas_call(
        functools.partial(_dispatch_kernel, parts=parts),
        grid=(n_tok // tm,),
        in_specs=[pl.BlockSpec((1, 1, tm * MOE_TOP_K), lambda i: (i, 0, 0), memory_space=pltpu.SMEM),
                  pl.BlockSpec(memory_space=pltpu.SMEM),
                  pl.BlockSpec((tm * parts, LANES), lambda i: (i, 0))],
        out_specs=pl.BlockSpec(memory_space=pl.ANY),
        out_shape=jax.ShapeDtypeStruct((n_blocks * MOE_TILE * parts, LANES), F32),
        scratch_shapes=[pltpu.VMEM((MOE_TILE * parts, LANES), F32), pltpu.SemaphoreType.DMA(())],
        compiler_params=_params(("arbitrary",)),
        name="moe_dispatch",
    )(slot_tiles, last_block, h)


def _combine_kernel(slot_ref, next_slot_ref, x_ref, gate_ref, mod_ref, *rest, final):
    ys_ref, o_ref, ybuf_ref, sems = rest[-4:]
    tm, d = x_ref.shape[1:]
    parts = d // LANES
    g = pl.program_id(0) * pl.num_programs(1) + pl.program_id(1)
    n_steps = pl.num_programs(0) * pl.num_programs(1)

    def fetch(slots, buf):
        def issue(t, carry):
            for k in range(MOE_TOP_K):
                s = slots[0, 0, t * MOE_TOP_K + k]
                src = ys_ref.at[pl.ds(pl.multiple_of(s * parts, parts), parts)]
                dst = ybuf_ref.at[buf * MOE_TOP_K + k, pl.ds(pl.multiple_of(t * parts, parts), parts)]
                pltpu.make_async_copy(src, dst, sems.at[buf]).start(priority=k % 2)
            return carry

        lax.fori_loop(0, tm, issue, 0, unroll=DMA_ISSUE_UNROLL)

    def combine(buf):
        for k in range(MOE_TOP_K):
            pltpu.make_async_copy(ys_ref.at[pl.ds(0, tm * parts)], ybuf_ref.at[buf * MOE_TOP_K + k], sems.at[buf]).wait()
        gate = gate_ref[0]
        y = None
        for k in range(MOE_TOP_K):
            term = gate[:, k:k + 1] * _load_row_tiled(ybuf_ref, tm, d, (buf * MOE_TOP_K + k,))
            y = term if y is None else y + term
        x = x_ref[0] + mod_ref[0, 5:6, :] * y
        if final:
            fw_ref = rest[0]
            x = x * lax.rsqrt(jnp.mean(x * x, axis=-1, keepdims=True) + RMS_EPS) * fw_ref[...]
        o_ref[0] = x

    pl.when(g == 0)(lambda: fetch(slot_ref, 0))
    for buf in range(2):
        pl.when((g + 1 < n_steps) & ((g + 1) % 2 == buf))(lambda buf=buf: fetch(next_slot_ref, buf))
    for buf in range(2):
        pl.when(g % 2 == buf)(lambda buf=buf: combine(buf))


def moe_combine(xc, ys, slot_tiles, gates, mod, n_lat, final_w=None):
    bsz, nt, d = xc.shape
    tm = slot_tiles.shape[-1] // MOE_TOP_K
    tiles = nt // tm
    last = bsz * tiles - 1
    slot_spec = lambda ahead: pl.BlockSpec((1, 1, tm * MOE_TOP_K),
                                           lambda b, j: (jnp.minimum(b * tiles + j + ahead, last), 0, 0),
                                           memory_space=pltpu.SMEM)
    in_specs = [slot_spec(0), slot_spec(1),
                pl.BlockSpec((1, tm, d), lambda b, j: (b, j, 0)),
                pl.BlockSpec((1, tm, LANES), lambda b, j: (b, j, 0)),
                pl.BlockSpec((1, SUBLANES, d), _mod_index(n_lat // tm, bsz))]
    args = [slot_tiles, slot_tiles, xc, gates, mod]
    if final_w is not None:
        in_specs.append(pl.BlockSpec((1, d), lambda b, j: (0, 0)))
        args.append(final_w.reshape(1, d))
    in_specs.append(pl.BlockSpec(memory_space=pl.ANY))
    args.append(ys)
    return pl.pallas_call(
        functools.partial(_combine_kernel, final=final_w is not None),
        grid=(bsz, tiles),
        in_specs=in_specs,
        out_specs=pl.BlockSpec((1, tm, d), lambda b, j: (b, j, 0)),
        out_shape=jax.ShapeDtypeStruct((bsz, nt, d), F32),
        scratch_shapes=[pltpu.VMEM((2 * MOE_TOP_K, tm * (d // LANES), LANES), F32), pltpu.SemaphoreType.DMA((2,))],
        compiler_params=_params(("arbitrary", "arbitrary")),
        name="moe_combine",
    )(*args)


def moe_layer(xc, mod, norm_w, w_router, b_router, experts, n_lat, final_w=None):
    bsz, nt, d = xc.shape
    n_tok = bsz * nt
    h, idx, gates, rank, counts = router(xc, mod, norm_w, w_router, b_router, n_lat)
    top = lambda a: a.reshape(n_tok, LANES)[:, :MOE_TOP_K]
    slot_tiles, block_e, n_used, last_block, n_blocks = _moe_plan(top(idx), top(rank), counts[0, :N_EXPERTS], n_tok,
                                                                   _token_tile(n_lat, nt))
    xs = moe_dispatch(h, slot_tiles, last_block, n_blocks, n_tok)
    ys = moe_experts(xs, block_e, n_used, *experts)
    return moe_combine(xc, ys, slot_tiles, gates, mod, n_lat, final_w)


def kernel(x, c, ctx, c_ctx, ada_w, ada_b, norm_w, final_norm_w, ev_w_in, ev_w_out, s5_lam_re, s5_lam_im, s5_log_step, s5_b_re, s5_b_im, s5_c_re, s5_c_im, s5_d, s5_w_glu, ret_log_decay, na_w_qkv, na_w_o, na_rpb, moe_w_router, moe_b_router, moe_w1, moe_b1, moe_w2, moe_b2):
    bsz, n_lat, d = x.shape
    nt = n_lat + ctx.shape[1]
    depth = ada_w.shape[0]
    mod = adaln_table(c, c_ctx, ada_w, ada_b)
    streams = (x, ctx)
    for i in range(depth):
        last = i == depth - 1
        j = i // 2
        if i % 2 == 0:
            proj = norm_proj(streams, mod[i], norm_w[i, 0], ev_w_in[j], n_lat, nt, 0)
            s5p = _s5_layout(s5_lam_re[j], s5_lam_im[j], s5_log_step[j], s5_b_re[j], s5_b_im[j],
                             s5_c_re[j], s5_c_im[j])
            y_s5 = s5_scan(proj, s5p, n_lat)
            o_ret = retention(proj, ret_log_decay[j], n_lat)
            xc = mix_out(streams, proj, y_s5, o_ret, mod[i], s5_d[j], s5_w_glu[j], ev_w_out[j], n_lat)
        else:
            qkv = norm_proj(streams, mod[i], norm_w[i, 0], na_w_qkv[j], n_lat, nt, 0)
            att = na_attention(qkv, na_rpb[j], n_lat)
            assert last, "an odd layer is only supported as the final layer (no context output needed)"
            xc = proj_residual(att, streams[0], mod[i], na_w_o[j])
        experts = (moe_w1, moe_b1, moe_w2, moe_b2, i)
        if last:
            return moe_layer(xc[:, :n_lat], mod[i], norm_w[i, 1], moe_w_router[i], moe_b_router[i],
                             experts, n_lat, final_norm_w)
        streams = (moe_layer(xc, mod[i], norm_w[i, 1], moe_w_router[i], moe_b_router[i], experts, n_lat),)
```

```python
import functools

import numpy as np
import jax
import jax.numpy as jnp
from jax import lax
from jax.experimental import pallas as pl
from jax.experimental.pallas import tpu as pltpu

F32 = jnp.float32
BF16 = jnp.bfloat16

GRID_W = 64
RMS_EPS = 1e-6
S5_WIDTH = 512
S5_GROUP = 16
S5_GROUPS = S5_WIDTH // S5_GROUP
S5_STATE = 64
RET_HEADS = 4
RET_HEAD_DIM = 128
RET_WIDTH = RET_HEADS * RET_HEAD_DIM
ROPE_BASE = 10000.0
NA_HEADS = 16
NA_HEAD_DIM = 64
NA_WIDTH = NA_HEADS * NA_HEAD_DIM
NA_WIN_ROWS = 8
NA_WIN_COLS = 16
N_EXPERTS = 32
MOE_TOP_K = 4
SWIGLU_LIMIT = 7.0
SWIGLU_ALPHA = 1.702

LANES = 128
SUBLANES = 8
MXU_DIM = 256
V7X_VMEM_BYTES = 64 * 1024 * 1024
VMEM_LIMIT = V7X_VMEM_BYTES * 7 // 8

MAX_TOKEN_TILE = 512
S5_CHUNK = 256
RET_CHUNK = 256
MOE_TILE = 512
DMA_ISSUE_UNROLL = 8
NA_Q_ROWS = 4
NA_BAND_ROWS = 12
NA_SOFTMAX_ROWS = 16
NA_SCORE_SLOTS = 2
assert NA_BAND_ROWS >= NA_WIN_ROWS + NA_Q_ROWS - 1 and (NA_BAND_ROWS - NA_Q_ROWS) % NA_Q_ROWS == 0
MASK_VALUE = -1e30

S5_SLICES = 8
assert S5_SLICES * MXU_DIM == S5_GROUPS * S5_STATE


def _params(sem):
    return pltpu.CompilerParams(dimension_semantics=sem, vmem_limit_bytes=VMEM_LIMIT)


def _adaln_kernel(c_ref, w_ref, b_ref, o_ref):
    c = c_ref[...]
    s = c * jax.nn.sigmoid(c)
    o_ref[0] = jnp.dot(s, w_ref[0], preferred_element_type=F32,
                       precision=lax.Precision.HIGHEST) + b_ref[0]


def adaln_table(c, c_ctx, ada_w, ada_b):
    depth, d, d6 = ada_w.shape
    bsz = c.shape[0]
    cond = jnp.concatenate([c, c_ctx[None, :]], axis=0)
    cond = jnp.pad(cond, ((0, SUBLANES - (bsz + 1)), (0, 0)))
    tn = d6 // 4
    out = pl.pallas_call(
        _adaln_kernel,
        grid=(depth, d6 // tn),
        in_specs=[pl.BlockSpec((SUBLANES, d), lambda i, j: (0, 0)),
                  pl.BlockSpec((1, d, tn), lambda i, j: (i, 0, j)),
                  pl.BlockSpec((1, 1, tn), lambda i, j: (i, 0, j))],
        out_specs=pl.BlockSpec((1, SUBLANES, tn), lambda i, j: (i, 0, j)),
        out_shape=jax.ShapeDtypeStruct((depth, SUBLANES, d6), F32),
        compiler_params=_params(("arbitrary", "arbitrary")),
        name="adaln",
    )(cond, ada_w, ada_b.reshape(depth, 1, d6))
    tab = out[:, :bsz + 1].reshape(depth, bsz + 1, 6, d)
    return jnp.pad(tab, ((0, 0), (0, 0), (0, 2), (0, 0)))


def _norm_mod(x, nw, shift, scale):
    y = x * lax.rsqrt(jnp.mean(x * x, axis=-1, keepdims=True) + RMS_EPS)
    return (y * nw) * (1 + scale) + shift


def _token_tile(n_lat, nt):
    tm = MAX_TOKEN_TILE
    while n_lat % tm or (nt - n_lat) % tm:
        tm //= 2
    return tm


def _mod_index(n_lat_tiles, bsz):
    def index(b, j):
        return (jnp.where(j >= n_lat_tiles, bsz, b), 0, 0)
    return index


def _stream_specs(streams, tm, n_lat):
    d = streams[0].shape[-1]
    lat_tiles = n_lat // tm
    if len(streams) == 1:
        return [pl.BlockSpec((1, tm, d), lambda b, j: (b, j, 0))]
    return [pl.BlockSpec((1, tm, d), lambda b, j: (b, jnp.minimum(j, lat_tiles - 1), 0)),
            pl.BlockSpec((1, tm, d), lambda b, j: (b, jnp.maximum(j - lat_tiles, 0), 0))]


def _stream_tile(refs, lat_tiles):
    if len(refs) == 1:
        return refs[0][0]
    return jnp.where(pl.program_id(1) >= lat_tiles, refs[1][0], refs[0][0])


def _proj_kernel(*refs, shift_row, n_streams, lat_tiles):
    mod_ref, nw_ref, w_ref, o_ref = refs[n_streams:]
    h = _norm_mod(_stream_tile(refs[:n_streams], lat_tiles), nw_ref[...], mod_ref[0, shift_row:shift_row + 1, :],
                  mod_ref[0, shift_row + 1:shift_row + 2, :])
    o_ref[0] = jnp.dot(h.astype(BF16), w_ref[...], preferred_element_type=F32).astype(o_ref.dtype)


def norm_proj(streams, mod, norm_w, w, n_lat, nt, shift_row):
    bsz, _, d = streams[0].shape
    n = w.shape[1]
    tm = _token_tile(n_lat, nt)
    return pl.pallas_call(
        functools.partial(_proj_kernel, shift_row=shift_row, n_streams=len(streams), lat_tiles=n_lat // tm),
        grid=(bsz, nt // tm),
        in_specs=_stream_specs(streams, tm, n_lat) + [
            pl.BlockSpec((1, SUBLANES, d), _mod_index(n_lat // tm, bsz)),
            pl.BlockSpec((1, d), lambda b, j: (0, 0)),
            pl.BlockSpec((d, n), lambda b, j: (0, 0))],
        out_specs=pl.BlockSpec((1, tm, n), lambda b, j: (b, j, 0)),
        out_shape=jax.ShapeDtypeStruct((bsz, nt, n), BF16),
        compiler_params=_params(("arbitrary", "arbitrary")),
        name="norm_proj",
    )(*streams, mod, norm_w.reshape(1, d), w.astype(BF16))


def _s5_discretize(lam_re, lam_im, log_step, b_re, b_im):
    lam_re = jnp.minimum(lam_re.astype(F32), -1e-4)
    lam_im = lam_im.astype(F32)
    step = jnp.exp(log_step.astype(F32))[..., None]
    mag = jnp.exp(lam_re * step)
    ang = lam_im * step
    a_re, a_im = mag * jnp.cos(ang), mag * jnp.sin(ang)
    den = lam_re * lam_re + lam_im * lam_im
    n_re, n_im = a_re - 1.0, a_im
    co_re = (n_re * lam_re + n_im * lam_im) / den
    co_im = (n_im * lam_re - n_re * lam_im) / den
    b_re, b_im = b_re.astype(F32), b_im.astype(F32)
    bb_re = co_re[..., None] * b_re - co_im[..., None] * b_im
    bb_im = co_re[..., None] * b_im + co_im[..., None] * b_re
    return a_re, a_im, bb_re, bb_im


def _s5_layout(lam_re, lam_im, log_step, b_re, b_im, c_re, c_im):
    a_re, a_im, bb_re, bb_im = _s5_discretize(lam_re, lam_im, log_step, b_re, b_im)
    gh = S5_GROUPS // 2
    nq = S5_STATE // S5_GROUP
    replicate = jnp.asarray(np.tile(np.eye(S5_GROUP, dtype=np.float32), (1, gh)))
    blk = np.arange(MXU_DIM) // S5_GROUP
    own_group = jnp.asarray(blk[:, None] == blk[None, :])

    def block_diag(t):
        full = jnp.einsum('dkrc,cl->dkrl', t, replicate, precision=lax.Precision.HIGHEST)
        return jnp.where(own_group, full, 0.0).astype(BF16)

    def arrange_a(a):
        a = a.reshape(2, 2, gh, nq, S5_GROUP)
        return a.transpose(0, 1, 3, 2, 4).reshape(2, S5_SLICES, MXU_DIM)

    def arrange_b(bb):
        bb = bb.reshape(2, 2, gh, nq, S5_GROUP, S5_GROUP)
        return block_diag(bb.transpose(0, 1, 3, 2, 5, 4).reshape(2, S5_SLICES, MXU_DIM, S5_GROUP))

    def arrange_c(c):
        c = c.astype(F32).reshape(2, 2, gh, S5_GROUP, nq, S5_GROUP)
        return block_diag(c.transpose(0, 1, 4, 2, 5, 3).reshape(2, S5_SLICES, MXU_DIM, S5_GROUP))

    return (arrange_a(a_re), arrange_a(a_im), arrange_b(bb_re), arrange_b(bb_im),
            arrange_c(c_re), arrange_c(-c_im.astype(F32)))


def _s5_kernel(u_ref, bre_ref, bim_ref, cre_ref, cim_ref, are_ref, aim_ref, y_ref, bu_ref, st_ref, h_ref, *, nb):
    t_chunk = S5_CHUNK
    half = MXU_DIM // LANES
    d = pl.program_id(0)

    @pl.when(pl.program_id(1) == 0)
    def _():
        h_ref[...] = jnp.zeros_like(h_ref)

    def slab(b, c, lh):
        return (b * 2 + c) * half + lh

    for b in range(nb):
        for h in range(2):
            ub = u_ref[b, :, h * MXU_DIM:(h + 1) * MXU_DIM]
            for q in range(S5_SLICES // 2):
                k = h * (S5_SLICES // 2) + q
                for c, w_ref in ((0, bre_ref), (1, bim_ref)):
                    r = jnp.dot(ub, w_ref[0, k], preferred_element_type=F32)
                    for lh in range(half):
                        bu_ref[slab(b, c, lh), pl.ds(k, t_chunk, stride=S5_SLICES), :] = r[:, lh * LANES:(lh + 1) * LANES]

    ar = [are_ref[0, :, lh * LANES:(lh + 1) * LANES] for lh in range(half)]
    ai = [aim_ref[0, :, lh * LANES:(lh + 1) * LANES] for lh in range(half)]

    def step(t, carry):
        tt = jnp.where(d == 0, t, t_chunk - 1 - t)
        new = []
        for b in range(nb):
            for lh in range(half):
                hr, hi = carry[2 * (b * half + lh)], carry[2 * (b * half + lh) + 1]
                rows = pl.ds(pl.multiple_of(tt * S5_SLICES, S5_SLICES), S5_SLICES)
                xr = bu_ref[slab(b, 0, lh), rows, :]
                xi = bu_ref[slab(b, 1, lh), rows, :]
                nr = ar[lh] * hr - ai[lh] * hi + xr
                ni = ar[lh] * hi + ai[lh] * hr + xi
                st_ref[slab(b, 0, lh), rows, :] = nr
                st_ref[slab(b, 1, lh), rows, :] = ni
                new += [nr, ni]
        return tuple(new)

    n_state = nb * half * 2
    carry = lax.fori_loop(0, t_chunk, step, tuple(h_ref[j] for j in range(n_state)), unroll=8)
    for j in range(n_state):
        h_ref[j] = carry[j]

    for b in range(nb):
        for h in range(2):
            acc = None
            for q in range(S5_SLICES // 2):
                k = h * (S5_SLICES // 2) + q
                for c, w_ref in ((0, cre_ref), (1, cim_ref)):
                    s = jnp.concatenate(
                        [st_ref[slab(b, c, lh), pl.ds(k, t_chunk, stride=S5_SLICES), :] for lh in range(half)], axis=1)
                    term = jnp.dot(s.astype(BF16), w_ref[0, k], preferred_element_type=F32)
                    acc = term if acc is None else acc + term
            y_ref[0, b, :, h * MXU_DIM:(h + 1) * MXU_DIM] = acc.astype(y_ref.dtype)


def _scan_chunk_index(n_chunks, n_ctx_chunks):
    def chunk(d, i):
        fwd = lax.rem(i + (n_chunks - n_ctx_chunks), n_chunks)
        return jnp.where(d == 0, fwd, n_chunks - 1 - i)
    return chunk


def s5_scan(proj, s5p, n_lat):
    bsz, nt, _ = proj.shape
    a_re, a_im, bb_re, bb_im, cc_re, cc_im = s5p
    t = S5_CHUNK
    nc = nt // t
    chunk = _scan_chunk_index(nc, (nt - n_lat) // t)
    wspec = pl.BlockSpec((1, S5_SLICES, MXU_DIM, MXU_DIM), lambda d, i: (d, 0, 0, 0))
    aspec = pl.BlockSpec((1, S5_SLICES, MXU_DIM), lambda d, i: (d, 0, 0))
    n_slab = bsz * 2 * (MXU_DIM // LANES)
    return pl.pallas_call(
        functools.partial(_s5_kernel, nb=bsz),
        grid=(2, nc),
        in_specs=[pl.BlockSpec((bsz, t, S5_WIDTH), lambda d, i: (0, chunk(d, i), 0)),
                  wspec, wspec, wspec, wspec, aspec, aspec],
        out_specs=pl.BlockSpec((1, bsz, t, S5_WIDTH), lambda d, i: (d, 0, chunk(d, i), 0)),
        out_shape=jax.ShapeDtypeStruct((2, bsz, nt, S5_WIDTH), BF16),
        scratch_shapes=[pltpu.VMEM((n_slab, S5_SLICES * S5_CHUNK, LANES), F32),
                        pltpu.VMEM((n_slab, S5_SLICES * S5_CHUNK, LANES), F32),
                        pltpu.VMEM((n_slab, S5_SLICES, LANES), F32)],
        compiler_params=_params(("arbitrary", "arbitrary")),
        name="s5_scan",
    )(proj, bb_re, bb_im, cc_re, cc_im, a_re, a_im)


def _rope_tables(n_lat, n_ctx):
    half = RET_HEAD_DIM // 4
    freq = ROPE_BASE ** (-jnp.arange(half, dtype=F32) / half)
    rows = n_lat // GRID_W
    ang_r = jnp.arange(rows, dtype=F32)[:, None] * freq[None, :]
    ang_c = jnp.arange(GRID_W, dtype=F32)[:, None] * freq[None, :]

    def table(fn):
        by_row = jnp.broadcast_to(fn(ang_r)[:, None, :], (rows, GRID_W, half))
        by_col = jnp.broadcast_to(fn(ang_c)[None, :, :], (rows, GRID_W, half))
        return jnp.concatenate([by_row, by_row, by_col, by_col], axis=-1).reshape(n_lat, RET_HEAD_DIM)

    cos, sin = table(jnp.cos), table(jnp.sin)
    cos = jnp.concatenate([cos, jnp.ones((n_ctx, RET_HEAD_DIM), F32)], axis=0)
    sin = jnp.concatenate([sin, jnp.zeros((n_ctx, RET_HEAD_DIM), F32)], axis=0)
    return cos, sin


def _rope_rotation_matrix():
    blk = RET_HEAD_DIM // 2
    half = blk // 2
    r = np.zeros((RET_HEAD_DIM, RET_HEAD_DIM), np.float32)
    for base in (0, blk):
        for l in range(half):
            r[base + l + half, base + l] = -1.0
            r[base + l, base + l + half] = 1.0
    return jnp.asarray(r, BF16)


def _ret_decay(log_decay, t):
    lg = log_decay.astype(F32)
    scale = RET_HEAD_DIM ** -0.5
    idx = jnp.arange(t, dtype=F32)
    diff = idx[:, None] - idx[None, :]
    diff = jnp.stack([diff, -diff])[:, None]
    inner = jnp.where(diff >= 0, jnp.exp(lg[:, :, None, None] * jnp.maximum(diff, 0.0)), 0.0) * scale
    pos = jnp.stack([idx, t - 1.0 - idx])
    q_dec = jnp.exp(lg[:, :, None] * (pos[:, None, :] + 1.0))
    k_dec = jnp.exp(lg[:, :, None] * (t - 1.0 - pos[:, None, :])) * scale
    blk = jnp.exp(lg * t)
    bcast = lambda v: jnp.broadcast_to(v[..., None], v.shape + (RET_HEAD_DIM,))
    return inner, bcast(q_dec), bcast(k_dec), jnp.broadcast_to(blk[:, :, None, None], (2, RET_HEADS, SUBLANES, RET_HEAD_DIM))


def _ret_kernel(q_ref, k_ref, v_ref, cos_ref, sin_ref, rot_ref, inner_ref, qd_ref, kd_ref, bd_ref, o_ref, s_ref):
    @pl.when(pl.program_id(1) == 0)
    def _():
        s_ref[...] = jnp.zeros_like(s_ref)

    cos = cos_ref[...]
    sin = sin_ref[...]
    rot = rot_ref[...]
    nt_dims = (((1,), (1,)), ((), ()))
    for b in range(q_ref.shape[0]):
        for hd in range(RET_HEADS):
            sl = slice(hd * RET_HEAD_DIM, (hd + 1) * RET_HEAD_DIM)
            q = q_ref[b, :, sl]
            k = k_ref[b, :, sl]
            v = v_ref[b, :, sl]
            qr = q.astype(F32) * cos + jnp.dot(q, rot, preferred_element_type=F32) * sin
            kr = k.astype(F32) * cos + jnp.dot(k, rot, preferred_element_type=F32) * sin
            qb = qr.astype(BF16)
            att = lax.dot_general(qb, kr.astype(BF16), nt_dims, preferred_element_type=F32) * inner_ref[0, hd]
            s = s_ref[b * RET_HEADS + hd]
            o = jnp.dot(att.astype(BF16), v, preferred_element_type=F32)
            o = o + jnp.dot(qb, s.astype(BF16), preferred_element_type=F32) * qd_ref[0, hd]
            kd_t = (kr * kd_ref[0, hd]).T.astype(BF16)
            s_ref[b * RET_HEADS + hd] = bd_ref[0, hd, 0:1, :] * s + jnp.dot(kd_t, v, preferred_element_type=F32)
            o_ref[0, b, :, sl] = o.astype(o_ref.dtype)


def retention(proj, log_decay, n_lat):
    bsz, nt, _ = proj.shape
    t = RET_CHUNK
    nc = nt // t
    chunk = _scan_chunk_index(nc, (nt - n_lat) // t)
    cos, sin = _rope_tables(n_lat, nt - n_lat)
    inner, q_dec, k_dec, blk = _ret_decay(log_decay, t)
    w = RET_WIDTH

    def col(j):
        return pl.BlockSpec((bsz, t, w), lambda d, i: (0, chunk(d, i), j))

    tab = pl.BlockSpec((t, RET_HEAD_DIM), lambda d, i: (chunk(d, i), 0))
    dec = pl.BlockSpec((1, RET_HEADS, t, RET_HEAD_DIM), lambda d, i: (d, 0, 0, 0))
    return pl.pallas_call(
        _ret_kernel,
        grid=(2, nc),
        in_specs=[col(1), col(2), col(3), tab, tab,
                  pl.BlockSpec((RET_HEAD_DIM, RET_HEAD_DIM), lambda d, i: (0, 0)),
                  pl.BlockSpec((1, RET_HEADS, t, t), lambda d, i: (d, 0, 0, 0)),
                  dec, dec,
                  pl.BlockSpec((1, RET_HEADS, SUBLANES, RET_HEAD_DIM), lambda d, i: (d, 0, 0, 0))],
        out_specs=pl.BlockSpec((1, bsz, t, w), lambda d, i: (d, 0, chunk(d, i), 0)),
        out_shape=jax.ShapeDtypeStruct((2, bsz, nt, w), BF16),
        scratch_shapes=[pltpu.VMEM((bsz * RET_HEADS, RET_HEAD_DIM, RET_HEAD_DIM), F32)],
        compiler_params=_params(("arbitrary", "arbitrary")),
        name="retention",
    )(proj, proj, proj, cos, sin, _rope_rotation_matrix(), inner, q_dec, k_dec, blk)


def _mix_out_kernel(u_ref, g_ref, ys_ref, or_ref, *refs, n_streams, lat_tiles):
    mod_ref, d_ref, wglu_ref, wo_ref, o_ref = refs[n_streams:]
    y = u_ref[0].astype(F32) * d_ref[...] + ys_ref[0, 0].astype(F32) + ys_ref[1, 0].astype(F32)
    y = jax.nn.gelu(y)
    a = y * jax.nn.sigmoid(jnp.dot(y.astype(BF16), wglu_ref[...], preferred_element_type=F32))
    o = or_ref[0, 0].astype(F32) + or_ref[1, 0].astype(F32)
    heads = []
    for hd in range(RET_HEADS):
        oh = o[:, hd * RET_HEAD_DIM:(hd + 1) * RET_HEAD_DIM]
        heads.append(oh * lax.rsqrt(jnp.mean(oh * oh, axis=-1, keepdims=True) + RMS_EPS))
    r = jnp.concatenate(heads, axis=1) * jax.nn.silu(g_ref[0].astype(F32))
    m = jnp.dot(a.astype(BF16), wo_ref[:S5_WIDTH, :], preferred_element_type=F32)
    m = m + jnp.dot(r.astype(BF16), wo_ref[S5_WIDTH:, :], preferred_element_type=F32)
    o_ref[0] = _stream_tile(refs[:n_streams], lat_tiles) + mod_ref[0, 2:3, :] * m


def mix_out(streams, proj, y_s5, o_ret, mod, d_skip, w_glu, w_out, n_lat):
    bsz, nt, _ = proj.shape
    d = streams[0].shape[-1]
    tm = _token_tile(n_lat, nt)
    w = S5_WIDTH
    dirs = pl.BlockSpec((2, 1, tm, w), lambda b, j: (0, b, j, 0))
    return pl.pallas_call(
        functools.partial(_mix_out_kernel, n_streams=len(streams), lat_tiles=n_lat // tm),
        grid=(bsz, nt // tm),
        in_specs=[pl.BlockSpec((1, tm, w), lambda b, j: (b, j, 0)),
                  pl.BlockSpec((1, tm, w), lambda b, j: (b, j, 4)),
                  dirs, dirs] + _stream_specs(streams, tm, n_lat) + [
                  pl.BlockSpec((1, SUBLANES, d), _mod_index(n_lat // tm, bsz)),
                  pl.BlockSpec((1, w), lambda b, j: (0, 0)),
                  pl.BlockSpec((w, w), lambda b, j: (0, 0)),
                  pl.BlockSpec((w + RET_WIDTH, d), lambda b, j: (0, 0))],
        out_specs=pl.BlockSpec((1, tm, d), lambda b, j: (b, j, 0)),
        out_shape=jax.ShapeDtypeStruct((bsz, nt, d), F32),
        compiler_params=_params(("arbitrary", "arbitrary")),
        name="mix_out",
    )(proj, proj, y_s5, o_ret, *streams, mod, d_skip.reshape(1, w), w_glu.astype(BF16), w_out.astype(BF16))


def _na_band_start(r0, rows):
    return jnp.clip(r0 - NA_WIN_ROWS // 2, 0, rows - NA_BAND_ROWS)


def _na_bias(rpb, rows):
    w = GRID_W
    n_var = (NA_BAND_ROWS - NA_Q_ROWS) // NA_Q_ROWS + 1
    half = NA_WIN_ROWS // 2
    r0 = np.array([v * NA_Q_ROWS if v * NA_Q_ROWS <= half else rows - NA_BAND_ROWS + v * NA_Q_ROWS
                   for v in range(n_var)])
    bs = np.clip(r0 - half, 0, rows - NA_BAND_ROWS)
    assert list(r0 - bs) == [v * NA_Q_ROWS for v in range(n_var)]
    r = r0[:, None] + np.arange(NA_Q_ROWS)[None, :]
    rs = np.clip(r - half, 0, rows - NA_WIN_ROWS)
    a = bs[:, None] + np.arange(NA_BAND_ROWS)[None, :]
    row_ok = (a[:, None, :] >= rs[:, :, None]) & (a[:, None, :] < rs[:, :, None] + NA_WIN_ROWS)
    row_off = np.clip(a[:, None, :] - r[:, :, None] + (NA_WIN_ROWS - 1), 0, 2 * NA_WIN_ROWS - 2)
    col = np.arange(w)
    col_start = np.clip(col - NA_WIN_COLS // 2, 0, w - NA_WIN_COLS)
    col_ok = (col[None, :] >= col_start[:, None]) & (col[None, :] < col_start[:, None] + NA_WIN_COLS)
    col_off = col[None, :] - col[:, None] + (NA_WIN_COLS - 1)
    col_sel = (col_off[:, :, None] == np.arange(2 * NA_WIN_COLS - 1)) & col_ok[:, :, None]
    tiles = jnp.sum(rpb.astype(F32)[:, :, None, None, :] * jnp.asarray(col_sel, F32)[None, None], axis=-1)
    tiles = jnp.where(jnp.asarray(col_ok)[None, None], tiles, MASK_VALUE)
    masked = jnp.full((NA_HEADS, w, w), MASK_VALUE, F32)
    groups = []
    for v in range(n_var):
        per_row = []
        for rr in range(NA_Q_ROWS):
            per_row.append(jnp.concatenate(
                [tiles[:, int(row_off[v, rr, i])] if row_ok[v, rr, i] else masked for i in range(NA_BAND_ROWS)],
                axis=-1))
        groups.append(jnp.concatenate(per_row, axis=1))
    return jnp.stack(groups)


def _na_kernel(q_ref, kb_ref, vb_ref, kc_ref, vc_ref, bias_ref, o_ref, s_ref, p_ref):
    scale = NA_HEAD_DIM ** -0.5
    nt_dims = (((1,), (1,)), ((), ()))
    nq = q_ref.shape[1]
    n_ctx = kc_ref.shape[1]
    n_slot = s_ref.shape[0]
    heads_per_tile = LANES // NA_HEAD_DIM
    lane = lax.broadcasted_iota(jnp.int32, (nq, LANES), 1)
    for j in range(NA_WIDTH // LANES):
        sl = slice(j * LANES, (j + 1) * LANES)
        q2 = q_ref[0, :, sl].astype(F32) * scale
        k2 = kb_ref[0, :, sl]
        v2 = vb_ref[0, :, sl]
        kc2 = kc_ref[0, :, sl]
        vc2 = vc_ref[0, :, sl]
        qm = jnp.concatenate(
            [jnp.where((lane >= hh * NA_HEAD_DIM) & (lane < (hh + 1) * NA_HEAD_DIM), q2, 0.0).astype(BF16)
             for hh in range(heads_per_tile)], axis=0)
        slot = j % n_slot
        s_ref[slot, :, :n_ctx] = lax.dot_general(qm, kc2, nt_dims, preferred_element_type=F32)
        s_loc = lax.dot_general(qm, k2, nt_dims, preferred_element_type=F32)
        for hh in range(heads_per_tile):
            rows = slice(hh * nq, (hh + 1) * nq)
            s_ref[slot, rows, n_ctx:] = s_loc[rows] + bias_ref[0, j * heads_per_tile + hh]
        inv = []
        for c in range(heads_per_tile * nq // NA_SOFTMAX_ROWS):
            rows = slice(c * NA_SOFTMAX_ROWS, (c + 1) * NA_SOFTMAX_ROWS)
            s = s_ref[slot, rows, :]
            p = jnp.exp(s - jnp.max(s, axis=-1, keepdims=True))
            inv.append(1.0 / jnp.sum(p, axis=-1, keepdims=True))
            p_ref[slot, rows, :] = p.astype(BF16)
        o = jnp.dot(p_ref[slot, :, :n_ctx], vc2, preferred_element_type=F32)
        o = o + jnp.dot(p_ref[slot, :, n_ctx:], v2, preferred_element_type=F32)
        o = o * jnp.concatenate(inv, axis=0)
        o2 = o[:nq]
        for hh in range(1, heads_per_tile):
            o2 = jnp.where(lane >= hh * NA_HEAD_DIM, o[hh * nq:(hh + 1) * nq], o2)
        o_ref[0, :, sl] = o2.astype(o_ref.dtype)


def na_attention(qkv, rpb, n_lat):
    bsz, nt, _ = qkv.shape
    w = GRID_W
    rows = n_lat // w
    n_ctx = nt - n_lat
    nq = NA_Q_ROWS * w
    band = NA_BAND_ROWS * w
    n_all = n_ctx + band
    stacked = nq * (LANES // NA_HEAD_DIM)
    assert n_ctx % LANES == 0 and band % LANES == 0 and rows % NA_Q_ROWS == 0 and rows >= NA_BAND_ROWS + NA_WIN_ROWS // 2

    def band_start(g):
        return _na_band_start(g * NA_Q_ROWS, rows)

    def band_spec(j):
        return pl.BlockSpec((pl.Element(1), pl.Element(band), pl.Element(NA_WIDTH)),
                            lambda b, g: (b, band_start(g) * w, j * NA_WIDTH))

    def ctx_spec(j):
        return pl.BlockSpec((1, n_ctx, NA_WIDTH), lambda b, g: (b, n_lat // n_ctx, j))

    return pl.pallas_call(
        _na_kernel,
        grid=(bsz, rows // NA_Q_ROWS),
        in_specs=[pl.BlockSpec((1, nq, NA_WIDTH), lambda b, g: (b, g, 0)),
                  band_spec(1), band_spec(2), ctx_spec(1), ctx_spec(2),
                  pl.BlockSpec((1, NA_HEADS, nq, band), lambda b, g: (g - band_start(g) // NA_Q_ROWS, 0, 0, 0))],
        out_specs=pl.BlockSpec((1, nq, NA_WIDTH), lambda b, g: (b, g, 0)),
        out_shape=jax.ShapeDtypeStruct((bsz, n_lat, NA_WIDTH), BF16),
        scratch_shapes=[pltpu.VMEM((NA_SCORE_SLOTS, stacked, n_all), F32),
                        pltpu.VMEM((NA_SCORE_SLOTS, stacked, n_all), BF16)],
        compiler_params=_params(("arbitrary", "arbitrary")),
        name="na_attention",
    )(qkv, qkv, qkv, qkv, qkv, _na_bias(rpb, rows))


def _proj_res_kernel(a_ref, x_ref, mod_ref, w_ref, o_ref):
    m = jnp.dot(a_ref[0], w_ref[...], preferred_element_type=F32)
    o_ref[0] = x_ref[0] + mod_ref[0, 2:3, :] * m


def proj_residual(a, x, mod, w):
    bsz, n, k = a.shape
    d = x.shape[-1]
    tm = _token_tile(n, n)
    return pl.pallas_call(
        _proj_res_kernel,
        grid=(bsz, n // tm),
        in_specs=[pl.BlockSpec((1, tm, k), lambda b, j: (b, j, 0)),
                  pl.BlockSpec((1, tm, d), lambda b, j: (b, j, 0)),
                  pl.BlockSpec((1, SUBLANES, d), lambda b, j: (b, 0, 0)),
                  pl.BlockSpec((k, d), lambda b, j: (0, 0))],
        out_specs=pl.BlockSpec((1, tm, d), lambda b, j: (b, j, 0)),
        out_shape=jax.ShapeDtypeStruct((bsz, n, d), F32),
        compiler_params=_params(("arbitrary", "arbitrary")),
        name="proj_residual",
    )(a, x, mod, w.astype(BF16))


def _store_row_tiled(ref, value, index=()):
    rows, d = value.shape
    parts = d // LANES
    for j in range(parts):
        ref[index + (pl.ds(j, rows, stride=parts), slice(None))] = value[:, j * LANES:(j + 1) * LANES]


def _load_row_tiled(ref, rows, d, index=()):
    parts = d // LANES
    return jnp.concatenate([ref[index + (pl.ds(j, rows, stride=parts), slice(None))] for j in range(parts)], axis=1)


def _router_kernel(x_ref, mod_ref, nw_ref, wr_ref, br_ref, tri_ref, h_ref, idx_ref, gate_ref, rank_ref, cnt_ref,
                   run_ref):
    first_step = (pl.program_id(0) == 0) & (pl.program_id(1) == 0)

    @pl.when(first_step)
    def _():
        run_ref[...] = jnp.zeros_like(run_ref)

    h = _norm_mod(x_ref[0], nw_ref[...], mod_ref[0, 3:4, :], mod_ref[0, 4:5, :])
    _store_row_tiled(h_ref, h)
    logits = jnp.dot(h.astype(BF16), wr_ref[...], preferred_element_type=F32) + br_ref[...]
    lane = lax.broadcasted_iota(jnp.int32, logits.shape, 1)
    lane_f = lane.astype(F32)
    vals, idxs = [], []
    cur = logits
    for _ in range(MOE_TOP_K):
        m = jnp.max(cur, axis=-1, keepdims=True)
        first = jnp.min(jnp.where(cur == m, lane_f, float(LANES)), axis=-1, keepdims=True)
        vals.append(m)
        idxs.append(first)
        cur = jnp.where(lane_f == first, MASK_VALUE, cur)
    es = [jnp.exp(v - vals[0]) for v in vals]
    tot = es[0]
    for e in es[1:]:
        tot = tot + e
    onehots = [(lane_f == idxs[k]).astype(F32) for k in range(MOE_TOP_K)]
    multi = onehots[0]
    for oh in onehots[1:]:
        multi = multi + oh
    before = jnp.dot(tri_ref[...], multi.astype(BF16), preferred_element_type=F32) + run_ref[0:1, :]
    idx_out = jnp.zeros(logits.shape, F32)
    gate_out = jnp.zeros(logits.shape, F32)
    rank_out = jnp.zeros(logits.shape, F32)
    for k in range(MOE_TOP_K):
        idx_out = jnp.where(lane == k, idxs[k], idx_out)
        gate_out = jnp.where(lane == k, es[k] / tot, gate_out)
        rank_out = jnp.where(lane == k, jnp.sum(before * onehots[k], axis=-1, keepdims=True), rank_out)
    idx_ref[0] = idx_out.astype(jnp.int32)
    gate_ref[0] = gate_out
    rank_ref[0] = rank_out.astype(jnp.int32)
    run_ref[0:1, :] = run_ref[0:1, :] + jnp.sum(multi, axis=0, keepdims=True)
    cnt_ref[...] = jnp.broadcast_to(run_ref[0:1, :], cnt_ref.shape)


def router(xc, mod, norm_w, w_router, b_router, n_lat):
    bsz, nt, d = xc.shape
    tm = _token_tile(n_lat, nt)
    tiles = nt // tm
    parts = d // LANES
    wr = jnp.pad(w_router, ((0, 0), (0, LANES - N_EXPERTS))).astype(BF16)
    br = jnp.pad(b_router.astype(F32), (0, LANES - N_EXPERTS), constant_values=MASK_VALUE).reshape(1, LANES)
    tri = jnp.tril(jnp.ones((tm, tm), BF16), k=-1)
    tok = lambda n, dt: jax.ShapeDtypeStruct((bsz, nt, n), dt)
    out = lambda n: pl.BlockSpec((1, tm, n), lambda b, j: (b, j, 0))
    return pl.pallas_call(
        _router_kernel,
        grid=(bsz, nt // tm),
        in_specs=[pl.BlockSpec((1, tm, d), lambda b, j: (b, j, 0)),
                  pl.BlockSpec((1, SUBLANES, d), _mod_index(n_lat // tm, bsz)),
                  pl.BlockSpec((1, d), lambda b, j: (0, 0)),
                  pl.BlockSpec((d, LANES), lambda b, j: (0, 0)),
                  pl.BlockSpec((1, LANES), lambda b, j: (0, 0)),
                  pl.BlockSpec((tm, tm), lambda b, j: (0, 0))],
        out_specs=[pl.BlockSpec((tm * parts, LANES), lambda b, j: (b * tiles + j, 0)),
                   out(LANES), out(LANES), out(LANES),
                   pl.BlockSpec((SUBLANES, LANES), lambda b, j: (0, 0))],
        out_shape=[jax.ShapeDtypeStruct((bsz * nt * parts, LANES), F32),
                   tok(LANES, jnp.int32), tok(LANES, F32), tok(LANES, jnp.int32),
                   jax.ShapeDtypeStruct((SUBLANES, LANES), F32)],
        scratch_shapes=[pltpu.VMEM((SUBLANES, LANES), F32)],
        compiler_params=_params(("arbitrary", "arbitrary")),
        name="router",
    )(xc, mod, norm_w.reshape(1, d), wr, br, tri)


def _moe_kernel(be_ref, nu_ref, x_ref, w1_ref, b1_ref, w2_ref, b2_ref, o_ref, w1b_ref, w2b_ref):
    i = pl.program_id(0)
    ff = w2_ref.shape[2]
    used = i < nu_ref[0]

    @pl.when(used & ((i == 0) | (be_ref[i] != be_ref[jnp.maximum(i - 1, 0)])))
    def _():
        w1b_ref[...] = w1_ref[0, 0].astype(BF16)
        w2b_ref[...] = w2_ref[0, 0].astype(BF16)

    @pl.when(used)
    def _():
        d = w1b_ref.shape[0]
        x = _load_row_tiled(x_ref, MOE_TILE, d)
        t = jnp.dot(x.astype(BF16), w1b_ref[...], preferred_element_type=F32) + b1_ref[0, 0]
        x_glu = jnp.minimum(t[:, :ff], SWIGLU_LIMIT)
        x_lin = jnp.clip(t[:, ff:], -SWIGLU_LIMIT, SWIGLU_LIMIT)
        act = x_glu * jax.nn.sigmoid(SWIGLU_ALPHA * x_glu) * (x_lin + 1)
        y = jnp.dot(act.astype(BF16), w2b_ref[...], preferred_element_type=F32) + b2_ref[0, 0]
        _store_row_tiled(o_ref, y)

    @pl.when(jnp.logical_not(used))
    def _():
        o_ref[...] = jnp.zeros_like(o_ref)


def moe_experts(xs, block_e, n_used, w1, b1, w2, b2, layer):
    depth, ne, d, ff2 = w1.shape
    ff = w2.shape[2]
    tm = MOE_TILE * (d // LANES)
    nb = xs.shape[0] // tm
    grid_spec = pltpu.PrefetchScalarGridSpec(
        num_scalar_prefetch=2,
        grid=(nb,),
        in_specs=[pl.BlockSpec((tm, LANES), lambda i, be, nu: (jnp.where(i < nu[0], i, 0), 0)),
                  pl.BlockSpec((1, 1, d, ff2), lambda i, be, nu: (layer, be[i], 0, 0)),
                  pl.BlockSpec((1, 1, 1, ff2), lambda i, be, nu: (layer, be[i], 0, 0)),
                  pl.BlockSpec((1, 1, ff, d), lambda i, be, nu: (layer, be[i], 0, 0)),
                  pl.BlockSpec((1, 1, 1, d), lambda i, be, nu: (layer, be[i], 0, 0))],
        out_specs=pl.BlockSpec((tm, LANES), lambda i, be, nu: (i, 0)),
        scratch_shapes=[pltpu.VMEM((d, ff2), BF16), pltpu.VMEM((ff, d), BF16)],
    )
    return pl.pallas_call(
        _moe_kernel,
        grid_spec=grid_spec,
        out_shape=jax.ShapeDtypeStruct(xs.shape, F32),
        compiler_params=_params(("arbitrary",)),
        name="moe_experts",
    )(block_e, n_used, xs, w1, b1.reshape(depth, ne, 1, ff2), w2, b2.reshape(depth, ne, 1, d))


def _moe_plan(idx, rank, counts, n_tok, tm):
    counts = counts.astype(jnp.int32)
    padded = (counts + MOE_TILE - 1) // MOE_TILE * MOE_TILE
    padded_ends = jnp.cumsum(padded)
    padded_starts = padded_ends - padded
    onehot = idx[..., None] == jnp.arange(N_EXPERTS, dtype=jnp.int32)
    slot_of = rank + jnp.sum(jnp.where(onehot, padded_starts, 0), axis=-1)
    n_blocks = -(-n_tok * MOE_TOP_K // MOE_TILE) + N_EXPERTS
    block_row = jnp.arange(n_blocks, dtype=jnp.int32) * MOE_TILE
    block_e = jnp.minimum(jnp.sum(padded_ends[None, :] <= block_row[:, None], axis=1), N_EXPERTS - 1)
    n_used = padded_ends[-1:] // MOE_TILE
    last_block = jnp.concatenate([jnp.where(counts > 0, padded_ends - MOE_TILE, -1), n_used])
    slot_tiles = slot_of.astype(jnp.int32).reshape(n_tok // tm, 1, tm * MOE_TOP_K)
    return slot_tiles, block_e.astype(jnp.int32), n_used.astype(jnp.int32), last_block.astype(jnp.int32), n_blocks


def _dispatch_kernel(slot_ref, last_ref, h_ref, xs_ref, zero_ref, sem, *, parts):
    tm = h_ref.shape[0] // parts
    block = MOE_TILE * parts

    @pl.when(pl.program_id(0) == 0)
    def _():
        zero_ref[...] = jnp.zeros_like(zero_ref)

        def zero_copy(e):
            row = pl.multiple_of(jnp.maximum(last_ref[e], 0) * parts, block)
            return pltpu.make_async_copy(zero_ref, xs_ref.at[pl.ds(row, block)], sem)

        def block_copy(i):
            return pltpu.make_async_copy(zero_ref, xs_ref.at[pl.ds(pl.multiple_of(i * block, block), block)], sem)

        n_used = last_ref[N_EXPERTS]
        n_blocks = xs_ref.shape[0] // block
        lax.fori_loop(n_used, n_blocks, lambda i, c: (block_copy(i).start(), c)[1], 0)
        for e in range(N_EXPERTS):
            pl.when(last_ref[e] >= 0)(lambda e=e: zero_copy(e).start())
        for e in range(N_EXPERTS):
            pl.when(last_ref[e] >= 0)(lambda e=e: zero_copy(e).wait())
        lax.fori_loop(n_used, n_blocks, lambda i, c: (block_copy(i).wait(), c)[1], 0)

    def issue(t, carry):
        for k in range(MOE_TOP_K):
            s = slot_ref[0, 0, t * MOE_TOP_K + k]
            src = h_ref.at[pl.ds(pl.multiple_of(t * parts, parts), parts)]
            dst = xs_ref.at[pl.ds(pl.multiple_of(s * parts, parts), parts)]
            pltpu.make_async_copy(src, dst, sem).start(priority=k % 2)
        return carry

    lax.fori_loop(0, tm, issue, 0, unroll=DMA_ISSUE_UNROLL)
    for k in range(MOE_TOP_K):
        pltpu.make_async_copy(h_ref, xs_ref.at[pl.ds(0, tm * parts)], sem).wait()


def moe_dispatch(h, slot_tiles, last_block, n_blocks, n_tok):
    parts = h.shape[0] // n_tok
    tm = slot_tiles.shape[-1] // MOE_TOP_K
    return pl.pallas_call(
        functools.partial(_dispatch_kernel, parts=parts),
        grid=(n_tok // tm,),
        in_specs=[pl.BlockSpec((1, 1, tm * MOE_TOP_K), lambda i: (i, 0, 0), memory_space=pltpu.SMEM),
                  pl.BlockSpec(memory_space=pltpu.SMEM),
                  pl.BlockSpec((tm * parts, LANES), lambda i: (i, 0))],
        out_specs=pl.BlockSpec(memory_space=pl.ANY),
        out_shape=jax.ShapeDtypeStruct((n_blocks * MOE_TILE * parts, LANES), F32),
        scratch_shapes=[pltpu.VMEM((MOE_TILE * parts, LANES), F32), pltpu.SemaphoreType.DMA(())],
        compiler_params=_params(("arbitrary",)),
        name="moe_dispatch",
    )(slot_tiles, last_block, h)


def _combine_kernel(slot_ref, next_slot_ref, x_ref, gate_ref, mod_ref, *rest, final):
    ys_ref, o_ref, ybuf_ref, sems = rest[-4:]
    tm, d = x_ref.shape[1:]
    parts = d // LANES
    g = pl.program_id(0) * pl.num_programs(1) + pl.program_id(1)
    n_steps = pl.num_programs(0) * pl.num_programs(1)

    def fetch(slots, buf):
        def issue(t, carry):
            for k in range(MOE_TOP_K):
                s = slots[0, 0, t * MOE_TOP_K + k]
                src = ys_ref.at[pl.ds(pl.multiple_of(s * parts, parts), parts)]
                dst = ybuf_ref.at[buf * MOE_TOP_K + k, pl.ds(pl.multiple_of(t * parts, parts), parts)]
                pltpu.make_async_copy(src, dst, sems.at[buf]).start(priority=k % 2)
            return carry

        lax.fori_loop(0, tm, issue, 0, unroll=DMA_ISSUE_UNROLL)

    def combine(buf):
        for k in range(MOE_TOP_K):
            pltpu.make_async_copy(ys_ref.at[pl.ds(0, tm * parts)], ybuf_ref.at[buf * MOE_TOP_K + k], sems.at[buf]).wait()
        gate = gate_ref[0]
        y = None
        for k in range(MOE_TOP_K):
            term = gate[:, k:k + 1] * _load_row_tiled(ybuf_ref, tm, d, (buf * MOE_TOP_K + k,))
            y = term if y is None else y + term
        x = x_ref[0] + mod_ref[0, 5:6, :] * y
        if final:
            fw_ref = rest[0]
            x = x * lax.rsqrt(jnp.mean(x * x, axis=-1, keepdims=True) + RMS_EPS) * fw_ref[...]
        o_ref[0] = x

    pl.when(g == 0)(lambda: fetch(slot_ref, 0))
    for buf in range(2):
        pl.when((g + 1 < n_steps) & ((g + 1) % 2 == buf))(lambda buf=buf: fetch(next_slot_ref, buf))
    for buf in range(2):
        pl.when(g % 2 == buf)(lambda buf=buf: combine(buf))


def moe_combine(xc, ys, slot_tiles, gates, mod, n_lat, final_w=None):
    bsz, nt, d = xc.shape
    tm = slot_tiles.shape[-1] // MOE_TOP_K
    tiles = nt // tm
    last = bsz * tiles - 1
    slot_spec = lambda ahead: pl.BlockSpec((1, 1, tm * MOE_TOP_K),
                                           lambda b, j: (jnp.minimum(b * tiles + j + ahead, last), 0, 0),
                                           memory_space=pltpu.SMEM)
    in_specs = [slot_spec(0), slot_spec(1),
                pl.BlockSpec((1, tm, d), lambda b, j: (b, j, 0)),
                pl.BlockSpec((1, tm, LANES), lambda b, j: (b, j, 0)),
                pl.BlockSpec((1, SUBLANES, d), _mod_index(n_lat // tm, bsz))]
    args = [slot_tiles, slot_tiles, xc, gates, mod]
    if final_w is not None:
        in_specs.append(pl.BlockSpec((1, d), lambda b, j: (0, 0)))
        args.append(final_w.reshape(1, d))
    in_specs.append(pl.BlockSpec(memory_space=pl.ANY))
    args.append(ys)
    return pl.pallas_call(
        functools.partial(_combine_kernel, final=final_w is not None),
        grid=(bsz, tiles),
        in_specs=in_specs,
        out_specs=pl.BlockSpec((1, tm, d), lambda b, j: (b, j, 0)),
        out_shape=jax.ShapeDtypeStruct((bsz, nt, d), F32),
        scratch_shapes=[pltpu.VMEM((2 * MOE_TOP_K, tm * (d // LANES), LANES), F32), pltpu.SemaphoreType.DMA((2,))],
        compiler_params=_params(("arbitrary", "arbitrary")),
        name="moe_combine",
    )(*args)


def moe_layer(xc, mod, norm_w, w_router, b_router, experts, n_lat, final_w=None):
    bsz, nt, d = xc.shape
    n_tok = bsz * nt
    h, idx, gates, rank, counts = router(xc, mod, norm_w, w_router, b_router, n_lat)
    top = lambda a: a.reshape(n_tok, LANES)[:, :MOE_TOP_K]
    slot_tiles, block_e, n_used, last_block, n_blocks = _moe_plan(top(idx), top(rank), counts[0, :N_EXPERTS], n_tok,
                                                                   _token_tile(n_lat, nt))
    xs = moe_dispatch(h, slot_tiles, last_block, n_blocks, n_tok)
    ys = moe_experts(xs, block_e, n_used, *experts)
    return moe_combine(xc, ys, slot_tiles, gates, mod, n_lat, final_w)


def kernel(x, c, ctx, c_ctx, ada_w, ada_b, norm_w, final_norm_w, ev_w_in, ev_w_out, s5_lam_re, s5_lam_im, s5_log_step, s5_b_re, s5_b_im, s5_c_re, s5_c_im, s5_d, s5_w_glu, ret_log_decay, na_w_qkv, na_w_o, na_rpb, moe_w_router, moe_b_router, moe_w1, moe_b1, moe_w2, moe_b2):
    bsz, n_lat, d = x.shape
    nt = n_lat + ctx.shape[1]
    depth = ada_w.shape[0]
    mod = adaln_table(c, c_ctx, ada_w, ada_b)
    streams = (x, ctx)
    for i in range(depth):
        last = i == depth - 1
        j = i // 2
        if i % 2 == 0:
            proj = norm_proj(streams, mod[i], norm_w[i, 0], ev_w_in[j], n_lat, nt, 0)
            s5p = _s5_layout(s5_lam_re[j], s5_lam_im[j], s5_log_step[j], s5_b_re[j], s5_b_im[j],
                             s5_c_re[j], s5_c_im[j])
            y_s5 = s5_scan(proj, s5p, n_lat)
            o_ret = retention(proj, ret_log_decay[j], n_lat)
            xc = mix_out(streams, proj, y_s5, o_ret, mod[i], s5_d[j], s5_w_glu[j], ev_w_out[j], n_lat)
        else:
            qkv = norm_proj(streams, mod[i], norm_w[i, 0], na_w_qkv[j], n_lat, nt, 0)
            att = na_attention(qkv, na_rpb[j], n_lat)
            assert last, "an odd layer is only supported as the final layer (no context output needed)"
            xc = proj_residual(att, streams[0], mod[i], na_w_o[j])
        experts = (moe_w1, moe_b1, moe_w2, moe_b2, i)
        if last:
            return moe_layer(xc[:, :n_lat], mod[i], norm_w[i, 1], moe_w_router[i], moe_b_router[i],
                             experts, n_lat, final_norm_w)
        streams = (moe_layer(xc, mod[i], norm_w[i, 1], moe_w_router[i], moe_b_router[i], experts, n_lat),)
```

```python
import functools

import numpy as np
import jax
import jax.numpy as jnp
from jax import lax
from jax.experimental import pallas as pl
from jax.experimental.pallas import tpu as pltpu

F32 = jnp.float32
BF16 = jnp.bfloat16

GRID_W = 64
RMS_EPS = 1e-6
S5_WIDTH = 512
S5_GROUP = 16
S5_GROUPS = S5_WIDTH // S5_GROUP
S5_STATE = 64
RET_HEADS = 4
RET_HEAD_DIM = 128
RET_WIDTH = RET_HEADS * RET_HEAD_DIM
ROPE_BASE = 10000.0
NA_HEADS = 16
NA_HEAD_DIM = 64
NA_WIDTH = NA_HEADS * NA_HEAD_DIM
NA_WIN_ROWS = 8
NA_WIN_COLS = 16
N_EXPERTS = 32
MOE_TOP_K = 4
SWIGLU_LIMIT = 7.0
SWIGLU_ALPHA = 1.702

LANES = 128
SUBLANES = 8
MXU_DIM = 256
V7X_VMEM_BYTES = 64 * 1024 * 1024
VMEM_LIMIT = V7X_VMEM_BYTES * 7 // 8

MAX_TOKEN_TILE = 1024
COMBINE_TILE = 512
S5_CHUNK = 256
RET_CHUNK = 256
MOE_TILE = 512
DMA_ISSUE_UNROLL = 8
NA_Q_ROWS = 4
NA_BAND_ROWS = 12
NA_SOFTMAX_ROWS = 16
NA_SCORE_SLOTS = 2
assert NA_BAND_ROWS >= NA_WIN_ROWS + NA_Q_ROWS - 1 and (NA_BAND_ROWS - NA_Q_ROWS) % NA_Q_ROWS == 0
MASK_VALUE = -1e30

S5_SLICES = 8
assert S5_SLICES * MXU_DIM == S5_GROUPS * S5_STATE


def _params(sem):
    return pltpu.CompilerParams(dimension_semantics=sem, vmem_limit_bytes=VMEM_LIMIT)


def _adaln_kernel(c_ref, w_ref, b_ref, o_ref):
    c = c_ref[...]
    s = c * jax.nn.sigmoid(c)
    o_ref[0] = jnp.dot(s, w_ref[0], preferred_element_type=F32,
                       precision=lax.Precision.HIGHEST) + b_ref[0]


def adaln_table(c, c_ctx, ada_w, ada_b):
    depth, d, d6 = ada_w.shape
    bsz = c.shape[0]
    cond = jnp.concatenate([c, c_ctx[None, :]], axis=0)
    cond = jnp.pad(cond, ((0, SUBLANES - (bsz + 1)), (0, 0)))
    tn = d6 // 4
    out = pl.pallas_call(
        _adaln_kernel,
        grid=(depth, d6 // tn),
        in_specs=[pl.BlockSpec((SUBLANES, d), lambda i, j: (0, 0)),
                  pl.BlockSpec((1, d, tn), lambda i, j: (i, 0, j)),
                  pl.BlockSpec((1, 1, tn), lambda i, j: (i, 0, j))],
        out_specs=pl.BlockSpec((1, SUBLANES, tn), lambda i, j: (i, 0, j)),
        out_shape=jax.ShapeDtypeStruct((depth, SUBLANES, d6), F32),
        compiler_params=_params(("arbitrary", "arbitrary")),
        name="adaln",
    )(cond, ada_w, ada_b.reshape(depth, 1, d6))
    tab = out[:, :bsz + 1].reshape(depth, bsz + 1, 6, d)
    return jnp.pad(tab, ((0, 0), (0, 0), (0, 2), (0, 0)))


def _norm_mod(x, nw, shift, scale):
    y = x * lax.rsqrt(jnp.mean(x * x, axis=-1, keepdims=True) + RMS_EPS)
    return (y * nw) * (1 + scale) + shift


def _token_tile(n_lat, nt, cap=MAX_TOKEN_TILE):
    tm = cap
    while n_lat % tm or (nt - n_lat) % tm:
        tm //= 2
    return tm


def _mod_index(n_lat_tiles, bsz):
    def index(b, j):
        return (jnp.where(j >= n_lat_tiles, bsz, b), 0, 0)
    return index


def _stream_specs(streams, tm, n_lat):
    d = streams[0].shape[-1]
    lat_tiles = n_lat // tm
    if len(streams) == 1:
        return [pl.BlockSpec((1, tm, d), lambda b, j: (b, j, 0))]
    return [pl.BlockSpec((1, tm, d), lambda b, j: (b, jnp.minimum(j, lat_tiles - 1), 0)),
            pl.BlockSpec((1, tm, d), lambda b, j: (b, jnp.maximum(j - lat_tiles, 0), 0))]


def _stream_tile(refs, lat_tiles):
    if len(refs) == 1:
        return refs[0][0]
    return jnp.where(pl.program_id(1) >= lat_tiles, refs[1][0], refs[0][0])


def _proj_kernel(*refs, shift_row, n_streams, lat_tiles):
    mod_ref, nw_ref, w_ref, o_ref = refs[n_streams:]
    h = _norm_mod(_stream_tile(refs[:n_streams], lat_tiles), nw_ref[...], mod_ref[0, shift_row:shift_row + 1, :],
                  mod_ref[0, shift_row + 1:shift_row + 2, :])
    o_ref[0] = jnp.dot(h.astype(BF16), w_ref[...], preferred_element_type=F32).astype(o_ref.dtype)


def norm_proj(streams, mod, norm_w, w, n_lat, nt, shift_row):
    bsz, _, d = streams[0].shape
    n = w.shape[1]
    tm = _token_tile(n_lat, nt)
    return pl.pallas_call(
        functools.partial(_proj_kernel, shift_row=shift_row, n_streams=len(streams), lat_tiles=n_lat // tm),
        grid=(bsz, nt // tm),
        in_specs=_stream_specs(streams, tm, n_lat) + [
            pl.BlockSpec((1, SUBLANES, d), _mod_index(n_lat // tm, bsz)),
            pl.BlockSpec((1, d), lambda b, j: (0, 0)),
            pl.BlockSpec((d, n), lambda b, j: (0, 0))],
        out_specs=pl.BlockSpec((1, tm, n), lambda b, j: (b, j, 0)),
        out_shape=jax.ShapeDtypeStruct((bsz, nt, n), BF16),
        compiler_params=_params(("arbitrary", "arbitrary")),
        name="norm_proj",
    )(*streams, mod, norm_w.reshape(1, d), w.astype(BF16))


def _s5_discretize(lam_re, lam_im, log_step, b_re, b_im):
    lam_re = jnp.minimum(lam_re.astype(F32), -1e-4)
    lam_im = lam_im.astype(F32)
    step = jnp.exp(log_step.astype(F32))[..., None]
    mag = jnp.exp(lam_re * step)
    ang = lam_im * step
    a_re, a_im = mag * jnp.cos(ang), mag * jnp.sin(ang)
    den = lam_re * lam_re + lam_im * lam_im
    n_re, n_im = a_re - 1.0, a_im
    co_re = (n_re * lam_re + n_im * lam_im) / den
    co_im = (n_im * lam_re - n_re * lam_im) / den
    b_re, b_im = b_re.astype(F32), b_im.astype(F32)
    bb_re = co_re[..., None] * b_re - co_im[..., None] * b_im
    bb_im = co_re[..., None] * b_im + co_im[..., None] * b_re
    return a_re, a_im, bb_re, bb_im


def _s5_layout(lam_re, lam_im, log_step, b_re, b_im, c_re, c_im):
    a_re, a_im, bb_re, bb_im = _s5_discretize(lam_re, lam_im, log_step, b_re, b_im)
    gh = S5_GROUPS // 2
    nq = S5_STATE // S5_GROUP
    replicate = jnp.asarray(np.tile(np.eye(S5_GROUP, dtype=np.float32), (1, gh)))
    blk = np.arange(MXU_DIM) // S5_GROUP
    own_group = jnp.asarray(blk[:, None] == blk[None, :])

    def block_diag(t):
        full = jnp.einsum('dkrc,cl->dkrl', t, replicate, precision=lax.Precision.HIGHEST)
        return jnp.where(own_group, full, 0.0).astype(BF16)

    def arrange_a(a):
        a = a.reshape(2, 2, gh, nq, S5_GROUP)
        return a.transpose(0, 1, 3, 2, 4).reshape(2, S5_SLICES, MXU_DIM)

    def arrange_b(bb):
        bb = bb.reshape(2, 2, gh, nq, S5_GROUP, S5_GROUP)
        return block_diag(bb.transpose(0, 1, 3, 2, 5, 4).reshape(2, S5_SLICES, MXU_DIM, S5_GROUP))

    def arrange_c(c):
        c = c.astype(F32).reshape(2, 2, gh, S5_GROUP, nq, S5_GROUP)
        return block_diag(c.transpose(0, 1, 4, 2, 5, 3).reshape(2, S5_SLICES, MXU_DIM, S5_GROUP))

    return (arrange_a(a_re), arrange_a(a_im), arrange_b(bb_re), arrange_b(bb_im),
            arrange_c(c_re), arrange_c(-c_im.astype(F32)))


def _s5_kernel(u_ref, bre_ref, bim_ref, cre_ref, cim_ref, are_ref, aim_ref, y_ref, bu_ref, st_ref, h_ref, *, nb):
    t_chunk = S5_CHUNK
    half = MXU_DIM // LANES
    d = pl.program_id(0)

    @pl.when(pl.program_id(1) == 0)
    def _():
        h_ref[...] = jnp.zeros_like(h_ref)

    def slab(b, c, lh):
        return (b * 2 + c) * half + lh

    for b in range(nb):
        for h in range(2):
            ub = u_ref[b, :, h * MXU_DIM:(h + 1) * MXU_DIM]
            for q in range(S5_SLICES // 2):
                k = h * (S5_SLICES // 2) + q
                for c, w_ref in ((0, bre_ref), (1, bim_ref)):
                    r = jnp.dot(ub, w_ref[0, k], preferred_element_type=F32)
                    for lh in range(half):
                        bu_ref[slab(b, c, lh), pl.ds(k, t_chunk, stride=S5_SLICES), :] = r[:, lh * LANES:(lh + 1) * LANES]

    ar = [are_ref[0, :, lh * LANES:(lh + 1) * LANES] for lh in range(half)]
    ai = [aim_ref[0, :, lh * LANES:(lh + 1) * LANES] for lh in range(half)]

    def step(t, carry):
        tt = jnp.where(d == 0, t, t_chunk - 1 - t)
        new = []
        for b in range(nb):
            for lh in range(half):
                hr, hi = carry[2 * (b * half + lh)], carry[2 * (b * half + lh) + 1]
                rows = pl.ds(pl.multiple_of(tt * S5_SLICES, S5_SLICES), S5_SLICES)
                xr = bu_ref[slab(b, 0, lh), rows, :]
                xi = bu_ref[slab(b, 1, lh), rows, :]
                nr = ar[lh] * hr - ai[lh] * hi + xr
                ni = ar[lh] * hi + ai[lh] * hr + xi
                st_ref[slab(b, 0, lh), rows, :] = nr
                st_ref[slab(b, 1, lh), rows, :] = ni
                new += [nr, ni]
        return tuple(new)

    n_state = nb * half * 2
    carry = lax.fori_loop(0, t_chunk, step, tuple(h_ref[j] for j in range(n_state)), unroll=8)
    for j in range(n_state):
        h_ref[j] = carry[j]

    for b in range(nb):
        for h in range(2):
            acc = None
            for q in range(S5_SLICES // 2):
                k = h * (S5_SLICES // 2) + q
                for c, w_ref in ((0, cre_ref), (1, cim_ref)):
                    s = jnp.concatenate(
                        [st_ref[slab(b, c, lh), pl.ds(k, t_chunk, stride=S5_SLICES), :] for lh in range(half)], axis=1)
                    term = jnp.dot(s.astype(BF16), w_ref[0, k], preferred_element_type=F32)
                    acc = term if acc is None else acc + term
            y_ref[0, b, :, h * MXU_DIM:(h + 1) * MXU_DIM] = acc.astype(y_ref.dtype)


def _scan_chunk_index(n_chunks, n_ctx_chunks):
    def chunk(d, i):
        fwd = lax.rem(i + (n_chunks - n_ctx_chunks), n_chunks)
        return jnp.where(d == 0, fwd, n_chunks - 1 - i)
    return chunk


def s5_scan(proj, s5p, n_lat):
    bsz, nt, _ = proj.shape
    a_re, a_im, bb_re, bb_im, cc_re, cc_im = s5p
    t = S5_CHUNK
    nc = nt // t
    chunk = _scan_chunk_index(nc, (nt - n_lat) // t)
    wspec = pl.BlockSpec((1, S5_SLICES, MXU_DIM, MXU_DIM), lambda d, i: (d, 0, 0, 0))
    aspec = pl.BlockSpec((1, S5_SLICES, MXU_DIM), lambda d, i: (d, 0, 0))
    n_slab = bsz * 2 * (MXU_DIM // LANES)
    return pl.pallas_call(
        functools.partial(_s5_kernel, nb=bsz),
        grid=(2, nc),
        in_specs=[pl.BlockSpec((bsz, t, S5_WIDTH), lambda d, i: (0, chunk(d, i), 0)),
                  wspec, wspec, wspec, wspec, aspec, aspec],
        out_specs=pl.BlockSpec((1, bsz, t, S5_WIDTH), lambda d, i: (d, 0, chunk(d, i), 0)),
        out_shape=jax.ShapeDtypeStruct((2, bsz, nt, S5_WIDTH), BF16),
        scratch_shapes=[pltpu.VMEM((n_slab, S5_SLICES * S5_CHUNK, LANES), F32),
                        pltpu.VMEM((n_slab, S5_SLICES * S5_CHUNK, LANES), F32),
                        pltpu.VMEM((n_slab, S5_SLICES, LANES), F32)],
        compiler_params=_params(("arbitrary", "arbitrary")),
        name="s5_scan",
    )(proj, bb_re, bb_im, cc_re, cc_im, a_re, a_im)


def _rope_tables(n_lat, n_ctx):
    half = RET_HEAD_DIM // 4
    freq = ROPE_BASE ** (-jnp.arange(half, dtype=F32) / half)
    rows = n_lat // GRID_W
    ang_r = jnp.arange(rows, dtype=F32)[:, None] * freq[None, :]
    ang_c = jnp.arange(GRID_W, dtype=F32)[:, None] * freq[None, :]

    def table(fn):
        by_row = jnp.broadcast_to(fn(ang_r)[:, None, :], (rows, GRID_W, half))
        by_col = jnp.broadcast_to(fn(ang_c)[None, :, :], (rows, GRID_W, half))
        return jnp.concatenate([by_row, by_row, by_col, by_col], axis=-1).reshape(n_lat, RET_HEAD_DIM)

    cos, sin = table(jnp.cos), table(jnp.sin)
    cos = jnp.concatenate([cos, jnp.ones((n_ctx, RET_HEAD_DIM), F32)], axis=0)
    sin = jnp.concatenate([sin, jnp.zeros((n_ctx, RET_HEAD_DIM), F32)], axis=0)
    return cos, sin


def _rope_rotation_matrix():
    blk = RET_HEAD_DIM // 2
    half = blk // 2
    r = np.zeros((RET_HEAD_DIM, RET_HEAD_DIM), np.float32)
    for base in (0, blk):
        for l in range(half):
            r[base + l + half, base + l] = -1.0
            r[base + l, base + l + half] = 1.0
    return jnp.asarray(r, BF16)


def _ret_decay(log_decay, t):
    lg = log_decay.astype(F32)
    scale = RET_HEAD_DIM ** -0.5
    idx = jnp.arange(t, dtype=F32)
    diff = idx[:, None] - idx[None, :]
    diff = jnp.stack([diff, -diff])[:, None]
    inner = jnp.where(diff >= 0, jnp.exp(lg[:, :, None, None] * jnp.maximum(diff, 0.0)), 0.0) * scale
    pos = jnp.stack([idx, t - 1.0 - idx])
    q_dec = jnp.exp(lg[:, :, None] * (pos[:, None, :] + 1.0))
    k_dec = jnp.exp(lg[:, :, None] * (t - 1.0 - pos[:, None, :])) * scale
    blk = jnp.exp(lg * t)
    bcast = lambda v: jnp.broadcast_to(v[..., None], v.shape + (RET_HEAD_DIM,))
    return inner, bcast(q_dec), bcast(k_dec), jnp.broadcast_to(blk[:, :, None, None], (2, RET_HEADS, SUBLANES, RET_HEAD_DIM))


def _ret_kernel(q_ref, k_ref, v_ref, cos_ref, sin_ref, rot_ref, inner_ref, qd_ref, kd_ref, bd_ref, o_ref, s_ref):
    @pl.when(pl.program_id(1) == 0)
    def _():
        s_ref[...] = jnp.zeros_like(s_ref)

    cos = cos_ref[...]
    sin = sin_ref[...]
    rot = rot_ref[...]
    nt_dims = (((1,), (1,)), ((), ()))
    for b in range(q_ref.shape[0]):
        for hd in range(RET_HEADS):
            sl = slice(hd * RET_HEAD_DIM, (hd + 1) * RET_HEAD_DIM)
            q = q_ref[b, :, sl]
            k = k_ref[b, :, sl]
            v = v_ref[b, :, sl]
            qr = q.astype(F32) * cos + jnp.dot(q, rot, preferred_element_type=F32) * sin
            kr = k.astype(F32) * cos + jnp.dot(k, rot, preferred_element_type=F32) * sin
            qb = qr.astype(BF16)
            att = lax.dot_general(qb, kr.astype(BF16), nt_dims, preferred_element_type=F32) * inner_ref[0, hd]
            s = s_ref[b * RET_HEADS + hd]
            o = jnp.dot(att.astype(BF16), v, preferred_element_type=F32)
            o = o + jnp.dot(qb, s.astype(BF16), preferred_element_type=F32) * qd_ref[0, hd]
            kd_t = (kr * kd_ref[0, hd]).T.astype(BF16)
            s_ref[b * RET_HEADS + hd] = bd_ref[0, hd, 0:1, :] * s + jnp.dot(kd_t, v, preferred_element_type=F32)
            o_ref[0, b, :, sl] = o.astype(o_ref.dtype)


def retention(proj, log_decay, n_lat):
    bsz, nt, _ = proj.shape
    t = RET_CHUNK
    nc = nt // t
    chunk = _scan_chunk_index(nc, (nt - n_lat) // t)
    cos, sin = _rope_tables(n_lat, nt - n_lat)
    inner, q_dec, k_dec, blk = _ret_decay(log_decay, t)
    w = RET_WIDTH

    def col(j):
        return pl.BlockSpec((bsz, t, w), lambda d, i: (0, chunk(d, i), j))

    tab = pl.BlockSpec((t, RET_HEAD_DIM), lambda d, i: (chunk(d, i), 0))
    dec = pl.BlockSpec((1, RET_HEADS, t, RET_HEAD_DIM), lambda d, i: (d, 0, 0, 0))
    return pl.pallas_call(
        _ret_kernel,
        grid=(2, nc),
        in_specs=[col(1), col(2), col(3), tab, tab,
                  pl.BlockSpec((RET_HEAD_DIM, RET_HEAD_DIM), lambda d, i: (0, 0)),
                  pl.BlockSpec((1, RET_HEADS, t, t), lambda d, i: (d, 0, 0, 0)),
                  dec, dec,
                  pl.BlockSpec((1, RET_HEADS, SUBLANES, RET_HEAD_DIM), lambda d, i: (d, 0, 0, 0))],
        out_specs=pl.BlockSpec((1, bsz, t, w), lambda d, i: (d, 0, chunk(d, i), 0)),
        out_shape=jax.ShapeDtypeStruct((2, bsz, nt, w), BF16),
        scratch_shapes=[pltpu.VMEM((bsz * RET_HEADS, RET_HEAD_DIM, RET_HEAD_DIM), F32)],
        compiler_params=_params(("arbitrary", "arbitrary")),
        name="retention",
    )(proj, proj, proj, cos, sin, _rope_rotation_matrix(), inner, q_dec, k_dec, blk)


def _mix_out_kernel(u_ref, g_ref, ys_ref, or_ref, *refs, n_streams, lat_tiles):
    mod_ref, d_ref, wglu_ref, wo_ref, o_ref = refs[n_streams:]
    y = u_ref[0].astype(F32) * d_ref[...] + ys_ref[0, 0].astype(F32) + ys_ref[1, 0].astype(F32)
    y = jax.nn.gelu(y)
    a = y * jax.nn.sigmoid(jnp.dot(y.astype(BF16), wglu_ref[...], preferred_element_type=F32))
    o = or_ref[0, 0].astype(F32) + or_ref[1, 0].astype(F32)
    heads = []
    for hd in range(RET_HEADS):
        oh = o[:, hd * RET_HEAD_DIM:(hd + 1) * RET_HEAD_DIM]
        heads.append(oh * lax.rsqrt(jnp.mean(oh * oh, axis=-1, keepdims=True) + RMS_EPS))
    r = jnp.concatenate(heads, axis=1) * jax.nn.silu(g_ref[0].astype(F32))
    m = jnp.dot(a.astype(BF16), wo_ref[:S5_WIDTH, :], preferred_element_type=F32)
    m = m + jnp.dot(r.astype(BF16), wo_ref[S5_WIDTH:, :], preferred_element_type=F32)
    o_ref[0] = _stream_tile(refs[:n_streams], lat_tiles) + mod_ref[0, 2:3, :] * m


def mix_out(streams, proj, y_s5, o_ret, mod, d_skip, w_glu, w_out, n_lat):
    bsz, nt, _ = proj.shape
    d = streams[0].shape[-1]
    tm = _token_tile(n_lat, nt)
    w = S5_WIDTH
    dirs = pl.BlockSpec((2, 1, tm, w), lambda b, j: (0, b, j, 0))
    return pl.pallas_call(
        functools.partial(_mix_out_kernel, n_streams=len(streams), lat_tiles=n_lat // tm),
        grid=(bsz, nt // tm),
        in_specs=[pl.BlockSpec((1, tm, w), lambda b, j: (b, j, 0)),
                  pl.BlockSpec((1, tm, w), lambda b, j: (b, j, 4)),
                  dirs, dirs] + _stream_specs(streams, tm, n_lat) + [
                  pl.BlockSpec((1, SUBLANES, d), _mod_index(n_lat // tm, bsz)),
                  pl.BlockSpec((1, w), lambda b, j: (0, 0)),
                  pl.BlockSpec((w, w), lambda b, j: (0, 0)),
                  pl.BlockSpec((w + RET_WIDTH, d), lambda b, j: (0, 0))],
        out_specs=pl.BlockSpec((1, tm, d), lambda b, j: (b, j, 0)),
        out_shape=jax.ShapeDtypeStruct((bsz, nt, d), F32),
        compiler_params=_params(("arbitrary", "arbitrary")),
        name="mix_out",
    )(proj, proj, y_s5, o_ret, *streams, mod, d_skip.reshape(1, w), w_glu.astype(BF16), w_out.astype(BF16))


def _na_band_start(r0, rows):
    return jnp.clip(r0 - NA_WIN_ROWS // 2, 0, rows - NA_BAND_ROWS)


def _na_bias(rpb, rows):
    w = GRID_W
    n_var = (NA_BAND_ROWS - NA_Q_ROWS) // NA_Q_ROWS + 1
    half = NA_WIN_ROWS // 2
    r0 = np.array([v * NA_Q_ROWS if v * NA_Q_ROWS <= half else rows - NA_BAND_ROWS + v * NA_Q_ROWS
                   for v in range(n_var)])
    bs = np.clip(r0 - half, 0, rows - NA_BAND_ROWS)
    assert list(r0 - bs) == [v * NA_Q_ROWS for v in range(n_var)]
    r = r0[:, None] + np.arange(NA_Q_ROWS)[None, :]
    rs = np.clip(r - half, 0, rows - NA_WIN_ROWS)
    a = bs[:, None] + np.arange(NA_BAND_ROWS)[None, :]
    row_ok = (a[:, None, :] >= rs[:, :, None]) & (a[:, None, :] < rs[:, :, None] + NA_WIN_ROWS)
    row_off = np.clip(a[:, None, :] - r[:, :, None] + (NA_WIN_ROWS - 1), 0, 2 * NA_WIN_ROWS - 2)
    col = np.arange(w)
    col_start = np.clip(col - NA_WIN_COLS // 2, 0, w - NA_WIN_COLS)
    col_ok = (col[None, :] >= col_start[:, None]) & (col[None, :] < col_start[:, None] + NA_WIN_COLS)
    col_off = col[None, :] - col[:, None] + (NA_WIN_COLS - 1)
    col_sel = (col_off[:, :, None] == np.arange(2 * NA_WIN_COLS - 1)) & col_ok[:, :, None]
    tiles = jnp.sum(rpb.astype(F32)[:, :, None, None, :] * jnp.asarray(col_sel, F32)[None, None], axis=-1)
    tiles = jnp.where(jnp.asarray(col_ok)[None, None], tiles, MASK_VALUE)
    masked = jnp.full((NA_HEADS, w, w), MASK_VALUE, F32)
    groups = []
    for v in range(n_var):
        per_row = []
        for rr in range(NA_Q_ROWS):
            per_row.append(jnp.concatenate(
                [tiles[:, int(row_off[v, rr, i])] if row_ok[v, rr, i] else masked for i in range(NA_BAND_ROWS)],
                axis=-1))
        groups.append(jnp.concatenate(per_row, axis=1))
    return jnp.stack(groups)


def _na_kernel(q_ref, kb_ref, vb_ref, kc_ref, vc_ref, bias_ref, o_ref, s_ref, p_ref):
    scale = NA_HEAD_DIM ** -0.5
    nt_dims = (((1,), (1,)), ((), ()))
    nq = q_ref.shape[1]
    n_ctx = kc_ref.shape[1]
    n_slot = s_ref.shape[0]
    heads_per_tile = LANES // NA_HEAD_DIM
    lane = lax.broadcasted_iota(jnp.int32, (nq, LANES), 1)
    for j in range(NA_WIDTH // LANES):
        sl = slice(j * LANES, (j + 1) * LANES)
        q2 = q_ref[0, :, sl].astype(F32) * scale
        k2 = kb_ref[0, :, sl]
        v2 = vb_ref[0, :, sl]
        kc2 = kc_ref[0, :, sl]
        vc2 = vc_ref[0, :, sl]
        qm = jnp.concatenate(
            [jnp.where((lane >= hh * NA_HEAD_DIM) & (lane < (hh + 1) * NA_HEAD_DIM), q2, 0.0).astype(BF16)
             for hh in range(heads_per_tile)], axis=0)
        slot = j % n_slot
        s_ref[slot, :, :n_ctx] = lax.dot_general(qm, kc2, nt_dims, preferred_element_type=F32)
        s_loc = lax.dot_general(qm, k2, nt_dims, preferred_element_type=F32)
        for hh in range(heads_per_tile):
            rows = slice(hh * nq, (hh + 1) * nq)
            s_ref[slot, rows, n_ctx:] = s_loc[rows] + bias_ref[0, j * heads_per_tile + hh]
        inv = []
        for c in range(heads_per_tile * nq // NA_SOFTMAX_ROWS):
            rows = slice(c * NA_SOFTMAX_ROWS, (c + 1) * NA_SOFTMAX_ROWS)
            s = s_ref[slot, rows, :]
            p = jnp.exp(s - jnp.max(s, axis=-1, keepdims=True))
            inv.append(1.0 / jnp.sum(p, axis=-1, keepdims=True))
            p_ref[slot, rows, :] = p.astype(BF16)
        o = jnp.dot(p_ref[slot, :, :n_ctx], vc2, preferred_element_type=F32)
        o = o + jnp.dot(p_ref[slot, :, n_ctx:], v2, preferred_element_type=F32)
        o = o * jnp.concatenate(inv, axis=0)
        o2 = o[:nq]
        for hh in range(1, heads_per_tile):
            o2 = jnp.where(lane >= hh * NA_HEAD_DIM, o[hh * nq:(hh + 1) * nq], o2)
        o_ref[0, :, sl] = o2.astype(o_ref.dtype)


def na_attention(qkv, rpb, n_lat):
    bsz, nt, _ = qkv.shape
    w = GRID_W
    rows = n_lat // w
    n_ctx = nt - n_lat
    nq = NA_Q_ROWS * w
    band = NA_BAND_ROWS * w
    n_all = n_ctx + band
    stacked = nq * (LANES // NA_HEAD_DIM)
    assert n_ctx % LANES == 0 and band % LANES == 0 and rows % NA_Q_ROWS == 0 and rows >= NA_BAND_ROWS + NA_WIN_ROWS // 2

    def band_start(g):
        return _na_band_start(g * NA_Q_ROWS, rows)

    def band_spec(j):
        return pl.BlockSpec((pl.Element(1), pl.Element(band), pl.Element(NA_WIDTH)),
                            lambda b, g: (b, band_start(g) * w, j * NA_WIDTH))

    def ctx_spec(j):
        return pl.BlockSpec((1, n_ctx, NA_WIDTH), lambda b, g: (b, n_lat // n_ctx, j))

    return pl.pallas_call(
        _na_kernel,
        grid=(bsz, rows // NA_Q_ROWS),
        in_specs=[pl.BlockSpec((1, nq, NA_WIDTH), lambda b, g: (b, g, 0)),
                  band_spec(1), band_spec(2), ctx_spec(1), ctx_spec(2),
                  pl.BlockSpec((1, NA_HEADS, nq, band), lambda b, g: (g - band_start(g) // NA_Q_ROWS, 0, 0, 0))],
        out_specs=pl.BlockSpec((1, nq, NA_WIDTH), lambda b, g: (b, g, 0)),
        out_shape=jax.ShapeDtypeStruct((bsz, n_lat, NA_WIDTH), BF16),
        scratch_shapes=[pltpu.VMEM((NA_SCORE_SLOTS, stacked, n_all), F32),
                        pltpu.VMEM((NA_SCORE_SLOTS, stacked, n_all), BF16)],
        compiler_params=_params(("arbitrary", "arbitrary")),
        name="na_attention",
    )(qkv, qkv, qkv, qkv, qkv, _na_bias(rpb, rows))


def _proj_res_kernel(a_ref, x_ref, mod_ref, w_ref, o_ref):
    m = jnp.dot(a_ref[0], w_ref[...], preferred_element_type=F32)
    o_ref[0] = x_ref[0] + mod_ref[0, 2:3, :] * m


def proj_residual(a, x, mod, w):
    bsz, n, k = a.shape
    d = x.shape[-1]
    tm = _token_tile(n, n)
    return pl.pallas_call(
        _proj_res_kernel,
        grid=(bsz, n // tm),
        in_specs=[pl.BlockSpec((1, tm, k), lambda b, j: (b, j, 0)),
                  pl.BlockSpec((1, tm, d), lambda b, j: (b, j, 0)),
                  pl.BlockSpec((1, SUBLANES, d), lambda b, j: (b, 0, 0)),
                  pl.BlockSpec((k, d), lambda b, j: (0, 0))],
        out_specs=pl.BlockSpec((1, tm, d), lambda b, j: (b, j, 0)),
        out_shape=jax.ShapeDtypeStruct((bsz, n, d), F32),
        compiler_params=_params(("arbitrary", "arbitrary")),
        name="proj_residual",
    )(a, x, mod, w.astype(BF16))


def _store_row_tiled(ref, value, index=()):
    rows, d = value.shape
    parts = d // LANES
    for j in range(parts):
        ref[index + (pl.ds(j, rows, stride=parts), slice(None))] = value[:, j * LANES:(j + 1) * LANES]


def _load_row_tiled(ref, rows, d, index=()):
    parts = d // LANES
    return jnp.concatenate([ref[index + (pl.ds(j, rows, stride=parts), slice(None))] for j in range(parts)], axis=1)


def _router_kernel(x_ref, mod_ref, nw_ref, wr_ref, br_ref, tri_ref, h_ref, idx_ref, gate_ref, rank_ref, cnt_ref,
                   run_ref):
    first_step = (pl.program_id(0) == 0) & (pl.program_id(1) == 0)

    @pl.when(first_step)
    def _():
        run_ref[...] = jnp.zeros_like(run_ref)

    h = _norm_mod(x_ref[0], nw_ref[...], mod_ref[0, 3:4, :], mod_ref[0, 4:5, :])
    _store_row_tiled(h_ref, h)
    logits = jnp.dot(h.astype(BF16), wr_ref[...], preferred_element_type=F32) + br_ref[...]
    lane = lax.broadcasted_iota(jnp.int32, logits.shape, 1)
    lane_f = lane.astype(F32)
    vals, idxs = [], []
    cur = logits
    for _ in range(MOE_TOP_K):
        m = jnp.max(cur, axis=-1, keepdims=True)
        first = jnp.min(jnp.where(cur == m, lane_f, float(LANES)), axis=-1, keepdims=True)
        vals.append(m)
        idxs.append(first)
        cur = jnp.where(lane_f == first, MASK_VALUE, cur)
    es = [jnp.exp(v - vals[0]) for v in vals]
    tot = es[0]
    for e in es[1:]:
        tot = tot + e
    onehots = [(lane_f == idxs[k]).astype(F32) for k in range(MOE_TOP_K)]
    multi = onehots[0]
    for oh in onehots[1:]:
        multi = multi + oh
    before = jnp.dot(tri_ref[...], multi.astype(BF16), preferred_element_type=F32) + run_ref[0:1, :]
    idx_out = jnp.zeros(logits.shape, F32)
    gate_out = jnp.zeros(logits.shape, F32)
    rank_out = jnp.zeros(logits.shape, F32)
    for k in range(MOE_TOP_K):
        idx_out = jnp.where(lane == k, idxs[k], idx_out)
        gate_out = jnp.where(lane == k, es[k] / tot, gate_out)
        rank_out = jnp.where(lane == k, jnp.sum(before * onehots[k], axis=-1, keepdims=True), rank_out)
    idx_ref[0] = idx_out.astype(jnp.int32)
    gate_ref[0] = gate_out
    rank_ref[0] = rank_out.astype(jnp.int32)
    run_ref[0:1, :] = run_ref[0:1, :] + jnp.sum(multi, axis=0, keepdims=True)
    cnt_ref[...] = jnp.broadcast_to(run_ref[0:1, :], cnt_ref.shape)


def router(xc, mod, norm_w, w_router, b_router, n_lat):
    bsz, nt, d = xc.shape
    tm = _token_tile(n_lat, nt)
    tiles = nt // tm
    parts = d // LANES
    wr = jnp.pad(w_router, ((0, 0), (0, LANES - N_EXPERTS))).astype(BF16)
    br = jnp.pad(b_router.astype(F32), (0, LANES - N_EXPERTS), constant_values=MASK_VALUE).reshape(1, LANES)
    tri = jnp.tril(jnp.ones((tm, tm), BF16), k=-1)
    tok = lambda n, dt: jax.ShapeDtypeStruct((bsz, nt, n), dt)
    out = lambda n: pl.BlockSpec((1, tm, n), lambda b, j: (b, j, 0))
    return pl.pallas_call(
        _router_kernel,
        grid=(bsz, nt // tm),
        in_specs=[pl.BlockSpec((1, tm, d), lambda b, j: (b, j, 0)),
                  pl.BlockSpec((1, SUBLANES, d), _mod_index(n_lat // tm, bsz)),
                  pl.BlockSpec((1, d), lambda b, j: (0, 0)),
                  pl.BlockSpec((d, LANES), lambda b, j: (0, 0)),
                  pl.BlockSpec((1, LANES), lambda b, j: (0, 0)),
                  pl.BlockSpec((tm, tm), lambda b, j: (0, 0))],
        out_specs=[pl.BlockSpec((tm * parts, LANES), lambda b, j: (b * tiles + j, 0)),
                   out(LANES), out(LANES), out(LANES),
                   pl.BlockSpec((SUBLANES, LANES), lambda b, j: (0, 0))],
        out_shape=[jax.ShapeDtypeStruct((bsz * nt * parts, LANES), F32),
                   tok(LANES, jnp.int32), tok(LANES, F32), tok(LANES, jnp.int32),
                   jax.ShapeDtypeStruct((SUBLANES, LANES), F32)],
        scratch_shapes=[pltpu.VMEM((SUBLANES, LANES), F32)],
        compiler_params=_params(("arbitrary", "arbitrary")),
        name="router",
    )(xc, mod, norm_w.reshape(1, d), wr, br, tri)


def _moe_kernel(be_ref, nu_ref, x_ref, w1_ref, b1_ref, w2_ref, b2_ref, o_ref, w1b_ref, w2b_ref):
    i = pl.program_id(0)
    ff = w2_ref.shape[2]
    used = i < nu_ref[0]

    @pl.when(used & ((i == 0) | (be_ref[i] != be_ref[jnp.maximum(i - 1, 0)])))
    def _():
        w1b_ref[...] = w1_ref[0, 0].astype(BF16)
        w2b_ref[...] = w2_ref[0, 0].astype(BF16)

    @pl.when(used)
    def _():
        d = w1b_ref.shape[0]
        x = _load_row_tiled(x_ref, MOE_TILE, d)
        t = jnp.dot(x.astype(BF16), w1b_ref[...], preferred_element_type=F32) + b1_ref[0, 0]
        x_glu = jnp.minimum(t[:, :ff], SWIGLU_LIMIT)
        x_lin = jnp.clip(t[:, ff:], -SWIGLU_LIMIT, SWIGLU_LIMIT)
        act = x_glu * jax.nn.sigmoid(SWIGLU_ALPHA * x_glu) * (x_lin + 1)
        y = jnp.dot(act.astype(BF16), w2b_ref[...], preferred_element_type=F32) + b2_ref[0, 0]
        _store_row_tiled(o_ref, y)

    @pl.when(jnp.logical_not(used))
    def _():
        o_ref[...] = jnp.zeros_like(o_ref)


def moe_experts(xs, block_e, n_used, w1, b1, w2, b2, layer):
    depth, ne, d, ff2 = w1.shape
    ff = w2.shape[2]
    tm = MOE_TILE * (d // LANES)
    nb = xs.shape[0] // tm
    grid_spec = pltpu.PrefetchScalarGridSpec(
        num_scalar_prefetch=2,
        grid=(nb,),
        in_specs=[pl.BlockSpec((tm, LANES), lambda i, be, nu: (jnp.where(i < nu[0], i, 0), 0)),
                  pl.BlockSpec((1, 1, d, ff2), lambda i, be, nu: (layer, be[i], 0, 0)),
                  pl.BlockSpec((1, 1, 1, ff2), lambda i, be, nu: (layer, be[i], 0, 0)),
                  pl.BlockSpec((1, 1, ff, d), lambda i, be, nu: (layer, be[i], 0, 0)),
                  pl.BlockSpec((1, 1, 1, d), lambda i, be, nu: (layer, be[i], 0, 0))],
        out_specs=pl.BlockSpec((tm, LANES), lambda i, be, nu: (i, 0)),
        scratch_shapes=[pltpu.VMEM((d, ff2), BF16), pltpu.VMEM((ff, d), BF16)],
    )
    return pl.pallas_call(
        _moe_kernel,
        grid_spec=grid_spec,
        out_shape=jax.ShapeDtypeStruct(xs.shape, F32),
        compiler_params=_params(("arbitrary",)),
        name="moe_experts",
    )(block_e, n_used, xs, w1, b1.reshape(depth, ne, 1, ff2), w2, b2.reshape(depth, ne, 1, d))


def _moe_plan(idx, rank, counts, n_tok):
    counts = counts.astype(jnp.int32)
    padded = (counts + MOE_TILE - 1) // MOE_TILE * MOE_TILE
    padded_ends = jnp.cumsum(padded)
    padded_starts = padded_ends - padded
    onehot = idx[..., None] == jnp.arange(N_EXPERTS, dtype=jnp.int32)
    slot_of = rank + jnp.sum(jnp.where(onehot, padded_starts, 0), axis=-1)
    n_blocks = -(-n_tok * MOE_TOP_K // MOE_TILE) + N_EXPERTS
    block_row = jnp.arange(n_blocks, dtype=jnp.int32) * MOE_TILE
    block_e = jnp.minimum(jnp.sum(padded_ends[None, :] <= block_row[:, None], axis=1), N_EXPERTS - 1)
    n_used = padded_ends[-1:] // MOE_TILE
    last_block = jnp.concatenate([jnp.where(counts > 0, padded_ends - MOE_TILE, -1), n_used])
    return (slot_of.astype(jnp.int32), block_e.astype(jnp.int32), n_used.astype(jnp.int32),
            last_block.astype(jnp.int32), n_blocks)


def _dispatch_kernel(slot_ref, last_ref, h_ref, xs_ref, zero_ref, sem, *, parts):
    tm = h_ref.shape[0] // parts
    block = MOE_TILE * parts

    @pl.when(pl.program_id(0) == 0)
    def _():
        zero_ref[...] = jnp.zeros_like(zero_ref)

        def zero_copy(e):
            row = pl.multiple_of(jnp.maximum(last_ref[e], 0) * parts, block)
            return pltpu.make_async_copy(zero_ref, xs_ref.at[pl.ds(row, block)], sem)

        def block_copy(i):
            return pltpu.make_async_copy(zero_ref, xs_ref.at[pl.ds(pl.multiple_of(i * block, block), block)], sem)

        n_used = last_ref[N_EXPERTS]
        n_blocks = xs_ref.shape[0] // block
        lax.fori_loop(n_used, n_blocks, lambda i, c: (block_copy(i).start(), c)[1], 0)
        for e in range(N_EXPERTS):
            pl.when(last_ref[e] >= 0)(lambda e=e: zero_copy(e).start())
        for e in range(N_EXPERTS):
            pl.when(last_ref[e] >= 0)(lambda e=e: zero_copy(e).wait())
        lax.fori_loop(n_used, n_blocks, lambda i, c: (block_copy(i).wait(), c)[1], 0)

    def issue(t, carry):
        for k in range(MOE_TOP_K):
            s = slot_ref[0, 0, t * MOE_TOP_K + k]
            src = h_ref.at[pl.ds(pl.multiple_of(t * parts, parts), parts)]
            dst = xs_ref.at[pl.ds(pl.multiple_of(s * parts, parts), parts)]
            pltpu.make_async_copy(src, dst, sem).start(priority=k % 2)
        return carry

    lax.fori_loop(0, tm, issue, 0, unroll=DMA_ISSUE_UNROLL)
    for k in range(MOE_TOP_K):
        pltpu.make_async_copy(h_ref, xs_ref.at[pl.ds(0, tm * parts)], sem).wait()


def moe_dispatch(h, slot_tiles, last_block, n_blocks, n_tok):
    parts = h.shape[0] // n_tok
    tm = slot_tiles.shape[-1] // MOE_TOP_K
    return pl.pallas_call(
        functools.partial(_dispatch_kernel, parts=parts),
        grid=(n_tok // tm,),
        in_specs=[pl.BlockSpec((1, 1, tm * MOE_TOP_K), lambda i: (i, 0, 0), memory_space=pltpu.SMEM),
                  pl.BlockSpec(memory_space=pltpu.SMEM),
                  pl.BlockSpec((tm * parts, LANES), lambda i: (i, 0))],
        out_specs=pl.BlockSpec(memory_space=pl.ANY),
        out_shape=jax.ShapeDtypeStruct((n_blocks * MOE_TILE * parts, LANES), F32),
        scratch_shapes=[pltpu.VMEM((MOE_TILE * parts, LANES), F32), pltpu.SemaphoreType.DMA(())],
        compiler_params=_params(("arbitrary",)),
        name="moe_dispatch",
    )(slot_tiles, last_block, h)


def _combine_kernel(slot_ref, next_slot_ref, x_ref, gate_ref, mod_ref, *rest, final):
    ys_ref, o_ref, ybuf_ref, sems = rest[-4:]
    tm, d = x_ref.shape[1:]
    parts = d // LANES
    g = pl.program_id(0) * pl.num_programs(1) + pl.program_id(1)
    n_steps = pl.num_programs(0) * pl.num_programs(1)

    def fetch(slots, buf):
        def issue(t, carry):
            for k in range(MOE_TOP_K):
                s = slots[0, 0, t * MOE_TOP_K + k]
                src = ys_ref.at[pl.ds(pl.multiple_of(s * parts, parts), parts)]
                dst = ybuf_ref.at[buf * MOE_TOP_K + k, pl.ds(pl.multiple_of(t * parts, parts), parts)]
                pltpu.make_async_copy(src, dst, sems.at[buf]).start(priority=k % 2)
            return carry

        lax.fori_loop(0, tm, issue, 0, unroll=DMA_ISSUE_UNROLL)

    def combine(buf):
        for k in range(MOE_TOP_K):
            pltpu.make_async_copy(ys_ref.at[pl.ds(0, tm * parts)], ybuf_ref.at[buf * MOE_TOP_K + k], sems.at[buf]).wait()
        gate = gate_ref[0]
        y = None
        for k in range(MOE_TOP_K):
            term = gate[:, k:k + 1] * _load_row_tiled(ybuf_ref, tm, d, (buf * MOE_TOP_K + k,))
            y = term if y is None else y + term
        x = x_ref[0] + mod_ref[0, 5:6, :] * y
        if final:
            fw_ref = rest[0]
            x = x * lax.rsqrt(jnp.mean(x * x, axis=-1, keepdims=True) + RMS_EPS) * fw_ref[...]
        o_ref[0] = x

    pl.when(g == 0)(lambda: fetch(slot_ref, 0))
    for buf in range(2):
        pl.when((g + 1 < n_steps) & ((g + 1) % 2 == buf))(lambda buf=buf: fetch(next_slot_ref, buf))
    for buf in range(2):
        pl.when(g % 2 == buf)(lambda buf=buf: combine(buf))


def moe_combine(xc, ys, slot_tiles, gates, mod, n_lat, final_w=None):
    bsz, nt, d = xc.shape
    tm = slot_tiles.shape[-1] // MOE_TOP_K
    tiles = nt // tm
    last = bsz * tiles - 1
    slot_spec = lambda ahead: pl.BlockSpec((1, 1, tm * MOE_TOP_K),
                                           lambda b, j: (jnp.minimum(b * tiles + j + ahead, last), 0, 0),
                                           memory_space=pltpu.SMEM)
    in_specs = [slot_spec(0), slot_spec(1),
                pl.BlockSpec((1, tm, d), lambda b, j: (b, j, 0)),
                pl.BlockSpec((1, tm, LANES), lambda b, j: (b, j, 0)),
                pl.BlockSpec((1, SUBLANES, d), _mod_index(n_lat // tm, bsz))]
    args = [slot_tiles, slot_tiles, xc, gates, mod]
    if final_w is not None:
        in_specs.append(pl.BlockSpec((1, d), lambda b, j: (0, 0)))
        args.append(final_w.reshape(1, d))
    in_specs.append(pl.BlockSpec(memory_space=pl.ANY))
    args.append(ys)
    return pl.pallas_call(
        functools.partial(_combine_kernel, final=final_w is not None),
        grid=(bsz, tiles),
        in_specs=in_specs,
        out_specs=pl.BlockSpec((1, tm, d), lambda b, j: (b, j, 0)),
        out_shape=jax.ShapeDtypeStruct((bsz, nt, d), F32),
        scratch_shapes=[pltpu.VMEM((2 * MOE_TOP_K, tm * (d // LANES), LANES), F32), pltpu.SemaphoreType.DMA((2,))],
        compiler_params=_params(("arbitrary", "arbitrary")),
        name="moe_combine",
    )(*args)


def moe_layer(xc, mod, norm_w, w_router, b_router, experts, n_lat, final_w=None):
    bsz, nt, d = xc.shape
    n_tok = bsz * nt
    h, idx, gates, rank, counts = router(xc, mod, norm_w, w_router, b_router, n_lat)
    top = lambda a: a.reshape(n_tok, LANES)[:, :MOE_TOP_K]
    slot_of, block_e, n_used, last_block, n_blocks = _moe_plan(top(idx), top(rank), counts[0, :N_EXPERTS], n_tok)
    tiled = lambda tm: slot_of.reshape(n_tok // tm, 1, tm * MOE_TOP_K)
    xs = moe_dispatch(h, tiled(_token_tile(n_tok, n_tok)), last_block, n_blocks, n_tok)
    ys = moe_experts(xs, block_e, n_used, *experts)
    return moe_combine(xc, ys, tiled(_token_tile(n_lat, nt, COMBINE_TILE)), gates, mod, n_lat, final_w)


def kernel(x, c, ctx, c_ctx, ada_w, ada_b, norm_w, final_norm_w, ev_w_in, ev_w_out, s5_lam_re, s5_lam_im, s5_log_step, s5_b_re, s5_b_im, s5_c_re, s5_c_im, s5_d, s5_w_glu, ret_log_decay, na_w_qkv, na_w_o, na_rpb, moe_w_router, moe_b_router, moe_w1, moe_b1, moe_w2, moe_b2):
    bsz, n_lat, d = x.shape
    nt = n_lat + ctx.shape[1]
    depth = ada_w.shape[0]
    mod = adaln_table(c, c_ctx, ada_w, ada_b)
    streams = (x, ctx)
    for i in range(depth):
        last = i == depth - 1
        j = i // 2
        if i % 2 == 0:
            proj = norm_proj(streams, mod[i], norm_w[i, 0], ev_w_in[j], n_lat, nt, 0)
            s5p = _s5_layout(s5_lam_re[j], s5_lam_im[j], s5_log_step[j], s5_b_re[j], s5_b_im[j],
                             s5_c_re[j], s5_c_im[j])
            y_s5 = s5_scan(proj, s5p, n_lat)
            o_ret = retention(proj, ret_log_decay[j], n_lat)
            xc = mix_out(streams, proj, y_s5, o_ret, mod[i], s5_d[j], s5_w_glu[j], ev_w_out[j], n_lat)
        else:
            qkv = norm_proj(streams, mod[i], norm_w[i, 0], na_w_qkv[j], n_lat, nt, 0)
            att = na_attention(qkv, na_rpb[j], n_lat)
            assert last, "an odd layer is only supported as the final layer (no context output needed)"
            xc = proj_residual(att, streams[0], mod[i], na_w_o[j])
        experts = (moe_w1, moe_b1, moe_w2, moe_b2, i)
        if last:
            return moe_layer(xc[:, :n_lat], mod[i], norm_w[i, 1], moe_w_router[i], moe_b_router[i],
                             experts, n_lat, final_norm_w)
        streams = (moe_layer(xc, mod[i], norm_w[i, 1], moe_w_router[i], moe_b_router[i], experts, n_lat),)
```

```python
import functools

import numpy as np
import jax
import jax.numpy as jnp
from jax import lax
from jax.experimental import pallas as pl
from jax.experimental.pallas import tpu as pltpu

F32 = jnp.float32
BF16 = jnp.bfloat16

GRID_W = 64
RMS_EPS = 1e-6
S5_WIDTH = 512
S5_GROUP = 16
S5_GROUPS = S5_WIDTH // S5_GROUP
S5_STATE = 64
RET_HEADS = 4
RET_HEAD_DIM = 128
RET_WIDTH = RET_HEADS * RET_HEAD_DIM
ROPE_BASE = 10000.0
NA_HEADS = 16
NA_HEAD_DIM = 64
NA_WIDTH = NA_HEADS * NA_HEAD_DIM
NA_WIN_ROWS = 8
NA_WIN_COLS = 16
N_EXPERTS = 32
MOE_TOP_K = 4
SWIGLU_LIMIT = 7.0
SWIGLU_ALPHA = 1.702

LANES = 128
SUBLANES = 8
MXU_DIM = 256
V7X_VMEM_BYTES = 64 * 1024 * 1024
VMEM_LIMIT = V7X_VMEM_BYTES * 7 // 8

MAX_TOKEN_TILE = 1024
COMBINE_TILE = 256
S5_CHUNK = 256
RET_CHUNK = 256
MOE_TILE = 512
DMA_ISSUE_UNROLL = 8
NA_Q_ROWS = 4
NA_BAND_ROWS = 12
NA_SOFTMAX_ROWS = 16
NA_SCORE_SLOTS = 2
assert NA_BAND_ROWS >= NA_WIN_ROWS + NA_Q_ROWS - 1 and (NA_BAND_ROWS - NA_Q_ROWS) % NA_Q_ROWS == 0
MASK_VALUE = -1e30

S5_SLICES = 8
assert S5_SLICES * MXU_DIM == S5_GROUPS * S5_STATE


def _params(sem):
    return pltpu.CompilerParams(dimension_semantics=sem, vmem_limit_bytes=VMEM_LIMIT)


def _adaln_kernel(c_ref, w_ref, b_ref, o_ref):
    c = c_ref[...]
    s = c * jax.nn.sigmoid(c)
    o_ref[0] = jnp.dot(s, w_ref[0], preferred_element_type=F32,
                       precision=lax.Precision.HIGHEST) + b_ref[0]


def adaln_table(c, c_ctx, ada_w, ada_b):
    depth, d, d6 = ada_w.shape
    bsz = c.shape[0]
    cond = jnp.concatenate([c, c_ctx[None, :]], axis=0)
    cond = jnp.pad(cond, ((0, SUBLANES - (bsz + 1)), (0, 0)))
    tn = d6 // 4
    out = pl.pallas_call(
        _adaln_kernel,
        grid=(depth, d6 // tn),
        in_specs=[pl.BlockSpec((SUBLANES, d), lambda i, j: (0, 0)),
                  pl.BlockSpec((1, d, tn), lambda i, j: (i, 0, j)),
                  pl.BlockSpec((1, 1, tn), lambda i, j: (i, 0, j))],
        out_specs=pl.BlockSpec((1, SUBLANES, tn), lambda i, j: (i, 0, j)),
        out_shape=jax.ShapeDtypeStruct((depth, SUBLANES, d6), F32),
        compiler_params=_params(("arbitrary", "arbitrary")),
        name="adaln",
    )(cond, ada_w, ada_b.reshape(depth, 1, d6))
    tab = out[:, :bsz + 1].reshape(depth, bsz + 1, 6, d)
    return jnp.pad(tab, ((0, 0), (0, 0), (0, 2), (0, 0)))


def _norm_mod(x, nw, shift, scale):
    y = x * lax.rsqrt(jnp.mean(x * x, axis=-1, keepdims=True) + RMS_EPS)
    return (y * nw) * (1 + scale) + shift


def _token_tile(n_lat, nt, cap=MAX_TOKEN_TILE):
    tm = cap
    while n_lat % tm or (nt - n_lat) % tm:
        tm //= 2
    return tm


def _mod_index(n_lat_tiles, bsz):
    def index(b, j):
        return (jnp.where(j >= n_lat_tiles, bsz, b), 0, 0)
    return index


def _stream_specs(streams, tm, n_lat):
    d = streams[0].shape[-1]
    lat_tiles = n_lat // tm
    if len(streams) == 1:
        return [pl.BlockSpec((1, tm, d), lambda b, j: (b, j, 0))]
    return [pl.BlockSpec((1, tm, d), lambda b, j: (b, jnp.minimum(j, lat_tiles - 1), 0)),
            pl.BlockSpec((1, tm, d), lambda b, j: (b, jnp.maximum(j - lat_tiles, 0), 0))]


def _stream_tile(refs, lat_tiles):
    if len(refs) == 1:
        return refs[0][0]
    return jnp.where(pl.program_id(1) >= lat_tiles, refs[1][0], refs[0][0])


def _proj_kernel(*refs, shift_row, n_streams, lat_tiles):
    mod_ref, nw_ref, w_ref, o_ref = refs[n_streams:]
    h = _norm_mod(_stream_tile(refs[:n_streams], lat_tiles), nw_ref[...], mod_ref[0, shift_row:shift_row + 1, :],
                  mod_ref[0, shift_row + 1:shift_row + 2, :])
    o_ref[0] = jnp.dot(h.astype(BF16), w_ref[...], preferred_element_type=F32).astype(o_ref.dtype)


def norm_proj(streams, mod, norm_w, w, n_lat, nt, shift_row):
    bsz, _, d = streams[0].shape
    n = w.shape[1]
    tm = _token_tile(n_lat, nt)
    return pl.pallas_call(
        functools.partial(_proj_kernel, shift_row=shift_row, n_streams=len(streams), lat_tiles=n_lat // tm),
        grid=(bsz, nt // tm),
        in_specs=_stream_specs(streams, tm, n_lat) + [
            pl.BlockSpec((1, SUBLANES, d), _mod_index(n_lat // tm, bsz)),
            pl.BlockSpec((1, d), lambda b, j: (0, 0)),
            pl.BlockSpec((d, n), lambda b, j: (0, 0))],
        out_specs=pl.BlockSpec((1, tm, n), lambda b, j: (b, j, 0)),
        out_shape=jax.ShapeDtypeStruct((bsz, nt, n), BF16),
        compiler_params=_params(("arbitrary", "arbitrary")),
        name="norm_proj",
    )(*streams, mod, norm_w.reshape(1, d), w.astype(BF16))


def _s5_discretize(lam_re, lam_im, log_step, b_re, b_im):
    lam_re = jnp.minimum(lam_re.astype(F32), -1e-4)
    lam_im = lam_im.astype(F32)
    step = jnp.exp(log_step.astype(F32))[..., None]
    mag = jnp.exp(lam_re * step)
    ang = lam_im * step
    a_re, a_im = mag * jnp.cos(ang), mag * jnp.sin(ang)
    den = lam_re * lam_re + lam_im * lam_im
    n_re, n_im = a_re - 1.0, a_im
    co_re = (n_re * lam_re + n_im * lam_im) / den
    co_im = (n_im * lam_re - n_re * lam_im) / den
    b_re, b_im = b_re.astype(F32), b_im.astype(F32)
    bb_re = co_re[..., None] * b_re - co_im[..., None] * b_im
    bb_im = co_re[..., None] * b_im + co_im[..., None] * b_re
    return a_re, a_im, bb_re, bb_im


def _s5_layout(lam_re, lam_im, log_step, b_re, b_im, c_re, c_im):
    a_re, a_im, bb_re, bb_im = _s5_discretize(lam_re, lam_im, log_step, b_re, b_im)
    gh = S5_GROUPS // 2
    nq = S5_STATE // S5_GROUP
    replicate = jnp.asarray(np.tile(np.eye(S5_GROUP, dtype=np.float32), (1, gh)))
    blk = np.arange(MXU_DIM) // S5_GROUP
    own_group = jnp.asarray(blk[:, None] == blk[None, :])

    def block_diag(t):
        full = jnp.einsum('dkrc,cl->dkrl', t, replicate, precision=lax.Precision.HIGHEST)
        return jnp.where(own_group, full, 0.0).astype(BF16)

    def arrange_a(a):
        a = a.reshape(2, 2, gh, nq, S5_GROUP)
        return a.transpose(0, 1, 3, 2, 4).reshape(2, S5_SLICES, MXU_DIM)

    def arrange_b(bb):
        bb = bb.reshape(2, 2, gh, nq, S5_GROUP, S5_GROUP)
        return block_diag(bb.transpose(0, 1, 3, 2, 5, 4).reshape(2, S5_SLICES, MXU_DIM, S5_GROUP))

    def arrange_c(c):
        c = c.astype(F32).reshape(2, 2, gh, S5_GROUP, nq, S5_GROUP)
        return block_diag(c.transpose(0, 1, 4, 2, 5, 3).reshape(2, S5_SLICES, MXU_DIM, S5_GROUP))

    return (arrange_a(a_re), arrange_a(a_im), arrange_b(bb_re), arrange_b(bb_im),
            arrange_c(c_re), arrange_c(-c_im.astype(F32)))


def _s5_kernel(u_ref, bre_ref, bim_ref, cre_ref, cim_ref, are_ref, aim_ref, y_ref, bu_ref, st_ref, h_ref, *, nb):
    t_chunk = S5_CHUNK
    half = MXU_DIM // LANES
    d = pl.program_id(0)

    @pl.when(pl.program_id(1) == 0)
    def _():
        h_ref[...] = jnp.zeros_like(h_ref)

    def slab(b, c, lh):
        return (b * 2 + c) * half + lh

    for b in range(nb):
        for h in range(2):
            ub = u_ref[b, :, h * MXU_DIM:(h + 1) * MXU_DIM]
            for q in range(S5_SLICES // 2):
                k = h * (S5_SLICES // 2) + q
                for c, w_ref in ((0, bre_ref), (1, bim_ref)):
                    r = jnp.dot(ub, w_ref[0, k], preferred_element_type=F32)
                    for lh in range(half):
                        bu_ref[slab(b, c, lh), pl.ds(k, t_chunk, stride=S5_SLICES), :] = r[:, lh * LANES:(lh + 1) * LANES]

    ar = [are_ref[0, :, lh * LANES:(lh + 1) * LANES] for lh in range(half)]
    ai = [aim_ref[0, :, lh * LANES:(lh + 1) * LANES] for lh in range(half)]

    def step(t, carry):
        tt = jnp.where(d == 0, t, t_chunk - 1 - t)
        new = []
        for b in range(nb):
            for lh in range(half):
                hr, hi = carry[2 * (b * half + lh)], carry[2 * (b * half + lh) + 1]
                rows = pl.ds(pl.multiple_of(tt * S5_SLICES, S5_SLICES), S5_SLICES)
                xr = bu_ref[slab(b, 0, lh), rows, :]
                xi = bu_ref[slab(b, 1, lh), rows, :]
                nr = ar[lh] * hr - ai[lh] * hi + xr
                ni = ar[lh] * hi + ai[lh] * hr + xi
                st_ref[slab(b, 0, lh), rows, :] = nr
                st_ref[slab(b, 1, lh), rows, :] = ni
                new += [nr, ni]
        return tuple(new)

    n_state = nb * half * 2
    carry = lax.fori_loop(0, t_chunk, step, tuple(h_ref[j] for j in range(n_state)), unroll=8)
    for j in range(n_state):
        h_ref[j] = carry[j]

    for b in range(nb):
        for h in range(2):
            acc = None
            for q in range(S5_SLICES // 2):
                k = h * (S5_SLICES // 2) + q
                for c, w_ref in ((0, cre_ref), (1, cim_ref)):
                    s = jnp.concatenate(
                        [st_ref[slab(b, c, lh), pl.ds(k, t_chunk, stride=S5_SLICES), :] for lh in range(half)], axis=1)
                    term = jnp.dot(s.astype(BF16), w_ref[0, k], preferred_element_type=F32)
                    acc = term if acc is None else acc + term
            y_ref[0, b, :, h * MXU_DIM:(h + 1) * MXU_DIM] = acc.astype(y_ref.dtype)


def _scan_chunk_index(n_chunks, n_ctx_chunks):
    def chunk(d, i):
        fwd = lax.rem(i + (n_chunks - n_ctx_chunks), n_chunks)
        return jnp.where(d == 0, fwd, n_chunks - 1 - i)
    return chunk


def s5_scan(proj, s5p, n_lat):
    bsz, nt, _ = proj.shape
    a_re, a_im, bb_re, bb_im, cc_re, cc_im = s5p
    t = S5_CHUNK
    nc = nt // t
    chunk = _scan_chunk_index(nc, (nt - n_lat) // t)
    wspec = pl.BlockSpec((1, S5_SLICES, MXU_DIM, MXU_DIM), lambda d, i: (d, 0, 0, 0))
    aspec = pl.BlockSpec((1, S5_SLICES, MXU_DIM), lambda d, i: (d, 0, 0))
    n_slab = bsz * 2 * (MXU_DIM // LANES)
    return pl.pallas_call(
        functools.partial(_s5_kernel, nb=bsz),
        grid=(2, nc),
        in_specs=[pl.BlockSpec((bsz, t, S5_WIDTH), lambda d, i: (0, chunk(d, i), 0)),
                  wspec, wspec, wspec, wspec, aspec, aspec],
        out_specs=pl.BlockSpec((1, bsz, t, S5_WIDTH), lambda d, i: (d, 0, chunk(d, i), 0)),
        out_shape=jax.ShapeDtypeStruct((2, bsz, nt, S5_WIDTH), BF16),
        scratch_shapes=[pltpu.VMEM((n_slab, S5_SLICES * S5_CHUNK, LANES), F32),
                        pltpu.VMEM((n_slab, S5_SLICES * S5_CHUNK, LANES), F32),
                        pltpu.VMEM((n_slab, S5_SLICES, LANES), F32)],
        compiler_params=_params(("arbitrary", "arbitrary")),
        name="s5_scan",
    )(proj, bb_re, bb_im, cc_re, cc_im, a_re, a_im)


def _rope_tables(n_lat, n_ctx):
    half = RET_HEAD_DIM // 4
    freq = ROPE_BASE ** (-jnp.arange(half, dtype=F32) / half)
    rows = n_lat // GRID_W
    ang_r = jnp.arange(rows, dtype=F32)[:, None] * freq[None, :]
    ang_c = jnp.arange(GRID_W, dtype=F32)[:, None] * freq[None, :]

    def table(fn):
        by_row = jnp.broadcast_to(fn(ang_r)[:, None, :], (rows, GRID_W, half))
        by_col = jnp.broadcast_to(fn(ang_c)[None, :, :], (rows, GRID_W, half))
        return jnp.concatenate([by_row, by_row, by_col, by_col], axis=-1).reshape(n_lat, RET_HEAD_DIM)

    cos, sin = table(jnp.cos), table(jnp.sin)
    cos = jnp.concatenate([cos, jnp.ones((n_ctx, RET_HEAD_DIM), F32)], axis=0)
    sin = jnp.concatenate([sin, jnp.zeros((n_ctx, RET_HEAD_DIM), F32)], axis=0)
    return cos, sin


def _rope_rotation_matrix():
    blk = RET_HEAD_DIM // 2
    half = blk // 2
    r = np.zeros((RET_HEAD_DIM, RET_HEAD_DIM), np.float32)
    for base in (0, blk):
        for l in range(half):
            r[base + l + half, base + l] = -1.0
            r[base + l, base + l + half] = 1.0
    return jnp.asarray(r, BF16)


def _ret_decay(log_decay, t):
    lg = log_decay.astype(F32)
    scale = RET_HEAD_DIM ** -0.5
    idx = jnp.arange(t, dtype=F32)
    diff = idx[:, None] - idx[None, :]
    diff = jnp.stack([diff, -diff])[:, None]
    inner = jnp.where(diff >= 0, jnp.exp(lg[:, :, None, None] * jnp.maximum(diff, 0.0)), 0.0) * scale
    pos = jnp.stack([idx, t - 1.0 - idx])
    q_dec = jnp.exp(lg[:, :, None] * (pos[:, None, :] + 1.0))
    k_dec = jnp.exp(lg[:, :, None] * (t - 1.0 - pos[:, None, :])) * scale
    blk = jnp.exp(lg * t)
    bcast = lambda v: jnp.broadcast_to(v[..., None], v.shape + (RET_HEAD_DIM,))
    return inner, bcast(q_dec), bcast(k_dec), jnp.broadcast_to(blk[:, :, None, None], (2, RET_HEADS, SUBLANES, RET_HEAD_DIM))


def _ret_kernel(q_ref, k_ref, v_ref, cos_ref, sin_ref, rot_ref, inner_ref, qd_ref, kd_ref, bd_ref, o_ref, s_ref):
    @pl.when(pl.program_id(1) == 0)
    def _():
        s_ref[...] = jnp.zeros_like(s_ref)

    cos = cos_ref[...]
    sin = sin_ref[...]
    rot = rot_ref[...]
    nt_dims = (((1,), (1,)), ((), ()))
    for b in range(q_ref.shape[0]):
        for hd in range(RET_HEADS):
            sl = slice(hd * RET_HEAD_DIM, (hd + 1) * RET_HEAD_DIM)
            q = q_ref[b, :, sl]
            k = k_ref[b, :, sl]
            v = v_ref[b, :, sl]
            qr = q.astype(F32) * cos + jnp.dot(q, rot, preferred_element_type=F32) * sin
            kr = k.astype(F32) * cos + jnp.dot(k, rot, preferred_element_type=F32) * sin
            qb = qr.astype(BF16)
            att = lax.dot_general(qb, kr.astype(BF16), nt_dims, preferred_element_type=F32) * inner_ref[0, hd]
            s = s_ref[b * RET_HEADS + hd]
            o = jnp.dot(att.astype(BF16), v, preferred_element_type=F32)
            o = o + jnp.dot(qb, s.astype(BF16), preferred_element_type=F32) * qd_ref[0, hd]
            kd_t = (kr * kd_ref[0, hd]).T.astype(BF16)
            s_ref[b * RET_HEADS + hd] = bd_ref[0, hd, 0:1, :] * s + jnp.dot(kd_t, v, preferred_element_type=F32)
            o_ref[0, b, :, sl] = o.astype(o_ref.dtype)


def retention(proj, log_decay, n_lat):
    bsz, nt, _ = proj.shape
    t = RET_CHUNK
    nc = nt // t
    chunk = _scan_chunk_index(nc, (nt - n_lat) // t)
    cos, sin = _rope_tables(n_lat, nt - n_lat)
    inner, q_dec, k_dec, blk = _ret_decay(log_decay, t)
    w = RET_WIDTH

    def col(j):
        return pl.BlockSpec((bsz, t, w), lambda d, i: (0, chunk(d, i), j))

    tab = pl.BlockSpec((t, RET_HEAD_DIM), lambda d, i: (chunk(d, i), 0))
    dec = pl.BlockSpec((1, RET_HEADS, t, RET_HEAD_DIM), lambda d, i: (d, 0, 0, 0))
    return pl.pallas_call(
        _ret_kernel,
        grid=(2, nc),
        in_specs=[col(1), col(2), col(3), tab, tab,
                  pl.BlockSpec((RET_HEAD_DIM, RET_HEAD_DIM), lambda d, i: (0, 0)),
                  pl.BlockSpec((1, RET_HEADS, t, t), lambda d, i: (d, 0, 0, 0)),
                  dec, dec,
                  pl.BlockSpec((1, RET_HEADS, SUBLANES, RET_HEAD_DIM), lambda d, i: (d, 0, 0, 0))],
        out_specs=pl.BlockSpec((1, bsz, t, w), lambda d, i: (d, 0, chunk(d, i), 0)),
        out_shape=jax.ShapeDtypeStruct((2, bsz, nt, w), BF16),
        scratch_shapes=[pltpu.VMEM((bsz * RET_HEADS, RET_HEAD_DIM, RET_HEAD_DIM), F32)],
        compiler_params=_params(("arbitrary", "arbitrary")),
        name="retention",
    )(proj, proj, proj, cos, sin, _rope_rotation_matrix(), inner, q_dec, k_dec, blk)


def _mix_out_kernel(u_ref, g_ref, ys_ref, or_ref, *refs, n_streams, lat_tiles):
    mod_ref, d_ref, wglu_ref, wo_ref, o_ref = refs[n_streams:]
    y = u_ref[0].astype(F32) * d_ref[...] + ys_ref[0, 0].astype(F32) + ys_ref[1, 0].astype(F32)
    y = jax.nn.gelu(y)
    a = y * jax.nn.sigmoid(jnp.dot(y.astype(BF16), wglu_ref[...], preferred_element_type=F32))
    o = or_ref[0, 0].astype(F32) + or_ref[1, 0].astype(F32)
    heads = []
    for hd in range(RET_HEADS):
        oh = o[:, hd * RET_HEAD_DIM:(hd + 1) * RET_HEAD_DIM]
        heads.append(oh * lax.rsqrt(jnp.mean(oh * oh, axis=-1, keepdims=True) + RMS_EPS))
    r = jnp.concatenate(heads, axis=1) * jax.nn.silu(g_ref[0].astype(F32))
    m = jnp.dot(a.astype(BF16), wo_ref[:S5_WIDTH, :], preferred_element_type=F32)
    m = m + jnp.dot(r.astype(BF16), wo_ref[S5_WIDTH:, :], preferred_element_type=F32)
    o_ref[0] = _stream_tile(refs[:n_streams], lat_tiles) + mod_ref[0, 2:3, :] * m


def mix_out(streams, proj, y_s5, o_ret, mod, d_skip, w_glu, w_out, n_lat):
    bsz, nt, _ = proj.shape
    d = streams[0].shape[-1]
    tm = _token_tile(n_lat, nt)
    w = S5_WIDTH
    dirs = pl.BlockSpec((2, 1, tm, w), lambda b, j: (0, b, j, 0))
    return pl.pallas_call(
        functools.partial(_mix_out_kernel, n_streams=len(streams), lat_tiles=n_lat // tm),
        grid=(bsz, nt // tm),
        in_specs=[pl.BlockSpec((1, tm, w), lambda b, j: (b, j, 0)),
                  pl.BlockSpec((1, tm, w), lambda b, j: (b, j, 4)),
                  dirs, dirs] + _stream_specs(streams, tm, n_lat) + [
                  pl.BlockSpec((1, SUBLANES, d), _mod_index(n_lat // tm, bsz)),
                  pl.BlockSpec((1, w), lambda b, j: (0, 0)),
                  pl.BlockSpec((w, w), lambda b, j: (0, 0)),
                  pl.BlockSpec((w + RET_WIDTH, d), lambda b, j: (0, 0))],
        out_specs=pl.BlockSpec((1, tm, d), lambda b, j: (b, j, 0)),
        out_shape=jax.ShapeDtypeStruct((bsz, nt, d), F32),
        compiler_params=_params(("arbitrary", "arbitrary")),
        name="mix_out",
    )(proj, proj, y_s5, o_ret, *streams, mod, d_skip.reshape(1, w), w_glu.astype(BF16), w_out.astype(BF16))


def _na_band_start(r0, rows):
    return jnp.clip(r0 - NA_WIN_ROWS // 2, 0, rows - NA_BAND_ROWS)


def _na_bias(rpb, rows):
    w = GRID_W
    n_var = (NA_BAND_ROWS - NA_Q_ROWS) // NA_Q_ROWS + 1
    half = NA_WIN_ROWS // 2
    r0 = np.array([v * NA_Q_ROWS if v * NA_Q_ROWS <= half else rows - NA_BAND_ROWS + v * NA_Q_ROWS
                   for v in range(n_var)])
    bs = np.clip(r0 - half, 0, rows - NA_BAND_ROWS)
    assert list(r0 - bs) == [v * NA_Q_ROWS for v in range(n_var)]
    r = r0[:, None] + np.arange(NA_Q_ROWS)[None, :]
    rs = np.clip(r - half, 0, rows - NA_WIN_ROWS)
    a = bs[:, None] + np.arange(NA_BAND_ROWS)[None, :]
    row_ok = (a[:, None, :] >= rs[:, :, None]) & (a[:, None, :] < rs[:, :, None] + NA_WIN_ROWS)
    row_off = np.clip(a[:, None, :] - r[:, :, None] + (NA_WIN_ROWS - 1), 0, 2 * NA_WIN_ROWS - 2)
    col = np.arange(w)
    col_start = np.clip(col - NA_WIN_COLS // 2, 0, w - NA_WIN_COLS)
    col_ok = (col[None, :] >= col_start[:, None]) & (col[None, :] < col_start[:, None] + NA_WIN_COLS)
    col_off = col[None, :] - col[:, None] + (NA_WIN_COLS - 1)
    col_sel = (col_off[:, :, None] == np.arange(2 * NA_WIN_COLS - 1)) & col_ok[:, :, None]
    tiles = jnp.sum(rpb.astype(F32)[:, :, None, None, :] * jnp.asarray(col_sel, F32)[None, None], axis=-1)
    tiles = jnp.where(jnp.asarray(col_ok)[None, None], tiles, MASK_VALUE)
    masked = jnp.full((NA_HEADS, w, w), MASK_VALUE, F32)
    groups = []
    for v in range(n_var):
        per_row = []
        for rr in range(NA_Q_ROWS):
            per_row.append(jnp.concatenate(
                [tiles[:, int(row_off[v, rr, i])] if row_ok[v, rr, i] else masked for i in range(NA_BAND_ROWS)],
                axis=-1))
        groups.append(jnp.concatenate(per_row, axis=1))
    return jnp.stack(groups)


def _na_kernel(q_ref, kb_ref, vb_ref, kc_ref, vc_ref, bias_ref, o_ref, s_ref, p_ref):
    scale = NA_HEAD_DIM ** -0.5
    nt_dims = (((1,), (1,)), ((), ()))
    nq = q_ref.shape[1]
    n_ctx = kc_ref.shape[1]
    n_slot = s_ref.shape[0]
    heads_per_tile = LANES // NA_HEAD_DIM
    lane = lax.broadcasted_iota(jnp.int32, (nq, LANES), 1)
    for j in range(NA_WIDTH // LANES):
        sl = slice(j * LANES, (j + 1) * LANES)
        q2 = q_ref[0, :, sl].astype(F32) * scale
        k2 = kb_ref[0, :, sl]
        v2 = vb_ref[0, :, sl]
        kc2 = kc_ref[0, :, sl]
        vc2 = vc_ref[0, :, sl]
        qm = jnp.concatenate(
            [jnp.where((lane >= hh * NA_HEAD_DIM) & (lane < (hh + 1) * NA_HEAD_DIM), q2, 0.0).astype(BF16)
             for hh in range(heads_per_tile)], axis=0)
        slot = j % n_slot
        s_ref[slot, :, :n_ctx] = lax.dot_general(qm, kc2, nt_dims, preferred_element_type=F32)
        s_loc = lax.dot_general(qm, k2, nt_dims, preferred_element_type=F32)
        for hh in range(heads_per_tile):
            rows = slice(hh * nq, (hh + 1) * nq)
            s_ref[slot, rows, n_ctx:] = s_loc[rows] + bias_ref[0, j * heads_per_tile + hh]
        inv = []
        for c in range(heads_per_tile * nq // NA_SOFTMAX_ROWS):
            rows = slice(c * NA_SOFTMAX_ROWS, (c + 1) * NA_SOFTMAX_ROWS)
            s = s_ref[slot, rows, :]
            p = jnp.exp(s - jnp.max(s, axis=-1, keepdims=True))
            inv.append(1.0 / jnp.sum(p, axis=-1, keepdims=True))
            p_ref[slot, rows, :] = p.astype(BF16)
        o = jnp.dot(p_ref[slot, :, :n_ctx], vc2, preferred_element_type=F32)
        o = o + jnp.dot(p_ref[slot, :, n_ctx:], v2, preferred_element_type=F32)
        o = o * jnp.concatenate(inv, axis=0)
        o2 = o[:nq]
        for hh in range(1, heads_per_tile):
            o2 = jnp.where(lane >= hh * NA_HEAD_DIM, o[hh * nq:(hh + 1) * nq], o2)
        o_ref[0, :, sl] = o2.astype(o_ref.dtype)


def na_attention(qkv, rpb, n_lat):
    bsz, nt, _ = qkv.shape
    w = GRID_W
    rows = n_lat // w
    n_ctx = nt - n_lat
    nq = NA_Q_ROWS * w
    band = NA_BAND_ROWS * w
    n_all = n_ctx + band
    stacked = nq * (LANES // NA_HEAD_DIM)
    assert n_ctx % LANES == 0 and band % LANES == 0 and rows % NA_Q_ROWS == 0 and rows >= NA_BAND_ROWS + NA_WIN_ROWS // 2

    def band_start(g):
        return _na_band_start(g * NA_Q_ROWS, rows)

    def band_spec(j):
        return pl.BlockSpec((pl.Element(1), pl.Element(band), pl.Element(NA_WIDTH)),
                            lambda b, g: (b, band_start(g) * w, j * NA_WIDTH))

    def ctx_spec(j):
        return pl.BlockSpec((1, n_ctx, NA_WIDTH), lambda b, g: (b, n_lat // n_ctx, j))

    return pl.pallas_call(
        _na_kernel,
        grid=(bsz, rows // NA_Q_ROWS),
        in_specs=[pl.BlockSpec((1, nq, NA_WIDTH), lambda b, g: (b, g, 0)),
                  band_spec(1), band_spec(2), ctx_spec(1), ctx_spec(2),
                  pl.BlockSpec((1, NA_HEADS, nq, band), lambda b, g: (g - band_start(g) // NA_Q_ROWS, 0, 0, 0))],
        out_specs=pl.BlockSpec((1, nq, NA_WIDTH), lambda b, g: (b, g, 0)),
        out_shape=jax.ShapeDtypeStruct((bsz, n_lat, NA_WIDTH), BF16),
        scratch_shapes=[pltpu.VMEM((NA_SCORE_SLOTS, stacked, n_all), F32),
                        pltpu.VMEM((NA_SCORE_SLOTS, stacked, n_all), BF16)],
        compiler_params=_params(("arbitrary", "arbitrary")),
        name="na_attention",
    )(qkv, qkv, qkv, qkv, qkv, _na_bias(rpb, rows))


def _proj_res_kernel(a_ref, x_ref, mod_ref, w_ref, o_ref):
    m = jnp.dot(a_ref[0], w_ref[...], preferred_element_type=F32)
    o_ref[0] = x_ref[0] + mod_ref[0, 2:3, :] * m


def proj_residual(a, x, mod, w):
    bsz, n, k = a.shape
    d = x.shape[-1]
    tm = _token_tile(n, n)
    return pl.pallas_call(
        _proj_res_kernel,
        grid=(bsz, n // tm),
        in_specs=[pl.BlockSpec((1, tm, k), lambda b, j: (b, j, 0)),
                  pl.BlockSpec((1, tm, d), lambda b, j: (b, j, 0)),
                  pl.BlockSpec((1, SUBLANES, d), lambda b, j: (b, 0, 0)),
                  pl.BlockSpec((k, d), lambda b, j: (0, 0))],
        out_specs=pl.BlockSpec((1, tm, d), lambda b, j: (b, j, 0)),
        out_shape=jax.ShapeDtypeStruct((bsz, n, d), F32),
        compiler_params=_params(("arbitrary", "arbitrary")),
        name="proj_residual",
    )(a, x, mod, w.astype(BF16))


def _store_row_tiled(ref, value, index=()):
    rows, d = value.shape
    parts = d // LANES
    for j in range(parts):
        ref[index + (pl.ds(j, rows, stride=parts), slice(None))] = value[:, j * LANES:(j + 1) * LANES]


def _load_row_tiled(ref, rows, d, index=()):
    parts = d // LANES
    return jnp.concatenate([ref[index + (pl.ds(j, rows, stride=parts), slice(None))] for j in range(parts)], axis=1)


def _router_kernel(x_ref, mod_ref, nw_ref, wr_ref, br_ref, tri_ref, h_ref, idx_ref, gate_ref, rank_ref, cnt_ref,
                   run_ref):
    first_step = (pl.program_id(0) == 0) & (pl.program_id(1) == 0)

    @pl.when(first_step)
    def _():
        run_ref[...] = jnp.zeros_like(run_ref)

    h = _norm_mod(x_ref[0], nw_ref[...], mod_ref[0, 3:4, :], mod_ref[0, 4:5, :])
    _store_row_tiled(h_ref, h)
    logits = jnp.dot(h.astype(BF16), wr_ref[...], preferred_element_type=F32) + br_ref[...]
    lane = lax.broadcasted_iota(jnp.int32, logits.shape, 1)
    lane_f = lane.astype(F32)
    vals, idxs = [], []
    cur = logits
    for _ in range(MOE_TOP_K):
        m = jnp.max(cur, axis=-1, keepdims=True)
        first = jnp.min(jnp.where(cur == m, lane_f, float(LANES)), axis=-1, keepdims=True)
        vals.append(m)
        idxs.append(first)
        cur = jnp.where(lane_f == first, MASK_VALUE, cur)
    es = [jnp.exp(v - vals[0]) for v in vals]
    tot = es[0]
    for e in es[1:]:
        tot = tot + e
    onehots = [(lane_f == idxs[k]).astype(F32) for k in range(MOE_TOP_K)]
    multi = onehots[0]
    for oh in onehots[1:]:
        multi = multi + oh
    before = jnp.dot(tri_ref[...], multi.astype(BF16), preferred_element_type=F32) + run_ref[0:1, :]
    idx_out = jnp.zeros(logits.shape, F32)
    gate_out = jnp.zeros(logits.shape, F32)
    rank_out = jnp.zeros(logits.shape, F32)
    for k in range(MOE_TOP_K):
        idx_out = jnp.where(lane == k, idxs[k], idx_out)
        gate_out = jnp.where(lane == k, es[k] / tot, gate_out)
        rank_out = jnp.where(lane == k, jnp.sum(before * onehots[k], axis=-1, keepdims=True), rank_out)
    idx_ref[0] = idx_out.astype(jnp.int32)
    gate_ref[0] = gate_out
    rank_ref[0] = rank_out.astype(jnp.int32)
    run_ref[0:1, :] = run_ref[0:1, :] + jnp.sum(multi, axis=0, keepdims=True)
    cnt_ref[...] = jnp.broadcast_to(run_ref[0:1, :], cnt_ref.shape)


def router(xc, mod, norm_w, w_router, b_router, n_lat):
    bsz, nt, d = xc.shape
    tm = _token_tile(n_lat, nt)
    tiles = nt // tm
    parts = d // LANES
    wr = jnp.pad(w_router, ((0, 0), (0, LANES - N_EXPERTS))).astype(BF16)
    br = jnp.pad(b_router.astype(F32), (0, LANES - N_EXPERTS), constant_values=MASK_VALUE).reshape(1, LANES)
    tri = jnp.tril(jnp.ones((tm, tm), BF16), k=-1)
    tok = lambda n, dt: jax.ShapeDtypeStruct((bsz, nt, n), dt)
    out = lambda n: pl.BlockSpec((1, tm, n), lambda b, j: (b, j, 0))
    return pl.pallas_call(
        _router_kernel,
        grid=(bsz, nt // tm),
        in_specs=[pl.BlockSpec((1, tm, d), lambda b, j: (b, j, 0)),
                  pl.BlockSpec((1, SUBLANES, d), _mod_index(n_lat // tm, bsz)),
                  pl.BlockSpec((1, d), lambda b, j: (0, 0)),
                  pl.BlockSpec((d, LANES), lambda b, j: (0, 0)),
                  pl.BlockSpec((1, LANES), lambda b, j: (0, 0)),
                  pl.BlockSpec((tm, tm), lambda b, j: (0, 0))],
        out_specs=[pl.BlockSpec((tm * parts, LANES), lambda b, j: (b * tiles + j, 0)),
                   out(LANES), out(LANES), out(LANES),
                   pl.BlockSpec((SUBLANES, LANES), lambda b, j: (0, 0))],
        out_shape=[jax.ShapeDtypeStruct((bsz * nt * parts, LANES), F32),
                   tok(LANES, jnp.int32), tok(LANES, F32), tok(LANES, jnp.int32),
                   jax.ShapeDtypeStruct((SUBLANES, LANES), F32)],
        scratch_shapes=[pltpu.VMEM((SUBLANES, LANES), F32)],
        compiler_params=_params(("arbitrary", "arbitrary")),
        name="router",
    )(xc, mod, norm_w.reshape(1, d), wr, br, tri)


def _moe_kernel(be_ref, nu_ref, x_ref, w1_ref, b1_ref, w2_ref, b2_ref, o_ref, w1b_ref, w2b_ref):
    i = pl.program_id(0)
    ff = w2_ref.shape[2]
    used = i < nu_ref[0]

    @pl.when(used & ((i == 0) | (be_ref[i] != be_ref[jnp.maximum(i - 1, 0)])))
    def _():
        w1b_ref[...] = w1_ref[0, 0].astype(BF16)
        w2b_ref[...] = w2_ref[0, 0].astype(BF16)

    @pl.when(used)
    def _():
        d = w1b_ref.shape[0]
        x = _load_row_tiled(x_ref, MOE_TILE, d)
        t = jnp.dot(x.astype(BF16), w1b_ref[...], preferred_element_type=F32) + b1_ref[0, 0]
        x_glu = jnp.minimum(t[:, :ff], SWIGLU_LIMIT)
        x_lin = jnp.clip(t[:, ff:], -SWIGLU_LIMIT, SWIGLU_LIMIT)
        act = x_glu * jax.nn.sigmoid(SWIGLU_ALPHA * x_glu) * (x_lin + 1)
        y = jnp.dot(act.astype(BF16), w2b_ref[...], preferred_element_type=F32) + b2_ref[0, 0]
        _store_row_tiled(o_ref, y)

    @pl.when(jnp.logical_not(used))
    def _():
        o_ref[...] = jnp.zeros_like(o_ref)


def moe_experts(xs, block_e, n_used, w1, b1, w2, b2, layer):
    depth, ne, d, ff2 = w1.shape
    ff = w2.shape[2]
    tm = MOE_TILE * (d // LANES)
    nb = xs.shape[0] // tm
    grid_spec = pltpu.PrefetchScalarGridSpec(
        num_scalar_prefetch=2,
        grid=(nb,),
        in_specs=[pl.BlockSpec((tm, LANES), lambda i, be, nu: (jnp.where(i < nu[0], i, 0), 0)),
                  pl.BlockSpec((1, 1, d, ff2), lambda i, be, nu: (layer, be[i], 0, 0)),
                  pl.BlockSpec((1, 1, 1, ff2), lambda i, be, nu: (layer, be[i], 0, 0)),
                  pl.BlockSpec((1, 1, ff, d), lambda i, be, nu: (layer, be[i], 0, 0)),
                  pl.BlockSpec((1, 1, 1, d), lambda i, be, nu: (layer, be[i], 0, 0))],
        out_specs=pl.BlockSpec((tm, LANES), lambda i, be, nu: (i, 0)),
        scratch_shapes=[pltpu.VMEM((d, ff2), BF16), pltpu.VMEM((ff, d), BF16)],
    )
    return pl.pallas_call(
        _moe_kernel,
        grid_spec=grid_spec,
        out_shape=jax.ShapeDtypeStruct(xs.shape, F32),
        compiler_params=_params(("arbitrary",)),
        name="moe_experts",
    )(block_e, n_used, xs, w1, b1.reshape(depth, ne, 1, ff2), w2, b2.reshape(depth, ne, 1, d))


def _moe_plan(idx, rank, counts, n_tok):
    counts = counts.astype(jnp.int32)
    padded = (counts + MOE_TILE - 1) // MOE_TILE * MOE_TILE
    padded_ends = jnp.cumsum(padded)
    padded_starts = padded_ends - padded
    onehot = idx[..., None] == jnp.arange(N_EXPERTS, dtype=jnp.int32)
    slot_of = rank + jnp.sum(jnp.where(onehot, padded_starts, 0), axis=-1)
    n_blocks = -(-n_tok * MOE_TOP_K // MOE_TILE) + N_EXPERTS
    block_row = jnp.arange(n_blocks, dtype=jnp.int32) * MOE_TILE
    block_e = jnp.minimum(jnp.sum(padded_ends[None, :] <= block_row[:, None], axis=1), N_EXPERTS - 1)
    n_used = padded_ends[-1:] // MOE_TILE
    last_block = jnp.concatenate([jnp.where(counts > 0, padded_ends - MOE_TILE, -1), n_used])
    return (slot_of.astype(jnp.int32), block_e.astype(jnp.int32), n_used.astype(jnp.int32),
            last_block.astype(jnp.int32), n_blocks)


def _dispatch_kernel(slot_ref, last_ref, h_ref, xs_ref, zero_ref, sem, *, parts):
    tm = h_ref.shape[0] // parts
    block = MOE_TILE * parts

    @pl.when(pl.program_id(0) == 0)
    def _():
        zero_ref[...] = jnp.zeros_like(zero_ref)

        def zero_copy(e):
            row = pl.multiple_of(jnp.maximum(last_ref[e], 0) * parts, block)
            return pltpu.make_async_copy(zero_ref, xs_ref.at[pl.ds(row, block)], sem)

        def block_copy(i):
            return pltpu.make_async_copy(zero_ref, xs_ref.at[pl.ds(pl.multiple_of(i * block, block), block)], sem)

        n_used = last_ref[N_EXPERTS]
        n_blocks = xs_ref.shape[0] // block
        lax.fori_loop(n_used, n_blocks, lambda i, c: (block_copy(i).start(), c)[1], 0)
        for e in range(N_EXPERTS):
            pl.when(last_ref[e] >= 0)(lambda e=e: zero_copy(e).start())
        for e in range(N_EXPERTS):
            pl.when(last_ref[e] >= 0)(lambda e=e: zero_copy(e).wait())
        lax.fori_loop(n_used, n_blocks, lambda i, c: (block_copy(i).wait(), c)[1], 0)

    def issue(t, carry):
        for k in range(MOE_TOP_K):
            s = slot_ref[0, 0, t * MOE_TOP_K + k]
            src = h_ref.at[pl.ds(pl.multiple_of(t * parts, parts), parts)]
            dst = xs_ref.at[pl.ds(pl.multiple_of(s * parts, parts), parts)]
            pltpu.make_async_copy(src, dst, sem).start(priority=k % 2)
        return carry

    lax.fori_loop(0, tm, issue, 0, unroll=DMA_ISSUE_UNROLL)
    for k in range(MOE_TOP_K):
        pltpu.make_async_copy(h_ref, xs_ref.at[pl.ds(0, tm * parts)], sem).wait()


def moe_dispatch(h, slot_tiles, last_block, n_blocks, n_tok):
    parts = h.shape[0] // n_tok
    tm = slot_tiles.shape[-1] // MOE_TOP_K
    return pl.pallas_call(
        functools.partial(_dispatch_kernel, parts=parts),
        grid=(n_tok // tm,),
        in_specs=[pl.BlockSpec((1, 1, tm * MOE_TOP_K), lambda i: (i, 0, 0), memory_space=pltpu.SMEM),
                  pl.BlockSpec(memory_space=pltpu.SMEM),
                  pl.BlockSpec((tm * parts, LANES), lambda i: (i, 0))],
        out_specs=pl.BlockSpec(memory_space=pl.ANY),
        out_shape=jax.ShapeDtypeStruct((n_blocks * MOE_TILE * parts, LANES), F32),
        scratch_shapes=[pltpu.VMEM((MOE_TILE * parts, LANES), F32), pltpu.SemaphoreType.DMA(())],
        compiler_params=_params(("arbitrary",)),
        name="moe_dispatch",
    )(slot_tiles, last_block, h)


def _combine_kernel(slot_ref, next_slot_ref, x_ref, gate_ref, mod_ref, *rest, final):
    ys_ref, o_ref, ybuf_ref, sems = rest[-4:]
    tm, d = x_ref.shape[1:]
    parts = d // LANES
    g = pl.program_id(0) * pl.num_programs(1) + pl.program_id(1)
    n_steps = pl.num_programs(0) * pl.num_programs(1)

    def fetch(slots, buf):
        def issue(t, carry):
            for k in range(MOE_TOP_K):
                s = slots[0, 0, t * MOE_TOP_K + k]
                src = ys_ref.at[pl.ds(pl.multiple_of(s * parts, parts), parts)]
                dst = ybuf_ref.at[buf * MOE_TOP_K + k, pl.ds(pl.multiple_of(t * parts, parts), parts)]
                pltpu.make_async_copy(src, dst, sems.at[buf]).start(priority=k % 2)
            return carry

        lax.fori_loop(0, tm, issue, 0, unroll=DMA_ISSUE_UNROLL)

    def combine(buf):
        for k in range(MOE_TOP_K):
            pltpu.make_async_copy(ys_ref.at[pl.ds(0, tm * parts)], ybuf_ref.at[buf * MOE_TOP_K + k], sems.at[buf]).wait()
        gate = gate_ref[0]
        y = None
        for k in range(MOE_TOP_K):
            term = gate[:, k:k + 1] * _load_row_tiled(ybuf_ref, tm, d, (buf * MOE_TOP_K + k,))
            y = term if y is None else y + term
        x = x_ref[0] + mod_ref[0, 5:6, :] * y
        if final:
            fw_ref = rest[0]
            x = x * lax.rsqrt(jnp.mean(x * x, axis=-1, keepdims=True) + RMS_EPS) * fw_ref[...]
        o_ref[0] = x

    pl.when(g == 0)(lambda: fetch(slot_ref, 0))
    for buf in range(2):
        pl.when((g + 1 < n_steps) & ((g + 1) % 2 == buf))(lambda buf=buf: fetch(next_slot_ref, buf))
    for buf in range(2):
        pl.when(g % 2 == buf)(lambda buf=buf: combine(buf))


def moe_combine(xc, ys, slot_tiles, gates, mod, n_lat, final_w=None):
    bsz, nt, d = xc.shape
    tm = slot_tiles.shape[-1] // MOE_TOP_K
    tiles = nt // tm
    last = bsz * tiles - 1
    slot_spec = lambda ahead: pl.BlockSpec((1, 1, tm * MOE_TOP_K),
                                           lambda b, j: (jnp.minimum(b * tiles + j + ahead, last), 0, 0),
                                           memory_space=pltpu.SMEM)
    in_specs = [slot_spec(0), slot_spec(1),
                pl.BlockSpec((1, tm, d), lambda b, j: (b, j, 0)),
                pl.BlockSpec((1, tm, LANES), lambda b, j: (b, j, 0)),
                pl.BlockSpec((1, SUBLANES, d), _mod_index(n_lat // tm, bsz))]
    args = [slot_tiles, slot_tiles, xc, gates, mod]
    if final_w is not None:
        in_specs.append(pl.BlockSpec((1, d), lambda b, j: (0, 0)))
        args.append(final_w.reshape(1, d))
    in_specs.append(pl.BlockSpec(memory_space=pl.ANY))
    args.append(ys)
    return pl.pallas_call(
        functools.partial(_combine_kernel, final=final_w is not None),
        grid=(bsz, tiles),
        in_specs=in_specs,
        out_specs=pl.BlockSpec((1, tm, d), lambda b, j: (b, j, 0)),
        out_shape=jax.ShapeDtypeStruct((bsz, nt, d), F32),
        scratch_shapes=[pltpu.VMEM((2 * MOE_TOP_K, tm * (d // LANES), LANES), F32), pltpu.SemaphoreType.DMA((2,))],
        compiler_params=_params(("arbitrary", "arbitrary")),
        name="moe_combine",
    )(*args)


def moe_layer(xc, mod, norm_w, w_router, b_router, experts, n_lat, final_w=None):
    bsz, nt, d = xc.shape
    n_tok = bsz * nt
    h, idx, gates, rank, counts = router(xc, mod, norm_w, w_router, b_router, n_lat)
    top = lambda a: a.reshape(n_tok, LANES)[:, :MOE_TOP_K]
    slot_of, block_e, n_used, last_block, n_blocks = _moe_plan(top(idx), top(rank), counts[0, :N_EXPERTS], n_tok)
    tiled = lambda tm: slot_of.reshape(n_tok // tm, 1, tm * MOE_TOP_K)
    xs = moe_dispatch(h, tiled(_token_tile(n_tok, n_tok)), last_block, n_blocks, n_tok)
    ys = moe_experts(xs, block_e, n_used, *experts)
    return moe_combine(xc, ys, tiled(_token_tile(n_lat, nt, COMBINE_TILE)), gates, mod, n_lat, final_w)


def kernel(x, c, ctx, c_ctx, ada_w, ada_b, norm_w, final_norm_w, ev_w_in, ev_w_out, s5_lam_re, s5_lam_im, s5_log_step, s5_b_re, s5_b_im, s5_c_re, s5_c_im, s5_d, s5_w_glu, ret_log_decay, na_w_qkv, na_w_o, na_rpb, moe_w_router, moe_b_router, moe_w1, moe_b1, moe_w2, moe_b2):
    bsz, n_lat, d = x.shape
    nt = n_lat + ctx.shape[1]
    depth = ada_w.shape[0]
    mod = adaln_table(c, c_ctx, ada_w, ada_b)
    streams = (x, ctx)
    for i in range(depth):
        last = i == depth - 1
        j = i // 2
        if i % 2 == 0:
            proj = norm_proj(streams, mod[i], norm_w[i, 0], ev_w_in[j], n_lat, nt, 0)
            s5p = _s5_layout(s5_lam_re[j], s5_lam_im[j], s5_log_step[j], s5_b_re[j], s5_b_im[j],
                             s5_c_re[j], s5_c_im[j])
            y_s5 = s5_scan(proj, s5p, n_lat)
            o_ret = retention(proj, ret_log_decay[j], n_lat)
            xc = mix_out(streams, proj, y_s5, o_ret, mod[i], s5_d[j], s5_w_glu[j], ev_w_out[j], n_lat)
        else:
            qkv = norm_proj(streams, mod[i], norm_w[i, 0], na_w_qkv[j], n_lat, nt, 0)
            att = na_attention(qkv, na_rpb[j], n_lat)
            assert last, "an odd layer is only supported as the final layer (no context output needed)"
            xc = proj_residual(att, streams[0], mod[i], na_w_o[j])
        experts = (moe_w1, moe_b1, moe_w2, moe_b2, i)
        if last:
            return moe_layer(xc[:, :n_lat], mod[i], norm_w[i, 1], moe_w_router[i], moe_b_router[i],
                             experts, n_lat, final_norm_w)
        streams = (moe_layer(xc, mod[i], norm_w[i, 1], moe_w_router[i], moe_b_router[i], experts, n_lat),)
```

```python
import functools

import numpy as np
import jax
import jax.numpy as jnp
from jax import lax
from jax.experimental import pallas as pl
from jax.experimental.pallas import tpu as pltpu

F32 = jnp.float32
BF16 = jnp.bfloat16

GRID_W = 64
RMS_EPS = 1e-6
S5_WIDTH = 512
S5_GROUP = 16
S5_GROUPS = S5_WIDTH // S5_GROUP
S5_STATE = 64
RET_HEADS = 4
RET_HEAD_DIM = 128
RET_WIDTH = RET_HEADS * RET_HEAD_DIM
ROPE_BASE = 10000.0
NA_HEADS = 16
NA_HEAD_DIM = 64
NA_WIDTH = NA_HEADS * NA_HEAD_DIM
NA_WIN_ROWS = 8
NA_WIN_COLS = 16
N_EXPERTS = 32
MOE_TOP_K = 4
SWIGLU_LIMIT = 7.0
SWIGLU_ALPHA = 1.702

LANES = 128
SUBLANES = 8
MXU_DIM = 256
V7X_VMEM_BYTES = 64 * 1024 * 1024
VMEM_LIMIT = V7X_VMEM_BYTES * 7 // 8

MAX_TOKEN_TILE = 1024
COMBINE_TILE = 512
S5_CHUNK = 256
RET_CHUNK = 256
MOE_TILE = 512
DMA_ISSUE_UNROLL = 8
NA_Q_ROWS = 4
NA_BAND_ROWS = 12
NA_SOFTMAX_ROWS = 16
NA_SCORE_SLOTS = 2
assert NA_BAND_ROWS >= NA_WIN_ROWS + NA_Q_ROWS - 1 and (NA_BAND_ROWS - NA_Q_ROWS) % NA_Q_ROWS == 0
MASK_VALUE = -1e30

S5_SLICES = 8
assert S5_SLICES * MXU_DIM == S5_GROUPS * S5_STATE


def _params(sem):
    return pltpu.CompilerParams(dimension_semantics=sem, vmem_limit_bytes=VMEM_LIMIT)


def _adaln_kernel(c_ref, w_ref, b_ref, o_ref):
    c = c_ref[...]
    s = c * jax.nn.sigmoid(c)
    o_ref[0] = jnp.dot(s, w_ref[0], preferred_element_type=F32,
                       precision=lax.Precision.HIGHEST) + b_ref[0]


def adaln_table(c, c_ctx, ada_w, ada_b):
    depth, d, d6 = ada_w.shape
    bsz = c.shape[0]
    cond = jnp.concatenate([c, c_ctx[None, :]], axis=0)
    cond = jnp.pad(cond, ((0, SUBLANES - (bsz + 1)), (0, 0)))
    tn = d6 // 4
    out = pl.pallas_call(
        _adaln_kernel,
        grid=(depth, d6 // tn),
        in_specs=[pl.BlockSpec((SUBLANES, d), lambda i, j: (0, 0)),
                  pl.BlockSpec((1, d, tn), lambda i, j: (i, 0, j)),
                  pl.BlockSpec((1, 1, tn), lambda i, j: (i, 0, j))],
        out_specs=pl.BlockSpec((1, SUBLANES, tn), lambda i, j: (i, 0, j)),
        out_shape=jax.ShapeDtypeStruct((depth, SUBLANES, d6), F32),
        compiler_params=_params(("arbitrary", "arbitrary")),
        name="adaln",
    )(cond, ada_w, ada_b.reshape(depth, 1, d6))
    tab = out[:, :bsz + 1].reshape(depth, bsz + 1, 6, d)
    return jnp.pad(tab, ((0, 0), (0, 0), (0, 2), (0, 0)))


def _norm_mod(x, nw, shift, scale):
    y = x * lax.rsqrt(jnp.mean(x * x, axis=-1, keepdims=True) + RMS_EPS)
    return (y * nw) * (1 + scale) + shift


def _token_tile(n_lat, nt, cap=MAX_TOKEN_TILE):
    tm = cap
    while n_lat % tm or (nt - n_lat) % tm:
        tm //= 2
    return tm


def _mod_index(n_lat_tiles, bsz):
    def index(b, j):
        return (jnp.where(j >= n_lat_tiles, bsz, b), 0, 0)
    return index


def _stream_specs(streams, tm, n_lat):
    d = streams[0].shape[-1]
    lat_tiles = n_lat // tm
    if len(streams) == 1:
        return [pl.BlockSpec((1, tm, d), lambda b, j: (b, j, 0))]
    return [pl.BlockSpec((1, tm, d), lambda b, j: (b, jnp.minimum(j, lat_tiles - 1), 0)),
            pl.BlockSpec((1, tm, d), lambda b, j: (b, jnp.maximum(j - lat_tiles, 0), 0))]


def _stream_tile(refs, lat_tiles):
    if len(refs) == 1:
        return refs[0][0]
    return jnp.where(pl.program_id(1) >= lat_tiles, refs[1][0], refs[0][0])


def _proj_kernel(*refs, shift_row, n_streams, lat_tiles):
    mod_ref, nw_ref, w_ref, o_ref = refs[n_streams:]
    h = _norm_mod(_stream_tile(refs[:n_streams], lat_tiles), nw_ref[...], mod_ref[0, shift_row:shift_row + 1, :],
                  mod_ref[0, shift_row + 1:shift_row + 2, :])
    o_ref[0] = jnp.dot(h.astype(BF16), w_ref[...], preferred_element_type=F32).astype(o_ref.dtype)


def norm_proj(streams, mod, norm_w, w, n_lat, nt, shift_row):
    bsz, _, d = streams[0].shape
    n = w.shape[1]
    tm = _token_tile(n_lat, nt)
    return pl.pallas_call(
        functools.partial(_proj_kernel, shift_row=shift_row, n_streams=len(streams), lat_tiles=n_lat // tm),
        grid=(bsz, nt // tm),
        in_specs=_stream_specs(streams, tm, n_lat) + [
            pl.BlockSpec((1, SUBLANES, d), _mod_index(n_lat // tm, bsz)),
            pl.BlockSpec((1, d), lambda b, j: (0, 0)),
            pl.BlockSpec((d, n), lambda b, j: (0, 0))],
        out_specs=pl.BlockSpec((1, tm, n), lambda b, j: (b, j, 0)),
        out_shape=jax.ShapeDtypeStruct((bsz, nt, n), BF16),
        compiler_params=_params(("arbitrary", "arbitrary")),
        name="norm_proj",
    )(*streams, mod, norm_w.reshape(1, d), w.astype(BF16))


def _s5_discretize(lam_re, lam_im, log_step, b_re, b_im):
    lam_re = jnp.minimum(lam_re.astype(F32), -1e-4)
    lam_im = lam_im.astype(F32)
    step = jnp.exp(log_step.astype(F32))[..., None]
    mag = jnp.exp(lam_re * step)
    ang = lam_im * step
    a_re, a_im = mag * jnp.cos(ang), mag * jnp.sin(ang)
    den = lam_re * lam_re + lam_im * lam_im
    n_re, n_im = a_re - 1.0, a_im
    co_re = (n_re * lam_re + n_im * lam_im) / den
    co_im = (n_im * lam_re - n_re * lam_im) / den
    b_re, b_im = b_re.astype(F32), b_im.astype(F32)
    bb_re = co_re[..., None] * b_re - co_im[..., None] * b_im
    bb_im = co_re[..., None] * b_im + co_im[..., None] * b_re
    return a_re, a_im, bb_re, bb_im


def _s5_layout(lam_re, lam_im, log_step, b_re, b_im, c_re, c_im):
    a_re, a_im, bb_re, bb_im = _s5_discretize(lam_re, lam_im, log_step, b_re, b_im)
    gh = S5_GROUPS // 2
    nq = S5_STATE // S5_GROUP
    replicate = jnp.asarray(np.tile(np.eye(S5_GROUP, dtype=np.float32), (1, gh)))
    blk = np.arange(MXU_DIM) // S5_GROUP
    own_group = jnp.asarray(blk[:, None] == blk[None, :])

    def block_diag(t):
        full = jnp.einsum('dkrc,cl->dkrl', t, replicate, precision=lax.Precision.HIGHEST)
        return jnp.where(own_group, full, 0.0).astype(BF16)

    def arrange_a(a):
        a = a.reshape(2, 2, gh, nq, S5_GROUP)
        return a.transpose(0, 1, 3, 2, 4).reshape(2, S5_SLICES, MXU_DIM)

    def arrange_b(bb):
        bb = bb.reshape(2, 2, gh, nq, S5_GROUP, S5_GROUP)
        return block_diag(bb.transpose(0, 1, 3, 2, 5, 4).reshape(2, S5_SLICES, MXU_DIM, S5_GROUP))

    def arrange_c(c):
        c = c.astype(F32).reshape(2, 2, gh, S5_GROUP, nq, S5_GROUP)
        return block_diag(c.transpose(0, 1, 4, 2, 5, 3).reshape(2, S5_SLICES, MXU_DIM, S5_GROUP))

    return (arrange_a(a_re), arrange_a(a_im), arrange_b(bb_re), arrange_b(bb_im),
            arrange_c(c_re), arrange_c(-c_im.astype(F32)))


def _s5_kernel(u_ref, bre_ref, bim_ref, cre_ref, cim_ref, are_ref, aim_ref, y_ref, bu_ref, st_ref, h_ref, *, nb):
    t_chunk = S5_CHUNK
    half = MXU_DIM // LANES
    d = pl.program_id(0)

    @pl.when(pl.program_id(1) == 0)
    def _():
        h_ref[...] = jnp.zeros_like(h_ref)

    def slab(b, c, lh):
        return (b * 2 + c) * half + lh

    for b in range(nb):
        for h in range(2):
            ub = u_ref[b, :, h * MXU_DIM:(h + 1) * MXU_DIM]
            for q in range(S5_SLICES // 2):
                k = h * (S5_SLICES // 2) + q
                for c, w_ref in ((0, bre_ref), (1, bim_ref)):
                    r = jnp.dot(ub, w_ref[0, k], preferred_element_type=F32)
                    for lh in range(half):
                        bu_ref[slab(b, c, lh), pl.ds(k, t_chunk, stride=S5_SLICES), :] = r[:, lh * LANES:(lh + 1) * LANES]

    ar = [are_ref[0, :, lh * LANES:(lh + 1) * LANES] for lh in range(half)]
    ai = [aim_ref[0, :, lh * LANES:(lh + 1) * LANES] for lh in range(half)]

    def step(t, carry):
        tt = jnp.where(d == 0, t, t_chunk - 1 - t)
        new = []
        for b in range(nb):
            for lh in range(half):
                hr, hi = carry[2 * (b * half + lh)], carry[2 * (b * half + lh) + 1]
                rows = pl.ds(pl.multiple_of(tt * S5_SLICES, S5_SLICES), S5_SLICES)
                xr = bu_ref[slab(b, 0, lh), rows, :]
                xi = bu_ref[slab(b, 1, lh), rows, :]
                nr = ar[lh] * hr - ai[lh] * hi + xr
                ni = ar[lh] * hi + ai[lh] * hr + xi
                st_ref[slab(b, 0, lh), rows, :] = nr
                st_ref[slab(b, 1, lh), rows, :] = ni
                new += [nr, ni]
        return tuple(new)

    n_state = nb * half * 2
    carry = lax.fori_loop(0, t_chunk, step, tuple(h_ref[j] for j in range(n_state)), unroll=8)
    for j in range(n_state):
        h_ref[j] = carry[j]

    for b in range(nb):
        for h in range(2):
            acc = None
            for q in range(S5_SLICES // 2):
                k = h * (S5_SLICES // 2) + q
                for c, w_ref in ((0, cre_ref), (1, cim_ref)):
                    s = jnp.concatenate(
                        [st_ref[slab(b, c, lh), pl.ds(k, t_chunk, stride=S5_SLICES), :] for lh in range(half)], axis=1)
                    term = jnp.dot(s.astype(BF16), w_ref[0, k], preferred_element_type=F32)
                    acc = term if acc is None else acc + term
            y_ref[0, b, :, h * MXU_DIM:(h + 1) * MXU_DIM] = acc.astype(y_ref.dtype)


def _scan_chunk_index(n_chunks, n_ctx_chunks):
    def chunk(d, i):
        fwd = lax.rem(i + (n_chunks - n_ctx_chunks), n_chunks)
        return jnp.where(d == 0, fwd, n_chunks - 1 - i)
    return chunk


def s5_scan(proj, s5p, n_lat):
    bsz, nt, _ = proj.shape
    a_re, a_im, bb_re, bb_im, cc_re, cc_im = s5p
    t = S5_CHUNK
    nc = nt // t
    chunk = _scan_chunk_index(nc, (nt - n_lat) // t)
    wspec = pl.BlockSpec((1, S5_SLICES, MXU_DIM, MXU_DIM), lambda d, i: (d, 0, 0, 0))
    aspec = pl.BlockSpec((1, S5_SLICES, MXU_DIM), lambda d, i: (d, 0, 0))
    n_slab = bsz * 2 * (MXU_DIM // LANES)
    return pl.pallas_call(
        functools.partial(_s5_kernel, nb=bsz),
        grid=(2, nc),
        in_specs=[pl.BlockSpec((bsz, t, S5_WIDTH), lambda d, i: (0, chunk(d, i), 0)),
                  wspec, wspec, wspec, wspec, aspec, aspec],
        out_specs=pl.BlockSpec((1, bsz, t, S5_WIDTH), lambda d, i: (d, 0, chunk(d, i), 0)),
        out_shape=jax.ShapeDtypeStruct((2, bsz, nt, S5_WIDTH), BF16),
        scratch_shapes=[pltpu.VMEM((n_slab, S5_SLICES * S5_CHUNK, LANES), F32),
                        pltpu.VMEM((n_slab, S5_SLICES * S5_CHUNK, LANES), F32),
                        pltpu.VMEM((n_slab, S5_SLICES, LANES), F32)],
        compiler_params=_params(("arbitrary", "arbitrary")),
        name="s5_scan",
    )(proj, bb_re, bb_im, cc_re, cc_im, a_re, a_im)


def _rope_tables(n_lat, n_ctx):
    half = RET_HEAD_DIM // 4
    freq = ROPE_BASE ** (-jnp.arange(half, dtype=F32) / half)
    rows = n_lat // GRID_W
    ang_r = jnp.arange(rows, dtype=F32)[:, None] * freq[None, :]
    ang_c = jnp.arange(GRID_W, dtype=F32)[:, None] * freq[None, :]

    def table(fn):
        by_row = jnp.broadcast_to(fn(ang_r)[:, None, :], (rows, GRID_W, half))
        by_col = jnp.broadcast_to(fn(ang_c)[None, :, :], (rows, GRID_W, half))
        return jnp.concatenate([by_row, by_row, by_col, by_col], axis=-1).reshape(n_lat, RET_HEAD_DIM)

    cos, sin = table(jnp.cos), table(jnp.sin)
    cos = jnp.concatenate([cos, jnp.ones((n_ctx, RET_HEAD_DIM), F32)], axis=0)
    sin = jnp.concatenate([sin, jnp.zeros((n_ctx, RET_HEAD_DIM), F32)], axis=0)
    return cos, sin


def _rope_rotation_matrix():
    blk = RET_HEAD_DIM // 2
    half = blk // 2
    r = np.zeros((RET_HEAD_DIM, RET_HEAD_DIM), np.float32)
    for base in (0, blk):
        for l in range(half):
            r[base + l + half, base + l] = -1.0
            r[base + l, base + l + half] = 1.0
    return jnp.asarray(r, BF16)


def _ret_decay(log_decay, t):
    lg = log_decay.astype(F32)
    scale = RET_HEAD_DIM ** -0.5
    idx = jnp.arange(t, dtype=F32)
    diff = idx[:, None] - idx[None, :]
    diff = jnp.stack([diff, -diff])[:, None]
    inner = jnp.where(diff >= 0, jnp.exp(lg[:, :, None, None] * jnp.maximum(diff, 0.0)), 0.0) * scale
    pos = jnp.stack([idx, t - 1.0 - idx])
    q_dec = jnp.exp(lg[:, :, None] * (pos[:, None, :] + 1.0))
    k_dec = jnp.exp(lg[:, :, None] * (t - 1.0 - pos[:, None, :])) * scale
    blk = jnp.exp(lg * t)
    bcast = lambda v: jnp.broadcast_to(v[..., None], v.shape + (RET_HEAD_DIM,))
    return inner, bcast(q_dec), bcast(k_dec), jnp.broadcast_to(blk[:, :, None, None], (2, RET_HEADS, SUBLANES, RET_HEAD_DIM))


def _ret_kernel(q_ref, k_ref, v_ref, cos_ref, sin_ref, rot_ref, inner_ref, qd_ref, kd_ref, bd_ref, o_ref, s_ref):
    @pl.when(pl.program_id(1) == 0)
    def _():
        s_ref[...] = jnp.zeros_like(s_ref)

    cos = cos_ref[...]
    sin = sin_ref[...]
    rot = rot_ref[...]
    nt_dims = (((1,), (1,)), ((), ()))
    for b in range(q_ref.shape[0]):
        for hd in range(RET_HEADS):
            sl = slice(hd * RET_HEAD_DIM, (hd + 1) * RET_HEAD_DIM)
            q = q_ref[b, :, sl]
            k = k_ref[b, :, sl]
            v = v_ref[b, :, sl]
            qr = q.astype(F32) * cos + jnp.dot(q, rot, preferred_element_type=F32) * sin
            kr = k.astype(F32) * cos + jnp.dot(k, rot, preferred_element_type=F32) * sin
            qb = qr.astype(BF16)
            att = lax.dot_general(qb, kr.astype(BF16), nt_dims, preferred_element_type=F32) * inner_ref[0, hd]
            s = s_ref[b * RET_HEADS + hd]
            o = jnp.dot(att.astype(BF16), v, preferred_element_type=F32)
            o = o + jnp.dot(qb, s.astype(BF16), preferred_element_type=F32) * qd_ref[0, hd]
            kd_t = (kr * kd_ref[0, hd]).T.astype(BF16)
            s_ref[b * RET_HEADS + hd] = bd_ref[0, hd, 0:1, :] * s + jnp.dot(kd_t, v, preferred_element_type=F32)
            o_ref[0, b, :, sl] = o.astype(o_ref.dtype)


def retention(proj, log_decay, n_lat):
    bsz, nt, _ = proj.shape
    t = RET_CHUNK
    nc = nt // t
    chunk = _scan_chunk_index(nc, (nt - n_lat) // t)
    cos, sin = _rope_tables(n_lat, nt - n_lat)
    inner, q_dec, k_dec, blk = _ret_decay(log_decay, t)
    w = RET_WIDTH

    def col(j):
        return pl.BlockSpec((bsz, t, w), lambda d, i: (0, chunk(d, i), j))

    tab = pl.BlockSpec((t, RET_HEAD_DIM), lambda d, i: (chunk(d, i), 0))
    dec = pl.BlockSpec((1, RET_HEADS, t, RET_HEAD_DIM), lambda d, i: (d, 0, 0, 0))
    return pl.pallas_call(
        _ret_kernel,
        grid=(2, nc),
        in_specs=[col(1), col(2), col(3), tab, tab,
                  pl.BlockSpec((RET_HEAD_DIM, RET_HEAD_DIM), lambda d, i: (0, 0)),
                  pl.BlockSpec((1, RET_HEADS, t, t), lambda d, i: (d, 0, 0, 0)),
                  dec, dec,
                  pl.BlockSpec((1, RET_HEADS, SUBLANES, RET_HEAD_DIM), lambda d, i: (d, 0, 0, 0))],
        out_specs=pl.BlockSpec((1, bsz, t, w), lambda d, i: (d, 0, chunk(d, i), 0)),
        out_shape=jax.ShapeDtypeStruct((2, bsz, nt, w), BF16),
        scratch_shapes=[pltpu.VMEM((bsz * RET_HEADS, RET_HEAD_DIM, RET_HEAD_DIM), F32)],
        compiler_params=_params(("arbitrary", "arbitrary")),
        name="retention",
    )(proj, proj, proj, cos, sin, _rope_rotation_matrix(), inner, q_dec, k_dec, blk)


def _mix_out_kernel(u_ref, g_ref, ys_ref, or_ref, *refs, n_streams, lat_tiles):
    mod_ref, d_ref, wglu_ref, wo_ref, o_ref = refs[n_streams:]
    y = u_ref[0].astype(F32) * d_ref[...] + ys_ref[0, 0].astype(F32) + ys_ref[1, 0].astype(F32)
    y = jax.nn.gelu(y)
    a = y * jax.nn.sigmoid(jnp.dot(y.astype(BF16), wglu_ref[...], preferred_element_type=F32))
    o = or_ref[0, 0].astype(F32) + or_ref[1, 0].astype(F32)
    heads = []
    for hd in range(RET_HEADS):
        oh = o[:, hd * RET_HEAD_DIM:(hd + 1) * RET_HEAD_DIM]
        heads.append(oh * lax.rsqrt(jnp.mean(oh * oh, axis=-1, keepdims=True) + RMS_EPS))
    r = jnp.concatenate(heads, axis=1) * jax.nn.silu(g_ref[0].astype(F32))
    m = jnp.dot(a.astype(BF16), wo_ref[:S5_WIDTH, :], preferred_element_type=F32)
    m = m + jnp.dot(r.astype(BF16), wo_ref[S5_WIDTH:, :], preferred_element_type=F32)
    o_ref[0] = _stream_tile(refs[:n_streams], lat_tiles) + mod_ref[0, 2:3, :] * m


def mix_out(streams, proj, y_s5, o_ret, mod, d_skip, w_glu, w_out, n_lat):
    bsz, nt, _ = proj.shape
    d = streams[0].shape[-1]
    tm = _token_tile(n_lat, nt)
    w = S5_WIDTH
    dirs = pl.BlockSpec((2, 1, tm, w), lambda b, j: (0, b, j, 0))
    return pl.pallas_call(
        functools.partial(_mix_out_kernel, n_streams=len(streams), lat_tiles=n_lat // tm),
        grid=(bsz, nt // tm),
        in_specs=[pl.BlockSpec((1, tm, w), lambda b, j: (b, j, 0)),
                  pl.BlockSpec((1, tm, w), lambda b, j: (b, j, 4)),
                  dirs, dirs] + _stream_specs(streams, tm, n_lat) + [
                  pl.BlockSpec((1, SUBLANES, d), _mod_index(n_lat // tm, bsz)),
                  pl.BlockSpec((1, w), lambda b, j: (0, 0)),
                  pl.BlockSpec((w, w), lambda b, j: (0, 0)),
                  pl.BlockSpec((w + RET_WIDTH, d), lambda b, j: (0, 0))],
        out_specs=pl.BlockSpec((1, tm, d), lambda b, j: (b, j, 0)),
        out_shape=jax.ShapeDtypeStruct((bsz, nt, d), F32),
        compiler_params=_params(("arbitrary", "arbitrary")),
        name="mix_out",
    )(proj, proj, y_s5, o_ret, *streams, mod, d_skip.reshape(1, w), w_glu.astype(BF16), w_out.astype(BF16))


def _na_band_start(r0, rows):
    return jnp.clip(r0 - NA_WIN_ROWS // 2, 0, rows - NA_BAND_ROWS)


def _na_bias(rpb, rows):
    w = GRID_W
    n_var = (NA_BAND_ROWS - NA_Q_ROWS) // NA_Q_ROWS + 1
    half = NA_WIN_ROWS // 2
    r0 = np.array([v * NA_Q_ROWS if v * NA_Q_ROWS <= half else rows - NA_BAND_ROWS + v * NA_Q_ROWS
                   for v in range(n_var)])
    bs = np.clip(r0 - half, 0, rows - NA_BAND_ROWS)
    assert list(r0 - bs) == [v * NA_Q_ROWS for v in range(n_var)]
    r = r0[:, None] + np.arange(NA_Q_ROWS)[None, :]
    rs = np.clip(r - half, 0, rows - NA_WIN_ROWS)
    a = bs[:, None] + np.arange(NA_BAND_ROWS)[None, :]
    row_ok = (a[:, None, :] >= rs[:, :, None]) & (a[:, None, :] < rs[:, :, None] + NA_WIN_ROWS)
    row_off = np.clip(a[:, None, :] - r[:, :, None] + (NA_WIN_ROWS - 1), 0, 2 * NA_WIN_ROWS - 2)
    col = np.arange(w)
    col_start = np.clip(col - NA_WIN_COLS // 2, 0, w - NA_WIN_COLS)
    col_ok = (col[None, :] >= col_start[:, None]) & (col[None, :] < col_start[:, None] + NA_WIN_COLS)
    col_off = col[None, :] - col[:, None] + (NA_WIN_COLS - 1)
    col_sel = (col_off[:, :, None] == np.arange(2 * NA_WIN_COLS - 1)) & col_ok[:, :, None]
    tiles = jnp.sum(rpb.astype(F32)[:, :, None, None, :] * jnp.asarray(col_sel, F32)[None, None], axis=-1)
    tiles = jnp.where(jnp.asarray(col_ok)[None, None], tiles, MASK_VALUE)
    masked = jnp.full((NA_HEADS, w, w), MASK_VALUE, F32)
    groups = []
    for v in range(n_var):
        per_row = []
        for rr in range(NA_Q_ROWS):
            per_row.append(jnp.concatenate(
                [tiles[:, int(row_off[v, rr, i])] if row_ok[v, rr, i] else masked for i in range(NA_BAND_ROWS)],
                axis=-1))
        groups.append(jnp.concatenate(per_row, axis=1))
    return jnp.stack(groups)


def _na_kernel(q_ref, kb_ref, vb_ref, kc_ref, vc_ref, bias_ref, o_ref, s_ref, p_ref):
    scale = NA_HEAD_DIM ** -0.5
    nt_dims = (((1,), (1,)), ((), ()))
    nq = q_ref.shape[1]
    n_ctx = kc_ref.shape[1]
    n_slot = s_ref.shape[0]
    heads_per_tile = LANES // NA_HEAD_DIM
    lane = lax.broadcasted_iota(jnp.int32, (nq, LANES), 1)
    for j in range(NA_WIDTH // LANES):
        sl = slice(j * LANES, (j + 1) * LANES)
        q2 = q_ref[0, :, sl].astype(F32) * scale
        k2 = kb_ref[0, :, sl]
        v2 = vb_ref[0, :, sl]
        kc2 = kc_ref[0, :, sl]
        vc2 = vc_ref[0, :, sl]
        qm = jnp.concatenate(
            [jnp.where((lane >= hh * NA_HEAD_DIM) & (lane < (hh + 1) * NA_HEAD_DIM), q2, 0.0).astype(BF16)
             for hh in range(heads_per_tile)], axis=0)
        slot = j % n_slot
        s_ref[slot, :, :n_ctx] = lax.dot_general(qm, kc2, nt_dims, preferred_element_type=F32)
        s_loc = lax.dot_general(qm, k2, nt_dims, preferred_element_type=F32)
        for hh in range(heads_per_tile):
            rows = slice(hh * nq, (hh + 1) * nq)
            s_ref[slot, rows, n_ctx:] = s_loc[rows] + bias_ref[0, j * heads_per_tile + hh]
        inv = []
        for c in range(heads_per_tile * nq // NA_SOFTMAX_ROWS):
            rows = slice(c * NA_SOFTMAX_ROWS, (c + 1) * NA_SOFTMAX_ROWS)
            s = s_ref[slot, rows, :]
            p = jnp.exp(s - jnp.max(s, axis=-1, keepdims=True))
            inv.append(1.0 / jnp.sum(p, axis=-1, keepdims=True))
            p_ref[slot, rows, :] = p.astype(BF16)
        o = jnp.dot(p_ref[slot, :, :n_ctx], vc2, preferred_element_type=F32)
        o = o + jnp.dot(p_ref[slot, :, n_ctx:], v2, preferred_element_type=F32)
        o = o * jnp.concatenate(inv, axis=0)
        o2 = o[:nq]
        for hh in range(1, heads_per_tile):
            o2 = jnp.where(lane >= hh * NA_HEAD_DIM, o[hh * nq:(hh + 1) * nq], o2)
        o_ref[0, :, sl] = o2.astype(o_ref.dtype)


def na_attention(qkv, rpb, n_lat):
    bsz, nt, _ = qkv.shape
    w = GRID_W
    rows = n_lat // w
    n_ctx = nt - n_lat
    nq = NA_Q_ROWS * w
    band = NA_BAND_ROWS * w
    n_all = n_ctx + band
    stacked = nq * (LANES // NA_HEAD_DIM)
    assert n_ctx % LANES == 0 and band % LANES == 0 and rows % NA_Q_ROWS == 0 and rows >= NA_BAND_ROWS + NA_WIN_ROWS // 2

    def band_start(g):
        return _na_band_start(g * NA_Q_ROWS, rows)

    def band_spec(j):
        return pl.BlockSpec((pl.Element(1), pl.Element(band), pl.Element(NA_WIDTH)),
                            lambda b, g: (b, band_start(g) * w, j * NA_WIDTH))

    def ctx_spec(j):
        return pl.BlockSpec((1, n_ctx, NA_WIDTH), lambda b, g: (b, n_lat // n_ctx, j))

    return pl.pallas_call(
        _na_kernel,
        grid=(bsz, rows // NA_Q_ROWS),
        in_specs=[pl.BlockSpec((1, nq, NA_WIDTH), lambda b, g: (b, g, 0)),
                  band_spec(1), band_spec(2), ctx_spec(1), ctx_spec(2),
                  pl.BlockSpec((1, NA_HEADS, nq, band), lambda b, g: (g - band_start(g) // NA_Q_ROWS, 0, 0, 0))],
        out_specs=pl.BlockSpec((1, nq, NA_WIDTH), lambda b, g: (b, g, 0)),
        out_shape=jax.ShapeDtypeStruct((bsz, n_lat, NA_WIDTH), BF16),
        scratch_shapes=[pltpu.VMEM((NA_SCORE_SLOTS, stacked, n_all), F32),
                        pltpu.VMEM((NA_SCORE_SLOTS, stacked, n_all), BF16)],
        compiler_params=_params(("arbitrary", "arbitrary")),
        name="na_attention",
    )(qkv, qkv, qkv, qkv, qkv, _na_bias(rpb, rows))


def _proj_res_kernel(a_ref, x_ref, mod_ref, w_ref, o_ref):
    m = jnp.dot(a_ref[0], w_ref[...], preferred_element_type=F32)
    o_ref[0] = x_ref[0] + mod_ref[0, 2:3, :] * m


def proj_residual(a, x, mod, w):
    bsz, n, k = a.shape
    d = x.shape[-1]
    tm = _token_tile(n, n)
    return pl.pallas_call(
        _proj_res_kernel,
        grid=(bsz, n // tm),
        in_specs=[pl.BlockSpec((1, tm, k), lambda b, j: (b, j, 0)),
                  pl.BlockSpec((1, tm, d), lambda b, j: (b, j, 0)),
                  pl.BlockSpec((1, SUBLANES, d), lambda b, j: (b, 0, 0)),
                  pl.BlockSpec((k, d), lambda b, j: (0, 0))],
        out_specs=pl.BlockSpec((1, tm, d), lambda b, j: (b, j, 0)),
        out_shape=jax.ShapeDtypeStruct((bsz, n, d), F32),
        compiler_params=_params(("arbitrary", "arbitrary")),
        name="proj_residual",
    )(a, x, mod, w.astype(BF16))


def _store_row_tiled(ref, value, index=()):
    rows, d = value.shape
    parts = d // LANES
    for j in range(parts):
        ref[index + (pl.ds(j, rows, stride=parts), slice(None))] = value[:, j * LANES:(j + 1) * LANES]


def _load_row_tiled(ref, rows, d, index=()):
    parts = d // LANES
    return jnp.concatenate([ref[index + (pl.ds(j, rows, stride=parts), slice(None))] for j in range(parts)], axis=1)


def _router_kernel(x_ref, mod_ref, nw_ref, wr_ref, br_ref, tri_ref, h_ref, idx_ref, gate_ref, rank_ref, cnt_ref,
                   run_ref):
    first_step = (pl.program_id(0) == 0) & (pl.program_id(1) == 0)

    @pl.when(first_step)
    def _():
        run_ref[...] = jnp.zeros_like(run_ref)

    h = _norm_mod(x_ref[0], nw_ref[...], mod_ref[0, 3:4, :], mod_ref[0, 4:5, :])
    _store_row_tiled(h_ref, h)
    logits = jnp.dot(h.astype(BF16), wr_ref[...], preferred_element_type=F32) + br_ref[...]
    lane = lax.broadcasted_iota(jnp.int32, logits.shape, 1)
    lane_f = lane.astype(F32)
    vals, idxs = [], []
    cur = logits
    for _ in range(MOE_TOP_K):
        m = jnp.max(cur, axis=-1, keepdims=True)
        first = jnp.min(jnp.where(cur == m, lane_f, float(LANES)), axis=-1, keepdims=True)
        vals.append(m)
        idxs.append(first)
        cur = jnp.where(lane_f == first, MASK_VALUE, cur)
    es = [jnp.exp(v - vals[0]) for v in vals]
    tot = es[0]
    for e in es[1:]:
        tot = tot + e
    onehots = [(lane_f == idxs[k]).astype(F32) for k in range(MOE_TOP_K)]
    multi = onehots[0]
    for oh in onehots[1:]:
        multi = multi + oh
    before = jnp.dot(tri_ref[...], multi.astype(BF16), preferred_element_type=F32) + run_ref[0:1, :]
    idx_out = jnp.zeros(logits.shape, F32)
    gate_out = jnp.zeros(logits.shape, F32)
    rank_out = jnp.zeros(logits.shape, F32)
    for k in range(MOE_TOP_K):
        idx_out = jnp.where(lane == k, idxs[k], idx_out)
        gate_out = jnp.where(lane == k, es[k] / tot, gate_out)
        rank_out = jnp.where(lane == k, jnp.sum(before * onehots[k], axis=-1, keepdims=True), rank_out)
    idx_ref[0] = idx_out.astype(jnp.int32)
    gate_ref[0] = gate_out
    rank_ref[0] = rank_out.astype(jnp.int32)
    run_ref[0:1, :] = run_ref[0:1, :] + jnp.sum(multi, axis=0, keepdims=True)
    cnt_ref[...] = jnp.broadcast_to(run_ref[0:1, :], cnt_ref.shape)


def router(xc, mod, norm_w, w_router, b_router, n_lat):
    bsz, nt, d = xc.shape
    tm = _token_tile(n_lat, nt)
    tiles = nt // tm
    parts = d // LANES
    wr = jnp.pad(w_router, ((0, 0), (0, LANES - N_EXPERTS))).astype(BF16)
    br = jnp.pad(b_router.astype(F32), (0, LANES - N_EXPERTS), constant_values=MASK_VALUE).reshape(1, LANES)
    tri = jnp.tril(jnp.ones((tm, tm), BF16), k=-1)
    tok = lambda n, dt: jax.ShapeDtypeStruct((bsz, nt, n), dt)
    out = lambda n: pl.BlockSpec((1, tm, n), lambda b, j: (b, j, 0))
    return pl.pallas_call(
        _router_kernel,
        grid=(bsz, nt // tm),
        in_specs=[pl.BlockSpec((1, tm, d), lambda b, j: (b, j, 0)),
                  pl.BlockSpec((1, SUBLANES, d), _mod_index(n_lat // tm, bsz)),
                  pl.BlockSpec((1, d), lambda b, j: (0, 0)),
                  pl.BlockSpec((d, LANES), lambda b, j: (0, 0)),
                  pl.BlockSpec((1, LANES), lambda b, j: (0, 0)),
                  pl.BlockSpec((tm, tm), lambda b, j: (0, 0))],
        out_specs=[pl.BlockSpec((tm * parts, LANES), lambda b, j: (b * tiles + j, 0)),
                   out(LANES), out(LANES), out(LANES),
                   pl.BlockSpec((SUBLANES, LANES), lambda b, j: (0, 0))],
        out_shape=[jax.ShapeDtypeStruct((bsz * nt * parts, LANES), F32),
                   tok(LANES, jnp.int32), tok(LANES, F32), tok(LANES, jnp.int32),
                   jax.ShapeDtypeStruct((SUBLANES, LANES), F32)],
        scratch_shapes=[pltpu.VMEM((SUBLANES, LANES), F32)],
        compiler_params=_params(("arbitrary", "arbitrary")),
        name="router",
    )(xc, mod, norm_w.reshape(1, d), wr, br, tri)


def _moe_kernel(be_ref, nu_ref, nx_ref, par_ref, x_ref, w1_ref, b1_ref, w2_ref, b2_ref, o_ref,
                w1f_ref, w2f_ref, w1b_ref, w2b_ref, sems, *, layer):
    i = pl.program_id(0)
    ff = w2_ref.shape[3]
    used = i < nu_ref[0]

    def fetch(e, slot):
        return (pltpu.make_async_copy(w1_ref.at[layer, e], w1f_ref.at[slot], sems.at[slot]),
                pltpu.make_async_copy(w2_ref.at[layer, e], w2f_ref.at[slot], sems.at[slot]))

    @pl.when(used & ((i == 0) | (be_ref[i] != be_ref[jnp.maximum(i - 1, 0)])))
    def _():
        slot = par_ref[i]

        @pl.when(i == 0)
        def _():
            for c in fetch(be_ref[i], slot):
                c.start()

        for c in fetch(be_ref[i], slot):
            c.wait()

        @pl.when(nx_ref[i] >= 0)
        def _():
            for c in fetch(nx_ref[i], 1 - slot):
                c.start()

        w1b_ref[...] = w1f_ref[slot].astype(BF16)
        w2b_ref[...] = w2f_ref[slot].astype(BF16)

    @pl.when(used)
    def _():
        d = w1b_ref.shape[0]
        x = _load_row_tiled(x_ref, MOE_TILE, d)
        t = jnp.dot(x.astype(BF16), w1b_ref[...], preferred_element_type=F32) + b1_ref[0, 0]
        x_glu = jnp.minimum(t[:, :ff], SWIGLU_LIMIT)
        x_lin = jnp.clip(t[:, ff:], -SWIGLU_LIMIT, SWIGLU_LIMIT)
        act = x_glu * jax.nn.sigmoid(SWIGLU_ALPHA * x_glu) * (x_lin + 1)
        y = jnp.dot(act.astype(BF16), w2b_ref[...], preferred_element_type=F32) + b2_ref[0, 0]
        _store_row_tiled(o_ref, y)

    @pl.when(jnp.logical_not(used))
    def _():
        o_ref[...] = jnp.zeros_like(o_ref)


def moe_experts(xs, block_e, n_used, w1, b1, w2, b2, layer):
    depth, ne, d, ff2 = w1.shape
    ff = w2.shape[2]
    tm = MOE_TILE * (d // LANES)
    nb = xs.shape[0] // tm
    experts = jnp.arange(ne, dtype=jnp.int32)
    used_block = jnp.arange(nb, dtype=jnp.int32) < n_used[0]
    present = jnp.any((block_e[None, :] == experts[:, None]) & used_block[None, :], axis=1)
    later = present[None, :] & (experts[None, :] > experts[:, None])
    next_expert = jnp.where(jnp.any(later, axis=1), jnp.argmax(later, axis=1), -1).astype(jnp.int32)
    order = (jnp.cumsum(present) - 1).astype(jnp.int32)
    nxt = next_expert[block_e]
    par = order[block_e] % 2
    grid_spec = pltpu.PrefetchScalarGridSpec(
        num_scalar_prefetch=4,
        grid=(nb,),
        in_specs=[pl.BlockSpec((tm, LANES), lambda i, be, nu, nx, pr: (jnp.where(i < nu[0], i, 0), 0)),
                  pl.BlockSpec(memory_space=pl.ANY),
                  pl.BlockSpec((1, 1, 1, ff2), lambda i, be, nu, nx, pr: (layer, be[i], 0, 0)),
                  pl.BlockSpec(memory_space=pl.ANY),
                  pl.BlockSpec((1, 1, 1, d), lambda i, be, nu, nx, pr: (layer, be[i], 0, 0))],
        out_specs=pl.BlockSpec((tm, LANES), lambda i, be, nu, nx, pr: (i, 0)),
        scratch_shapes=[pltpu.VMEM((2, d, ff2), F32), pltpu.VMEM((2, ff, d), F32),
                        pltpu.VMEM((d, ff2), BF16), pltpu.VMEM((ff, d), BF16), pltpu.SemaphoreType.DMA((2,))],
    )
    return pl.pallas_call(
        functools.partial(_moe_kernel, layer=layer),
        grid_spec=grid_spec,
        out_shape=jax.ShapeDtypeStruct(xs.shape, F32),
        compiler_params=_params(("arbitrary",)),
        name="moe_experts",
    )(block_e, n_used, nxt, par, xs, w1, b1.reshape(depth, ne, 1, ff2), w2, b2.reshape(depth, ne, 1, d))


def _moe_plan(idx, rank, counts, n_tok):
    counts = counts.astype(jnp.int32)
    padded = (counts + MOE_TILE - 1) // MOE_TILE * MOE_TILE
    padded_ends = jnp.cumsum(padded)
    padded_starts = padded_ends - padded
    onehot = idx[..., None] == jnp.arange(N_EXPERTS, dtype=jnp.int32)
    slot_of = rank + jnp.sum(jnp.where(onehot, padded_starts, 0), axis=-1)
    n_blocks = -(-n_tok * MOE_TOP_K // MOE_TILE) + N_EXPERTS
    block_row = jnp.arange(n_blocks, dtype=jnp.int32) * MOE_TILE
    block_e = jnp.minimum(jnp.sum(padded_ends[None, :] <= block_row[:, None], axis=1), N_EXPERTS - 1)
    n_used = padded_ends[-1:] // MOE_TILE
    last_block = jnp.concatenate([jnp.where(counts > 0, padded_ends - MOE_TILE, -1), n_used])
    return (slot_of.astype(jnp.int32), block_e.astype(jnp.int32), n_used.astype(jnp.int32),
            last_block.astype(jnp.int32), n_blocks)


def _dispatch_kernel(slot_ref, last_ref, h_ref, xs_ref, zero_ref, sem, *, parts):
    tm = h_ref.shape[0] // parts
    block = MOE_TILE * parts

    @pl.when(pl.program_id(0) == 0)
    def _():
        zero_ref[...] = jnp.zeros_like(zero_ref)

        def zero_copy(e):
            row = pl.multiple_of(jnp.maximum(last_ref[e], 0) * parts, block)
            return pltpu.make_async_copy(zero_ref, xs_ref.at[pl.ds(row, block)], sem)

        def block_copy(i):
            return pltpu.make_async_copy(zero_ref, xs_ref.at[pl.ds(pl.multiple_of(i * block, block), block)], sem)

        n_used = last_ref[N_EXPERTS]
        n_blocks = xs_ref.shape[0] // block
        lax.fori_loop(n_used, n_blocks, lambda i, c: (block_copy(i).start(), c)[1], 0)
        for e in range(N_EXPERTS):
            pl.when(last_ref[e] >= 0)(lambda e=e: zero_copy(e).start())
        for e in range(N_EXPERTS):
            pl.when(last_ref[e] >= 0)(lambda e=e: zero_copy(e).wait())
        lax.fori_loop(n_used, n_blocks, lambda i, c: (block_copy(i).wait(), c)[1], 0)

    def issue(t, carry):
        for k in range(MOE_TOP_K):
            s = slot_ref[0, 0, t * MOE_TOP_K + k]
            src = h_ref.at[pl.ds(pl.multiple_of(t * parts, parts), parts)]
            dst = xs_ref.at[pl.ds(pl.multiple_of(s * parts, parts), parts)]
            pltpu.make_async_copy(src, dst, sem).start(priority=k % 2)
        return carry

    lax.fori_loop(0, tm, issue, 0, unroll=DMA_ISSUE_UNROLL)
    for k in range(MOE_TOP_K):
        pltpu.make_async_copy(h_ref, xs_ref.at[pl.ds(0, tm * parts)], sem).wait()


def moe_dispatch(h, slot_tiles, last_block, n_blocks, n_tok):
    parts = h.shape[0] // n_tok
    tm = slot_tiles.shape[-1] // MOE_TOP_K
    return pl.pallas_call(
        functools.partial(_dispatch_kernel, parts=parts),
        grid=(n_tok // tm,),
        in_specs=[pl.BlockSpec((1, 1, tm * MOE_TOP_K), lambda i: (i, 0, 0), memory_space=pltpu.SMEM),
                  pl.BlockSpec(memory_space=pltpu.SMEM),
                  pl.BlockSpec((tm * parts, LANES), lambda i: (i, 0))],
        out_specs=pl.BlockSpec(memory_space=pl.ANY),
        out_shape=jax.ShapeDtypeStruct((n_blocks * MOE_TILE * parts, LANES), F32),
        scratch_shapes=[pltpu.VMEM((MOE_TILE * parts, LANES), F32), pltpu.SemaphoreType.DMA(())],
        compiler_params=_params(("arbitrary",)),
        name="moe_dispatch",
    )(slot_tiles, last_block, h)


def _combine_kernel(slot_ref, next_slot_ref, x_ref, gate_ref, mod_ref, *rest, final):
    ys_ref, o_ref, ybuf_ref, sems = rest[-4:]
    tm, d = x_ref.shape[1:]
    parts = d // LANES
    g = pl.program_id(0) * pl.num_programs(1) + pl.program_id(1)
    n_steps = pl.num_programs(0) * pl.num_programs(1)

    def fetch(slots, buf):
        def issue(t, carry):
            for k in range(MOE_TOP_K):
                s = slots[0, 0, t * MOE_TOP_K + k]
                src = ys_ref.at[pl.ds(pl.multiple_of(s * parts, parts), parts)]
                dst = ybuf_ref.at[buf * MOE_TOP_K + k, pl.ds(pl.multiple_of(t * parts, parts), parts)]
                pltpu.make_async_copy(src, dst, sems.at[buf]).start(priority=k % 2)
            return carry

        lax.fori_loop(0, tm, issue, 0, unroll=DMA_ISSUE_UNROLL)

    def combine(buf):
        for k in range(MOE_TOP_K):
            pltpu.make_async_copy(ys_ref.at[pl.ds(0, tm * parts)], ybuf_ref.at[buf * MOE_TOP_K + k], sems.at[buf]).wait()
        gate = gate_ref[0]
        y = None
        for k in range(MOE_TOP_K):
            term = gate[:, k:k + 1] * _load_row_tiled(ybuf_ref, tm, d, (buf * MOE_TOP_K + k,))
            y = term if y is None else y + term
        x = x_ref[0] + mod_ref[0, 5:6, :] * y
        if final:
            fw_ref = rest[0]
            x = x * lax.rsqrt(jnp.mean(x * x, axis=-1, keepdims=True) + RMS_EPS) * fw_ref[...]
        o_ref[0] = x

    pl.when(g == 0)(lambda: fetch(slot_ref, 0))
    for buf in range(2):
        pl.when((g + 1 < n_steps) & ((g + 1) % 2 == buf))(lambda buf=buf: fetch(next_slot_ref, buf))
    for buf in range(2):
        pl.when(g % 2 == buf)(lambda buf=buf: combine(buf))


def moe_combine(xc, ys, slot_tiles, gates, mod, n_lat, final_w=None):
    bsz, nt, d = xc.shape
    tm = slot_tiles.shape[-1] // MOE_TOP_K
    tiles = nt // tm
    last = bsz * tiles - 1
    slot_spec = lambda ahead: pl.BlockSpec((1, 1, tm * MOE_TOP_K),
                                           lambda b, j: (jnp.minimum(b * tiles + j + ahead, last), 0, 0),
                                           memory_space=pltpu.SMEM)
    in_specs = [slot_spec(0), slot_spec(1),
                pl.BlockSpec((1, tm, d), lambda b, j: (b, j, 0)),
                pl.BlockSpec((1, tm, LANES), lambda b, j: (b, j, 0)),
                pl.BlockSpec((1, SUBLANES, d), _mod_index(n_lat // tm, bsz))]
    args = [slot_tiles, slot_tiles, xc, gates, mod]
    if final_w is not None:
        in_specs.append(pl.BlockSpec((1, d), lambda b, j: (0, 0)))
        args.append(final_w.reshape(1, d))
    in_specs.append(pl.BlockSpec(memory_space=pl.ANY))
    args.append(ys)
    return pl.pallas_call(
        functools.partial(_combine_kernel, final=final_w is not None),
        grid=(bsz, tiles),
        in_specs=in_specs,
        out_specs=pl.BlockSpec((1, tm, d), lambda b, j: (b, j, 0)),
        out_shape=jax.ShapeDtypeStruct((bsz, nt, d), F32),
        scratch_shapes=[pltpu.VMEM((2 * MOE_TOP_K, tm * (d // LANES), LANES), F32), pltpu.SemaphoreType.DMA((2,))],
        compiler_params=_params(("arbitrary", "arbitrary")),
        name="moe_combine",
    )(*args)


def moe_layer(xc, mod, norm_w, w_router, b_router, experts, n_lat, final_w=None):
    bsz, nt, d = xc.shape
    n_tok = bsz * nt
    h, idx, gates, rank, counts = router(xc, mod, norm_w, w_router, b_router, n_lat)
    top = lambda a: a.reshape(n_tok, LANES)[:, :MOE_TOP_K]
    slot_of, block_e, n_used, last_block, n_blocks = _moe_plan(top(idx), top(rank), counts[0, :N_EXPERTS], n_tok)
    tiled = lambda tm: slot_of.reshape(n_tok // tm, 1, tm * MOE_TOP_K)
    xs = moe_dispatch(h, tiled(_token_tile(n_tok, n_tok)), last_block, n_blocks, n_tok)
    ys = moe_experts(xs, block_e, n_used, *experts)
    return moe_combine(xc, ys, tiled(_token_tile(n_lat, nt, COMBINE_TILE)), gates, mod, n_lat, final_w)


def kernel(x, c, ctx, c_ctx, ada_w, ada_b, norm_w, final_norm_w, ev_w_in, ev_w_out, s5_lam_re, s5_lam_im, s5_log_step, s5_b_re, s5_b_im, s5_c_re, s5_c_im, s5_d, s5_w_glu, ret_log_decay, na_w_qkv, na_w_o, na_rpb, moe_w_router, moe_b_router, moe_w1, moe_b1, moe_w2, moe_b2):
    bsz, n_lat, d = x.shape
    nt = n_lat + ctx.shape[1]
    depth = ada_w.shape[0]
    mod = adaln_table(c, c_ctx, ada_w, ada_b)
    streams = (x, ctx)
    for i in range(depth):
        last = i == depth - 1
        j = i // 2
        if i % 2 == 0:
            proj = norm_proj(streams, mod[i], norm_w[i, 0], ev_w_in[j], n_lat, nt, 0)
            s5p = _s5_layout(s5_lam_re[j], s5_lam_im[j], s5_log_step[j], s5_b_re[j], s5_b_im[j],
                             s5_c_re[j], s5_c_im[j])
            y_s5 = s5_scan(proj, s5p, n_lat)
            o_ret = retention(proj, ret_log_decay[j], n_lat)
            xc = mix_out(streams, proj, y_s5, o_ret, mod[i], s5_d[j], s5_w_glu[j], ev_w_out[j], n_lat)
        else:
            qkv = norm_proj(streams, mod[i], norm_w[i, 0], na_w_qkv[j], n_lat, nt, 0)
            att = na_attention(qkv, na_rpb[j], n_lat)
            assert last, "an odd layer is only supported as the final layer (no context output needed)"
            xc = proj_residual(att, streams[0], mod[i], na_w_o[j])
        experts = (moe_w1, moe_b1, moe_w2, moe_b2, i)
        if last:
            return moe_layer(xc[:, :n_lat], mod[i], norm_w[i, 1], moe_w_router[i], moe_b_router[i],
                             experts, n_lat, final_norm_w)
        streams = (moe_layer(xc, mod[i], norm_w[i, 1], moe_w_router[i], moe_b_router[i], experts, n_lat),)
```
